```python
import jax
import jax.numpy as jnp
from jax import lax
import numpy as np

D_MODEL = 1024
BATCH = 8
SEQ = 8192
DEPTH = 2

GRID_W = 64
CTX_LEN = 256
EPS = 1e-6
N_MOD = 6

ATTN_HEADS = 8
ATTN_KV_HEADS = 2
HEAD_DIM = 64
AXIS_DIM = HEAD_DIM // 2
ROPE_THETA = 10000.0
Q_BLOCK = 128

SSM_HEADS = 8
SSM_HEAD_DIM = 64
SSM_D_INNER = SSM_HEADS * SSM_HEAD_DIM
SSM_GROUPS = 2
SSM_STATE = 64
SSM_CONV_W = 3
SSM_CHUNK = 128
SSM_CONV_DIM = SSM_D_INNER + 2 * SSM_GROUPS * SSM_STATE

Q_DIM = ATTN_HEADS * HEAD_DIM
KV_DIM = ATTN_KV_HEADS * HEAD_DIM
IDX_K = Q_DIM
IDX_V = IDX_K + KV_DIM
IDX_Z = IDX_V + KV_DIM
IDX_XBC = IDX_Z + SSM_D_INNER
IDX_DT = IDX_XBC + SSM_CONV_DIM
IN_PROJ_DIM = IDX_DT + 2 * SSM_HEADS
MIX_OUT_DIM = Q_DIM + SSM_D_INNER

SC_CONV_W = 3

FFN_HIDDEN = -(-(8 * D_MODEL) // (3 * 256)) * 256

N_EVEN = (DEPTH + 1) // 2
N_ODD = DEPTH // 2

kernel_name = "hybrid_gqa_ssd_shortconv_dit_block"


def rms_norm(x, g):
    xf = x.astype(jnp.float32)
    y = xf * lax.rsqrt(jnp.mean(xf * xf, axis=-1, keepdims=True) + EPS)
    return (y * g.astype(jnp.float32)).astype(x.dtype)


def modulate(h, shift, scale):
    return h * (1 + scale) + shift


def swiglu(h, w_gate, w_up, w_down):
    return (jax.nn.silu(h @ w_gate) * (h @ w_up)) @ w_down


def dw_conv(x, w, b=None):
    y = lax.conv_general_dilated(
        x, w[:, None, :].astype(x.dtype), window_strides=(1,), padding='SAME',
        dimension_numbers=('NWC', 'WIO', 'NWC'), feature_group_count=x.shape[-1])
    if b is not None:
        y = y + b
    return y


def axial_rope_tables(n_tokens):
    rows = n_tokens // GRID_W
    row = jnp.repeat(jnp.arange(rows), GRID_W).astype(jnp.float32)
    col = jnp.tile(jnp.arange(GRID_W), rows).astype(jnp.float32)
    inv = 1.0 / (ROPE_THETA ** (jnp.arange(0, AXIS_DIM, 2, dtype=jnp.float32) / AXIS_DIM))
    ang = jnp.concatenate([row[:, None] * inv, col[:, None] * inv], axis=-1)
    return jnp.cos(ang), jnp.sin(ang)


def apply_rope(x, cos, sin):
    xf = x.astype(jnp.float32).reshape(x.shape[:-1] + (HEAD_DIM // 2, 2))
    x0, x1 = xf[..., 0], xf[..., 1]
    cs, sn = cos[None, :, None, :], sin[None, :, None, :]
    out = jnp.stack([x0 * cs - x1 * sn, x0 * sn + x1 * cs], axis=-1)
    return out.reshape(x.shape).astype(x.dtype)


def attn_heads(p, q_norm, k_norm):
    b, t = p.shape[:2]
    q = rms_norm(p[..., :IDX_K].reshape(b, t, ATTN_HEADS, HEAD_DIM), q_norm)
    k = rms_norm(p[..., IDX_K:IDX_V].reshape(b, t, ATTN_KV_HEADS, HEAD_DIM), k_norm)
    v = p[..., IDX_V:IDX_Z].reshape(b, t, ATTN_KV_HEADS, HEAD_DIM)
    return q, k, v


def attend_blocks(q, keys, vals):
    b, t = q.shape[:2]
    grp = ATTN_HEADS // ATTN_KV_HEADS
    nb = t // Q_BLOCK
    qb = q.reshape(b, nb, Q_BLOCK, ATTN_KV_HEADS, grp, HEAD_DIM).transpose(1, 0, 2, 3, 4, 5)
    scale = HEAD_DIM ** -0.5

    def block(qi):
        s = jnp.einsum('bqkgd,bskd->bkgqs', qi, keys).astype(jnp.float32) * scale
        pr = jax.nn.softmax(s, axis=-1).astype(vals.dtype)
        return jnp.einsum('bkgqs,bskd->bqkgd', pr, vals)

    o = lax.map(block, qb)
    return o.transpose(1, 0, 2, 3, 4, 5).reshape(b, t, Q_DIM)


def segsum_exp(a):
    l = a.shape[-1]
    mask = jnp.tril(jnp.ones((l, l), dtype=bool))
    return jnp.exp(jnp.where(mask, a[..., :, None] - a[..., None, :], -jnp.inf))


def ssd_scan(xs, dt, a, bm, cm, h0):
    b, t, nh, p = xs.shape
    g, n = bm.shape[2], bm.shape[3]
    e = nh // g
    nc, l = t // SSM_CHUNK, SSM_CHUNK
    f32 = jnp.float32
    x = xs.astype(f32).reshape(b, nc, l, g, e, p)
    dtc = dt.astype(f32).reshape(b, nc, l, g, e)
    bc = bm.astype(f32).reshape(b, nc, l, g, n)
    cc = cm.astype(f32).reshape(b, nc, l, g, n)
    a_cs = jnp.cumsum(dtc * a.astype(f32).reshape(g, e), axis=2).transpose(0, 3, 4, 1, 2)
    xdt = x * dtc[..., None]
    cb = jnp.einsum('bclgn,bcsgn->bcgls', cc, bc)
    y_diag = jnp.einsum('bcgls,bgecls,bcsgep->bclgep', cb, segsum_exp(a_cs), xdt)
    decay_to_end = jnp.exp(a_cs[..., -1:] - a_cs)
    states = jnp.einsum('bclgn,bgecl,bclgep->bcgepn', bc, decay_to_end, xdt)
    chunk_decay = jnp.exp(a_cs[..., -1]).transpose(3, 0, 1, 2)

    def step(h, inp):
        dcy, st = inp
        return h * dcy[..., None, None] + st, h

    h_init = h0.astype(f32).reshape(b, g, e, p, n)
    h_final, h_in = lax.scan(step, h_init, (chunk_decay, states.transpose(1, 0, 2, 3, 4, 5)))
    y_off = jnp.einsum('bclgn,bgecl,cbgepn->bclgep', cc, jnp.exp(a_cs), h_in)
    y = (y_diag + y_off).reshape(b, t, nh, p)
    return y.astype(xs.dtype), h_final.reshape(b, nh, p, n)


def ssd_mixer(p, conv_w, conv_b, dt_bias, a_log, d_skip, ssm_norm, h0_f, h0_b):
    b, t = p.shape[:2]
    z = p[..., IDX_Z:IDX_XBC]
    xbc = jax.nn.silu(dw_conv(p[..., IDX_XBC:IDX_DT], conv_w, conv_b))
    xs = xbc[..., :SSM_D_INNER].reshape(b, t, SSM_HEADS, SSM_HEAD_DIM)
    bm = xbc[..., SSM_D_INNER:SSM_D_INNER + SSM_GROUPS * SSM_STATE].reshape(b, t, SSM_GROUPS, SSM_STATE)
    cm = xbc[..., SSM_D_INNER + SSM_GROUPS * SSM_STATE:].reshape(b, t, SSM_GROUPS, SSM_STATE)
    dt = jax.nn.softplus(p[..., IDX_DT:].astype(jnp.float32).reshape(b, t, 2, SSM_HEADS)
                         + dt_bias.astype(jnp.float32))
    a = -jnp.exp(a_log.astype(jnp.float32))
    y_f, h_f = ssd_scan(xs, dt[:, :, 0], a[0], bm, cm, h0_f)
    flip = lambda u: jnp.flip(u, axis=1)
    y_b, h_b = ssd_scan(flip(xs), flip(dt[:, :, 1]), a[1], flip(bm), flip(cm), h0_b)
    y = (y_f + flip(y_b) + xs * d_skip[:, None]).reshape(b, t, SSM_D_INNER)
    gy = (y.astype(jnp.float32) * jax.nn.silu(z.astype(jnp.float32))).reshape(b, t, SSM_GROUPS, -1)
    gy = gy * lax.rsqrt(jnp.mean(gy * gy, axis=-1, keepdims=True) + EPS)
    out = gy.reshape(b, t, SSM_D_INNER) * ssm_norm.astype(jnp.float32)
    return out.astype(p.dtype), h_f, h_b


def hybrid_mixer(h_lat, h_ctx, w_in, q_norm, k_norm, conv_w, conv_b, dt_bias, a_log, d_skip,
                 ssm_norm, w_out, cos, sin, ctx_out):
    p_lat = h_lat @ w_in
    p_ctx = h_ctx @ w_in
    qc, kc, vc = attn_heads(p_ctx, q_norm, k_norm)
    q, k, v = attn_heads(p_lat, q_norm, k_norm)
    q, k = apply_rope(q, cos, sin), apply_rope(k, cos, sin)
    a_lat = attend_blocks(q, jnp.concatenate([k, kc], axis=1), jnp.concatenate([v, vc], axis=1))
    b = h_lat.shape[0]
    zeros = jnp.zeros((b, SSM_HEADS, SSM_HEAD_DIM, SSM_STATE), jnp.float32)
    y_ctx, hf_ctx, hb_ctx = ssd_mixer(p_ctx, conv_w, conv_b, dt_bias, a_log, d_skip, ssm_norm, zeros, zeros)
    y_lat, _, _ = ssd_mixer(p_lat, conv_w, conv_b, dt_bias, a_log, d_skip, ssm_norm, hf_ctx, hb_ctx)
    m_lat = jnp.concatenate([a_lat, y_lat], axis=-1) @ w_out
    m_ctx = None
    if ctx_out:
        a_ctx = attend_blocks(qc, kc, vc)
        m_ctx = jnp.concatenate([a_ctx, y_ctx], axis=-1) @ w_out
    return m_lat, m_ctx


def shortconv_mixer(h, w_in, conv_w, w_out):
    gb, gc, u = jnp.split(h @ w_in, 3, axis=-1)
    return (gb * dw_conv(gc * u, conv_w)) @ w_out


def setup_inputs(seed: int = 0) -> dict:
    key = jax.random.key(seed)
    ks = iter(jax.random.split(key, 40))
    f32 = jnp.float32
    d = D_MODEL

    def nrm(shape, fan_in):
        return jax.random.normal(next(ks), shape, f32) * (fan_in ** -0.5)

    def gain(shape):
        return 1.0 + 0.05 * jax.random.normal(next(ks), shape, f32)

    x = jax.random.normal(next(ks), (BATCH, SEQ, d), f32)
    c = jax.random.normal(next(ks), (BATCH, d), f32)
    ctx = jax.random.normal(next(ks), (BATCH, CTX_LEN, d), f32)
    c_ctx = jax.random.normal(next(ks), (d,), f32)
    ada_w = 0.5 * nrm((DEPTH, d, N_MOD * d), d)
    ada_b = 0.02 * jax.random.normal(next(ks), (DEPTH, N_MOD * d), f32)
    norm_mix = gain((DEPTH, d))
    norm_ffn = gain((DEPTH, d))
    ffn_w_gate = nrm((DEPTH, d, FFN_HIDDEN), d)
    ffn_w_up = nrm((DEPTH, d, FFN_HIDDEN), d)
    ffn_w_down = nrm((DEPTH, FFN_HIDDEN, d), FFN_HIDDEN)
    hy_w_in = nrm((N_EVEN, d, IN_PROJ_DIM), d)
    hy_q_norm = gain((N_EVEN, HEAD_DIM))
    hy_k_norm = gain((N_EVEN, HEAD_DIM))
    hy_conv_w = nrm((N_EVEN, SSM_CONV_W, SSM_CONV_DIM), SSM_CONV_W)
    hy_conv_b = 0.02 * jax.random.normal(next(ks), (N_EVEN, SSM_CONV_DIM), f32)
    dt0 = jnp.exp(jax.random.uniform(next(ks), (N_EVEN, 2, SSM_HEADS), f32,
                                     np.log(1e-3).astype(np.float32), np.log(1e-1).astype(np.float32)))
    hy_dt_bias = dt0 + jnp.log(-jnp.expm1(-dt0))
    hy_a_log = jnp.log(jax.random.uniform(next(ks), (N_EVEN, 2, SSM_HEADS), f32, 1.0, 16.0))
    hy_d_skip = gain((N_EVEN, SSM_HEADS))
    hy_ssm_norm = gain((N_EVEN, SSM_D_INNER))
    hy_w_out = nrm((N_EVEN, MIX_OUT_DIM, d), MIX_OUT_DIM)
    sc_w_in = nrm((N_ODD, d, 3 * d), d)
    sc_conv_w = nrm((N_ODD, SC_CONV_W, d), SC_CONV_W)
    sc_w_out = nrm((N_ODD, d, d), d)
    final_norm = gain((d,))
    return {
        'x': x, 'c': c, 'ctx': ctx, 'c_ctx': c_ctx,
        'ada_w': ada_w, 'ada_b': ada_b, 'norm_mix': norm_mix, 'norm_ffn': norm_ffn,
        'ffn_w_gate': ffn_w_gate, 'ffn_w_up': ffn_w_up, 'ffn_w_down': ffn_w_down,
        'hy_w_in': hy_w_in, 'hy_q_norm': hy_q_norm, 'hy_k_norm': hy_k_norm,
        'hy_conv_w': hy_conv_w, 'hy_conv_b': hy_conv_b, 'hy_dt_bias': hy_dt_bias,
        'hy_a_log': hy_a_log, 'hy_d_skip': hy_d_skip, 'hy_ssm_norm': hy_ssm_norm, 'hy_w_out': hy_w_out,
        'sc_w_in': sc_w_in, 'sc_conv_w': sc_conv_w, 'sc_w_out': sc_w_out,
        'final_norm': final_norm,
    }


def reference(x, c, ctx, c_ctx, ada_w, ada_b, norm_mix, norm_ffn, ffn_w_gate, ffn_w_up, ffn_w_down,
              hy_w_in, hy_q_norm, hy_k_norm, hy_conv_w, hy_conv_b, hy_dt_bias, hy_a_log, hy_d_skip,
              hy_ssm_norm, hy_w_out, sc_w_in, sc_conv_w, sc_w_out, final_norm):
    cos, sin = axial_rope_tables(x.shape[1])
    h_ctx = ctx
    for i in range(DEPTH):
        ctx_read_later = any(j % 2 == 0 for j in range(i + 1, DEPTH))
        mod = (jax.nn.silu(c) @ ada_w[i] + ada_b[i])[:, None, :]
        sh1, sc1, g1, sh2, sc2, g2 = jnp.split(mod, N_MOD, axis=-1)
        hx = modulate(rms_norm(x, norm_mix[i]), sh1, sc1)
        need_ctx_in = (i % 2 == 0) or ctx_read_later
        if need_ctx_in:
            mod_c = (jax.nn.silu(c_ctx) @ ada_w[i] + ada_b[i])[None, None, :]
            csh1, csc1, cg1, csh2, csc2, cg2 = jnp.split(mod_c, N_MOD, axis=-1)
            hc = modulate(rms_norm(h_ctx, norm_mix[i]), csh1, csc1)
        if i % 2 == 0:
            e = i // 2
            m_lat, m_ctx = hybrid_mixer(hx, hc, hy_w_in[e], hy_q_norm[e], hy_k_norm[e], hy_conv_w[e],
                                        hy_conv_b[e], hy_dt_bias[e], hy_a_log[e], hy_d_skip[e],
                                        hy_ssm_norm[e], hy_w_out[e], cos, sin, ctx_read_later)
        else:
            o = i // 2
            m_lat = shortconv_mixer(hx, sc_w_in[o], sc_conv_w[o], sc_w_out[o])
            m_ctx = shortconv_mixer(hc, sc_w_in[o], sc_conv_w[o], sc_w_out[o]) if ctx_read_later else None
        x = x + g1 * m_lat
        x = x + g2 * swiglu(modulate(rms_norm(x, norm_ffn[i]), sh2, sc2),
                            ffn_w_gate[i], ffn_w_up[i], ffn_w_down[i])
        if ctx_read_later:
            h_ctx = h_ctx + cg1 * m_ctx
            h_ctx = h_ctx + cg2 * swiglu(modulate(rms_norm(h_ctx, norm_ffn[i]), csh2, csc2),
                                         ffn_w_gate[i], ffn_w_up[i], ffn_w_down[i])
    return rms_norm(x, final_norm)
```

```python
import functools

import jax
import jax.numpy as jnp
import numpy as np
from jax import lax
from jax.experimental import pallas as pl
from jax.experimental.pallas import tpu as pltpu

F32 = jnp.float32
BF16 = jnp.bfloat16

EPS = 1e-6
N_MOD = 6
GRID_W = 64
ROPE_THETA = 10000.0

ATTN_HEADS = 8
ATTN_KV_HEADS = 2
HEAD_DIM = 64
AXIS_DIM = HEAD_DIM // 2
KV_GROUP = ATTN_HEADS // ATTN_KV_HEADS
Q_DIM = ATTN_HEADS * HEAD_DIM
KV_DIM = ATTN_KV_HEADS * HEAD_DIM

SSM_HEADS = 8
SSM_HEAD_DIM = 64
SSM_D_INNER = SSM_HEADS * SSM_HEAD_DIM
SSM_GROUPS = 2
SSM_STATE = 64
SSM_CHUNK = 128
SSM_CONV_DIM = SSM_D_INNER + 2 * SSM_GROUPS * SSM_STATE
SSM_HEADS_PER_GROUP = SSM_HEADS // SSM_GROUPS

LANES = 128
SUBLANES = 8
XBCDT_W = SSM_CONV_DIM + LANES
NEG_BIG = -1e30

VMEM_LIMIT = 56 * 1024 * 1024


def _cparams(n_axes):
    return pltpu.CompilerParams(dimension_semantics=("parallel",) * n_axes, vmem_limit_bytes=VMEM_LIMIT)


def _const_spec(shape):
    nd = len(shape)
    return pl.BlockSpec(shape, lambda *_: (0,) * nd, pipeline_mode=pl.Buffered(1))


def _rms(x):
    return x * lax.rsqrt(jnp.mean(x * x, axis=-1, keepdims=True) + EPS)


def _silu(x):
    return x * (1.0 / (1.0 + jnp.exp(-x)))


def _dot(a, b):
    return jnp.dot(a, b, preferred_element_type=F32)


def _dot_exact(a, b):
    return jnp.dot(a, b, preferred_element_type=F32, precision=lax.Precision.HIGHEST)


def _mod_kernel(c_ref, w_ref, b_ref, o_ref):
    c = c_ref[...]
    o_ref[0] = _dot_exact(_silu(c), w_ref[0]) + b_ref[0]


def _modulation(cc, ada_w, ada_b):
    depth, d, nd = ada_w.shape
    r = cc.shape[0]
    tn = d
    return pl.pallas_call(
        _mod_kernel,
        grid=(depth, nd // tn),
        in_specs=[
            pl.BlockSpec((r, d), lambda l, j: (0, 0)),
            pl.BlockSpec((1, d, tn), lambda l, j: (l, 0, j)),
            pl.BlockSpec((1, 1, tn), lambda l, j: (l, 0, j)),
        ],
        out_specs=pl.BlockSpec((1, r, tn), lambda l, j: (l, 0, j)),
        out_shape=jax.ShapeDtypeStruct((depth, r, nd), F32),
        compiler_params=_cparams(2),
        name="mod",
    )(cc, ada_w, ada_b.reshape(depth, 1, nd))


def _inproj_kernel(x_ref, mod_ref, nw_ref, wn_ref, wt_ref, aq_ref, bq_ref, ck_ref, sk_ref, bd_ref,
                   qt_ref, vt_ref, k_ref, z_ref, xb_ref):
    x = x_ref[0]
    m = mod_ref[0]
    h = _rms(x) * nw_ref[...]
    h = h * (1.0 + m[1:2]) + m[0:1]
    hb = h.astype(BF16)
    pn = _dot(hb, wn_ref[...])
    pt = lax.dot_general(wt_ref[...], hb, (((1,), (1,)), ((), ())), preferred_element_type=F32)
    tm = x.shape[0]

    kx = pn[:, :KV_DIM]
    sq = kx * kx
    hi = sq.astype(BF16)
    lo = (sq - hi.astype(F32)).astype(BF16)
    ss = _dot(hi, bd_ref[...]) + _dot(lo, bd_ref[...])
    rs = lax.rsqrt(ss * (1.0 / HEAD_DIM) + EPS)
    lane = lax.broadcasted_iota(jnp.int32, kx.shape, 1)
    half = HEAD_DIM // 2
    swap = jnp.where((lane & half) == 0, pltpu.roll(kx, KV_DIM - half, 1), pltpu.roll(kx, half, 1))
    k_ref[0] = (rs * (kx * ck_ref[...] + swap * sk_ref[...])).astype(BF16)

    q3 = pt[:Q_DIM].reshape(ATTN_HEADS, HEAD_DIM, tm)
    qrs = lax.rsqrt(jnp.mean(q3 * q3, axis=1, keepdims=True) + EPS)
    qsw = jnp.concatenate([q3[:, half:], q3[:, :half]], axis=1)
    qo = qrs * (q3 * aq_ref[...][None] + qsw * bq_ref[...][None])
    qt_ref[0] = qo.reshape(Q_DIM, tm).astype(BF16)

    vt_ref[0, 0] = pt[Q_DIM:].astype(BF16)
    z_ref[0] = pn[:, KV_DIM:KV_DIM + SSM_D_INNER]
    xb_ref[0] = pn[:, KV_DIM + SSM_D_INNER:]


def _inproj(x, mod, per_batch_mod, nw, wn, wt, aq, bq, ck, sk, bd, tm):
    b, t, d = x.shape
    nt = t // tm
    mod_idx = (lambda bi, i: (bi, 0, 0)) if per_batch_mod else (lambda bi, i: (0, 0, 0))
    wn_cols = wn.shape[1]
    return pl.pallas_call(
        _inproj_kernel,
        grid=(b, nt),
        in_specs=[
            pl.BlockSpec((1, tm, d), lambda bi, i: (bi, i, 0)),
            pl.BlockSpec((1, N_MOD, d), mod_idx),
            _const_spec((1, d)),
            _const_spec((d, wn_cols)),
            _const_spec((Q_DIM + KV_DIM, d)),
            pl.BlockSpec((HEAD_DIM, tm), lambda bi, i: (0, i)),
            pl.BlockSpec((HEAD_DIM, tm), lambda bi, i: (0, i)),
            pl.BlockSpec((tm, KV_DIM), lambda bi, i: (i, 0)),
            pl.BlockSpec((tm, KV_DIM), lambda bi, i: (i, 0)),
            _const_spec((KV_DIM, KV_DIM)),
        ],
        out_specs=[
            pl.BlockSpec((1, Q_DIM, tm), lambda bi, i: (bi, 0, i)),
            pl.BlockSpec((1, 1, KV_DIM, tm), lambda bi, i: (bi, i, 0, 0)),
            pl.BlockSpec((1, tm, KV_DIM), lambda bi, i: (bi, i, 0)),
            pl.BlockSpec((1, tm, SSM_D_INNER), lambda bi, i: (bi, i, 0)),
            pl.BlockSpec((1, tm, XBCDT_W), lambda bi, i: (bi, i, 0)),
        ],
        out_shape=[
            jax.ShapeDtypeStruct((b, Q_DIM, t), BF16),
            jax.ShapeDtypeStruct((b, nt, KV_DIM, tm), BF16),
            jax.ShapeDtypeStruct((b, t, KV_DIM), BF16),
            jax.ShapeDtypeStruct((b, t, SSM_D_INNER), F32),
            jax.ShapeDtypeStruct((b, t, XBCDT_W), F32),
        ],
        compiler_params=_cparams(2),
        name="inproj",
    )(x, mod, nw, wn, wt, aq, bq, ck, sk, bd)


def _attn_kernel(qt_ref, k_ref, kc_ref, vt_ref, vct_ref, o_ref, *, kt):
    g = pl.program_id(1)
    tq = qt_ref.shape[2]
    cols = KV_GROUP * tq
    q4 = qt_ref[0]
    qcat = jnp.concatenate([q4[h * HEAD_DIM:(h + 1) * HEAD_DIM] for h in range(KV_GROUP)], axis=1)
    qp = jnp.concatenate([qcat] * ATTN_KV_HEADS, axis=0)
    rowgrp = lax.broadcasted_iota(jnp.int32, qp.shape, 0) // HEAD_DIM
    qp = jnp.where(rowgrp == g, qp, jnp.zeros_like(qp))

    def step(carry, ktile, vtile):
        m, l, acc = carry
        s = _dot(ktile, qp)
        m_new = jnp.maximum(m, jnp.max(s, axis=0, keepdims=True))
        alpha = jnp.exp(m - m_new)
        p = jnp.exp(s - m_new)
        l = alpha * l + jnp.sum(p, axis=0, keepdims=True)
        acc = alpha * acc + _dot(vtile, p.astype(BF16))
        return m_new, l, acc

    vt_tile = vt_ref.shape[3]
    n_sub = vt_tile // kt

    def body(j, carry):
        vt = vt_ref[0, j]
        for sidx in range(n_sub):
            start = pl.multiple_of(j * vt_tile + sidx * kt, kt)
            carry = step(carry, k_ref[0, pl.ds(start, kt), :], vt[:, sidx * kt:(sidx + 1) * kt])
        return carry

    init = (jnp.full((1, cols), NEG_BIG, F32), jnp.zeros((1, cols), F32), jnp.zeros((HEAD_DIM, cols), F32))
    carry = lax.fori_loop(0, vt_ref.shape[1], body, init)
    nc = kc_ref.shape[1]
    for sidx in range(nc // kt):
        carry = step(carry, kc_ref[0, sidx * kt:(sidx + 1) * kt, :], vct_ref[0, 0][:, sidx * kt:(sidx + 1) * kt])
    _, l, acc = carry
    o = acc * (1.0 / l)
    o4 = jnp.concatenate([o[:, h * tq:(h + 1) * tq] for h in range(KV_GROUP)], axis=0)
    o_ref[0] = o4.T.astype(BF16)


def _attention(qt, k, kc, vt, vct, tq, kt):
    b, _, t = qt.shape
    nt, vt_tile = vt.shape[1], vt.shape[3]
    nctx = kc.shape[1]
    gw = KV_GROUP * HEAD_DIM
    return pl.pallas_call(
        functools.partial(_attn_kernel, kt=kt),
        grid=(b, ATTN_KV_HEADS, t // tq),
        in_specs=[
            pl.BlockSpec((1, gw, tq), lambda bi, g, i: (bi, g, i)),
            pl.BlockSpec((1, t, KV_DIM), lambda bi, g, i: (bi, 0, 0)),
            pl.BlockSpec((1, nctx, KV_DIM), lambda bi, g, i: (bi, 0, 0)),
            pl.BlockSpec((1, nt, HEAD_DIM, vt_tile), lambda bi, g, i: (bi, 0, g, 0)),
            pl.BlockSpec((1, 1, HEAD_DIM, nctx), lambda bi, g, i: (bi, 0, g, 0)),
        ],
        out_specs=pl.BlockSpec((1, tq, gw), lambda bi, g, i: (bi, i, g)),
        out_shape=jax.ShapeDtypeStruct((b, t, Q_DIM), BF16),
        compiler_params=_cparams(3),
        name="attn",
    )(qt, k, kc, vt, vct)


def _ssdprep_kernel(x_ref, xp_ref, xn_ref, w_ref, b_ref, dtb_ref, u_ref, dt_ref, sc_ref):
    i = pl.program_id(1)
    last = pl.num_programs(1) - 1
    tp = x_ref.shape[1]
    x = x_ref[0]
    sc_ref[0:SUBLANES] = jnp.where(i > 0, xp_ref[0], 0.0)
    sc_ref[SUBLANES:SUBLANES + tp] = x
    sc_ref[SUBLANES + tp:] = jnp.where(i < last, xn_ref[0], 0.0)
    prev = sc_ref[SUBLANES - 1:SUBLANES - 1 + tp]
    nxt = sc_ref[SUBLANES + 1:SUBLANES + 1 + tp]
    w = w_ref[...]
    y = prev * w[0:1] + x * w[1:2] + nxt * w[2:3] + b_ref[...]
    u_ref[0] = _silu(y[:, :SSM_CONV_DIM])
    d = x[:, SSM_CONV_DIM:] + dtb_ref[...]
    dt_ref[0] = jnp.maximum(d, 0.0) + jnp.log1p(jnp.exp(-jnp.abs(d)))


def _ssdprep(xb, w, bias, dtb, tp):
    b, t, wd = xb.shape
    r8 = tp // SUBLANES
    nblk8 = t // SUBLANES
    return pl.pallas_call(
        _ssdprep_kernel,
        grid=(b, t // tp),
        in_specs=[
            pl.BlockSpec((1, tp, wd), lambda bi, i: (bi, i, 0)),
            pl.BlockSpec((1, SUBLANES, wd), lambda bi, i: (bi, jnp.maximum(i * r8 - 1, 0), 0)),
            pl.BlockSpec((1, SUBLANES, wd), lambda bi, i: (bi, jnp.minimum((i + 1) * r8, nblk8 - 1), 0)),
            _const_spec((3, wd)),
            _const_spec((1, wd)),
            _const_spec((1, LANES)),
        ],
        out_specs=[
            pl.BlockSpec((1, tp, SSM_CONV_DIM), lambda bi, i: (bi, i, 0)),
            pl.BlockSpec((1, tp, LANES), lambda bi, i: (bi, i, 0)),
        ],
        out_shape=[
            jax.ShapeDtypeStruct((b, t, SSM_CONV_DIM), F32),
            jax.ShapeDtypeStruct((b, t, LANES), F32),
        ],
        scratch_shapes=[pltpu.VMEM((tp + 2 * SUBLANES, wd), F32)],
        compiler_params=_cparams(2),
        name="ssdprep",
    )(xb, xb, xb, w, bias, dtb)


def _ssd_chunk(u, dt, a_row, h_ref, direction):
    ln = u.shape[0]
    xs = u[:, :SSM_D_INNER]
    bm = u[:, SSM_D_INNER:SSM_D_INNER + SSM_GROUPS * SSM_STATE]
    cm = u[:, SSM_D_INNER + SSM_GROUPS * SSM_STATE:]
    da = dt * a_row
    ri = lax.broadcasted_iota(jnp.int32, (ln, ln), 0)
    ci = lax.broadcasted_iota(jnp.int32, (ln, ln), 1)
    lower = (ri >= ci)
    upper = (ri <= ci)
    causal = lower if direction == 0 else upper
    tri = causal.astype(F32)
    cs_col = _dot_exact(tri, da)
    tri_t = (upper if direction == 0 else lower).astype(F32)
    cs_row = _dot_exact(da.T, tri_t)
    end = ln - 1 if direction == 0 else 0
    bt = bm.T
    bt16 = bt.astype(BF16)
    cm16 = cm.astype(BF16)
    ys = []
    for grp in range(SSM_GROUPS):
        gs = slice(grp * SSM_STATE, (grp + 1) * SSM_STATE)
        cb = _dot(cm16[:, gs], bt16[gs, :])
        for e in range(SSM_HEADS_PER_GROUP):
            hd = grp * SSM_HEADS_PER_GROUP + e
            c = direction * SSM_HEADS + hd
            col = cs_col[:, c:c + 1]
            row = cs_row[c:c + 1, :]
            lm = jnp.exp(jnp.where(causal, col - row, NEG_BIG))
            mm = (cb * lm).astype(BF16)
            xdt = xs[:, hd * SSM_HEAD_DIM:(hd + 1) * SSM_HEAD_DIM] * dt[:, c:c + 1]
            y_diag = _dot(mm, xdt.astype(BF16))
            h_in = h_ref[direction, hd]
            cw = (cm[:, gs] * jnp.exp(col)).astype(BF16)
            y_off = _dot(cw, h_in.astype(BF16))
            tot = cs_col[end:end + 1, c:c + 1]
            st = _dot(bt16[gs, :], (xdt * jnp.exp(tot - col)).astype(BF16))
            h_ref[direction, hd] = jnp.exp(tot) * h_in + st
            ys.append(y_diag + y_off)
    return jnp.concatenate(ys, axis=1)


def _ssd_kernel(uf_ref, ub_ref, dtf_ref, dtb_ref, uc_ref, dtc_ref, alog_ref, yf_ref, yb_ref, h_ref):
    c = pl.program_id(1)
    lane = lax.broadcasted_iota(jnp.int32, (1, LANES), 1)
    a_row = jnp.where(lane < 2 * SSM_HEADS, -jnp.exp(alog_ref[...]), 0.0)
    ln = SSM_CHUNK

    @pl.when(c == 0)
    def _():
        h_ref[...] = jnp.zeros_like(h_ref)
        n_ctx = uc_ref.shape[1] // ln
        for s in range(n_ctx):
            _ssd_chunk(uc_ref[0, s * ln:(s + 1) * ln], dtc_ref[0, s * ln:(s + 1) * ln], a_row, h_ref, 0)
        for s in reversed(range(n_ctx)):
            _ssd_chunk(uc_ref[0, s * ln:(s + 1) * ln], dtc_ref[0, s * ln:(s + 1) * ln], a_row, h_ref, 1)

    n_sub = uf_ref.shape[1] // ln
    for s in range(n_sub):
        rows = slice(s * ln, (s + 1) * ln)
        yf_ref[0, rows] = _ssd_chunk(uf_ref[0, rows], dtf_ref[0, rows], a_row, h_ref, 0)
    for s in reversed(range(n_sub)):
        rows = slice(s * ln, (s + 1) * ln)
        yb_ref[0, rows] = _ssd_chunk(ub_ref[0, rows], dtb_ref[0, rows], a_row, h_ref, 1)


def _ssd(u, dtp, u_ctx, dt_ctx, alog_row, rows):
    b, t, cw = u.shape
    nctx = u_ctx.shape[1]
    ns = t // rows
    fwd = lambda bi, c: (bi, c, 0)
    bwd = lambda bi, c: (bi, ns - 1 - c, 0)
    return pl.pallas_call(
        _ssd_kernel,
        grid=(b, ns),
        in_specs=[
            pl.BlockSpec((1, rows, cw), fwd),
            pl.BlockSpec((1, rows, cw), bwd),
            pl.BlockSpec((1, rows, LANES), fwd),
            pl.BlockSpec((1, rows, LANES), bwd),
            pl.BlockSpec((1, nctx, cw), lambda bi, c: (bi, 0, 0)),
            pl.BlockSpec((1, nctx, LANES), lambda bi, c: (bi, 0, 0)),
            _const_spec((1, LANES)),
        ],
        out_specs=[
            pl.BlockSpec((1, rows, SSM_D_INNER), fwd),
            pl.BlockSpec((1, rows, SSM_D_INNER), bwd),
        ],
        out_shape=[
            jax.ShapeDtypeStruct((b, t, SSM_D_INNER), F32),
            jax.ShapeDtypeStruct((b, t, SSM_D_INNER), F32),
        ],
        scratch_shapes=[pltpu.VMEM((2, SSM_HEADS, SSM_STATE, SSM_HEAD_DIM), F32)],
        compiler_params=pltpu.CompilerParams(dimension_semantics=("parallel", "arbitrary"),
                                             vmem_limit_bytes=VMEM_LIMIT),
        name="ssd",
    )(u, u, dtp, dtp, u_ctx, dt_ctx, alog_row)


def _ffn_tail(x1, m, nw, wg_ref, wu_ref, wd_ref):
    h = _rms(x1) * nw
    h = (h * (1.0 + m[4:5]) + m[3:4]).astype(BF16)
    gate = _dot(h, wg_ref[...])
    up = _dot(h, wu_ref[...])
    hid = (_silu(gate) * up).astype(BF16)
    return x1 + m[5:6] * _dot(hid, wd_ref[...])


def _mix0_kernel(x_ref, a_ref, yf_ref, yb_ref, u_ref, z_ref, mod_ref, dsk_ref, snw_ref, nw_ref,
                 wa_ref, wy_ref, wg_ref, wu_ref, wd_ref, o_ref):
    m = mod_ref[0]
    y = yf_ref[0] + yb_ref[0] + u_ref[0] * dsk_ref[...]
    gy = y * _silu(z_ref[0])
    gw = SSM_D_INNER // SSM_GROUPS
    gn = jnp.concatenate([_rms(gy[:, i * gw:(i + 1) * gw]) for i in range(SSM_GROUPS)], axis=1)
    yn = (gn * snw_ref[...]).astype(BF16)
    mix = _dot(a_ref[0], wa_ref[...]) + _dot(yn, wy_ref[...])
    x1 = x_ref[0] + m[2:3] * mix
    o_ref[0] = _ffn_tail(x1, m, nw_ref[...], wg_ref, wu_ref, wd_ref)


def _mix0(x, a, yf, yb, u, z, mod, dsk, snw, nw, wa, wy, wg, wu, wd, tm):
    b, t, d = x.shape
    ffn = wg.shape[1]
    row = lambda bi, i: (bi, i, 0)
    return pl.pallas_call(
        _mix0_kernel,
        grid=(b, t // tm),
        in_specs=[
            pl.BlockSpec((1, tm, d), row),
            pl.BlockSpec((1, tm, Q_DIM), row),
            pl.BlockSpec((1, tm, SSM_D_INNER), row),
            pl.BlockSpec((1, tm, SSM_D_INNER), row),
            pl.BlockSpec((1, tm, SSM_D_INNER), row),
            pl.BlockSpec((1, tm, SSM_D_INNER), row),
            pl.BlockSpec((1, N_MOD, d), lambda bi, i: (bi, 0, 0)),
            _const_spec((1, SSM_D_INNER)),
            _const_spec((1, SSM_D_INNER)),
            _const_spec((1, d)),
            _const_spec((Q_DIM, d)),
            _const_spec((SSM_D_INNER, d)),
            _const_spec((d, ffn)),
            _const_spec((d, ffn)),
            _const_spec((ffn, d)),
        ],
        out_specs=pl.BlockSpec((1, tm, d), row),
        out_shape=jax.ShapeDtypeStruct((b, t, d), F32),
        compiler_params=_cparams(2),
        name="mix0",
    )(x, a, yf, yb, u, z, mod, dsk, snw, nw, wa, wy, wg, wu, wd)


def _layer1_kernel(x_ref, xp_ref, xn_ref, mod_ref, nmix_ref, win_ref, cw_ref, wout_ref, nffn_ref,
                   wg_ref, wu_ref, wd_ref, fn_ref, o_ref, v_ref):
    i = pl.program_id(1)
    last = pl.num_programs(1) - 1
    m = mod_ref[0]
    tm, d = x_ref.shape[1], x_ref.shape[2]
    x = x_ref[0]
    xa = jnp.concatenate([xp_ref[0], x, xn_ref[0]], axis=0)
    h = _rms(xa) * nmix_ref[...]
    h = (h * (1.0 + m[1:2]) + m[0:1]).astype(BF16)
    p = _dot(h, win_ref[...])
    v = p[:, d:2 * d] * p[:, 2 * d:]
    rid = lax.broadcasted_iota(jnp.int32, (tm + 2 * SUBLANES, 1), 0)
    keep = jnp.logical_and(jnp.logical_or(rid >= SUBLANES, i > 0),
                           jnp.logical_or(rid < SUBLANES + tm, i < last))
    v_ref[...] = jnp.where(keep, v, 0.0)
    cw = cw_ref[...]
    conv = (v_ref[SUBLANES - 1:SUBLANES - 1 + tm] * cw[0:1] + v_ref[SUBLANES:SUBLANES + tm] * cw[1:2]
            + v_ref[SUBLANES + 1:SUBLANES + 1 + tm] * cw[2:3])
    gated = (p[SUBLANES:SUBLANES + tm, :d] * conv).astype(BF16)
    x1 = x + m[2:3] * _dot(gated, wout_ref[...])
    x2 = _ffn_tail(x1, m, nffn_ref[...], wg_ref, wu_ref, wd_ref)
    o_ref[0] = _rms(x2) * fn_ref[...]


def _layer1(x, mod, nmix, win, cw, wout, nffn, wg, wu, wd, fn, tm):
    b, t, d = x.shape
    ffn = wg.shape[1]
    r8 = tm // SUBLANES
    nblk8 = t // SUBLANES
    return pl.pallas_call(
        _layer1_kernel,
        grid=(b, t // tm),
        in_specs=[
            pl.BlockSpec((1, tm, d), lambda bi, i: (bi, i, 0)),
            pl.BlockSpec((1, SUBLANES, d), lambda bi, i: (bi, jnp.maximum(i * r8 - 1, 0), 0)),
            pl.BlockSpec((1, SUBLANES, d), lambda bi, i: (bi, jnp.minimum((i + 1) * r8, nblk8 - 1), 0)),
            pl.BlockSpec((1, N_MOD, d), lambda bi, i: (bi, 0, 0)),
            _const_spec((1, d)),
            _const_spec((d, 3 * d)),
            _const_spec((3, d)),
            _const_spec((d, d)),
            _const_spec((1, d)),
            _const_spec((d, ffn)),
            _const_spec((d, ffn)),
            _const_spec((ffn, d)),
            _const_spec((1, d)),
        ],
        out_specs=pl.BlockSpec((1, tm, d), lambda bi, i: (bi, i, 0)),
        out_shape=jax.ShapeDtypeStruct((b, t, d), F32),
        scratch_shapes=[pltpu.VMEM((tm + 2 * SUBLANES, d), F32)],
        compiler_params=_cparams(2),
        name="layer1",
    )(x, x, x, mod, nmix, win, cw, wout, nffn, wg, wu, wd, fn)


def _rope_tables(n_tokens):
    rows = n_tokens // GRID_W
    row = jnp.repeat(jnp.arange(rows), GRID_W).astype(F32)
    col = jnp.tile(jnp.arange(GRID_W), rows).astype(F32)
    inv = 1.0 / (ROPE_THETA ** (jnp.arange(0, AXIS_DIM, 2, dtype=F32) / AXIS_DIM))
    ang = jnp.concatenate([row[:, None] * inv, col[:, None] * inv], axis=-1)
    return jnp.cos(ang), jnp.sin(ang)


_HEAD_PERM = np.concatenate([np.arange(0, HEAD_DIM, 2), np.arange(1, HEAD_DIM, 2)])


def _qk_tables(cos, sin, q_gain, k_gain):
    half = HEAD_DIM // 2
    sign = jnp.concatenate([-jnp.ones((half,), F32), jnp.ones((half,), F32)])
    c64 = jnp.concatenate([cos, cos], axis=1)
    s64 = jnp.concatenate([sin, sin], axis=1) * sign
    swap = np.concatenate([np.arange(half, HEAD_DIM), np.arange(0, half)])

    def tables(gain, scale):
        gp = gain[_HEAD_PERM] * scale
        return c64 * gp[None, :], s64 * gp[swap][None, :]

    aq, bq = tables(q_gain, HEAD_DIM ** -0.5)
    ak, bk = tables(k_gain, 1.0)
    return aq.T, bq.T, jnp.tile(ak, (1, ATTN_KV_HEADS)), jnp.tile(bk, (1, ATTN_KV_HEADS))


def kernel(x, c, ctx, c_ctx, ada_w, ada_b, norm_mix, norm_ffn, ffn_w_gate, ffn_w_up, ffn_w_down, hy_w_in,
           hy_q_norm, hy_k_norm, hy_conv_w, hy_conv_b, hy_dt_bias, hy_a_log, hy_d_skip, hy_ssm_norm, hy_w_out,
           sc_w_in, sc_conv_w, sc_w_out, final_norm):
    b, t, d = x.shape
    nctx = ctx.shape[1]
    tm_in = min(512, t)
    tm_mix = min(256, t)

    n_rows = -(-(b + 1) // SUBLANES) * SUBLANES
    cc = jnp.zeros((n_rows, d), F32).at[:b].set(c).at[b].set(c_ctx)
    mods = _modulation(cc, ada_w, ada_b).reshape(ada_w.shape[0], n_rows, N_MOD, d)
    mod0, mod0_ctx, mod1 = mods[0, :b], mods[0, b:b + 1], mods[1, :b]

    w_in = hy_w_in[0]
    idx_k, idx_v, idx_z = Q_DIM, Q_DIM + KV_DIM, Q_DIM + 2 * KV_DIM
    idx_xbc = idx_z + SSM_D_INNER
    idx_dt = idx_xbc + SSM_CONV_DIM
    q_cols = (np.arange(ATTN_HEADS)[:, None] * HEAD_DIM + _HEAD_PERM[None, :]).reshape(-1)
    k_cols = idx_k + (np.arange(ATTN_KV_HEADS)[:, None] * HEAD_DIM + _HEAD_PERM[None, :]).reshape(-1)
    dt_pad = jnp.zeros((d, LANES - 2 * SSM_HEADS), F32)
    wn = jnp.concatenate([w_in[:, k_cols], w_in[:, idx_z:idx_dt], w_in[:, idx_dt:], dt_pad], axis=1).astype(BF16)
    wt = jnp.concatenate([w_in[:, q_cols], w_in[:, idx_v:idx_z]], axis=1).T.astype(BF16)
    nw0 = norm_mix[0][None, :]

    cos, sin = _rope_tables(t)
    aq, bq, ck, sk = _qk_tables(cos, sin, hy_q_norm[0], hy_k_norm[0])
    ones_c, zeros_c = jnp.ones((nctx, AXIS_DIM), F32), jnp.zeros((nctx, AXIS_DIM), F32)
    aq_c, bq_c, ck_c, sk_c = _qk_tables(ones_c, zeros_c, hy_q_norm[0], hy_k_norm[0])
    seg = np.arange(KV_DIM) // HEAD_DIM
    bd = jnp.asarray(seg[:, None] == seg[None, :], BF16)

    qt, vt, k, z, xb = _inproj(x, mod0, True, nw0, wn, wt, aq, bq, ck, sk, bd, tm_in)
    _, vct, kc, _, xb_c = _inproj(ctx, mod0_ctx, False, nw0, wn, wt, aq_c, bq_c, ck_c, sk_c, bd, nctx)

    a_lat = _attention(qt, k, kc, vt, vct, tq=min(128, t), kt=min(256, nctx, tm_in))

    conv_w = jnp.concatenate([hy_conv_w[0], jnp.zeros((3, LANES), F32)], axis=1)
    conv_b = jnp.concatenate([hy_conv_b[0], jnp.zeros((LANES,), F32)])[None, :]
    dtb = jnp.concatenate([hy_dt_bias[0].reshape(-1), jnp.zeros((LANES - 2 * SSM_HEADS,), F32)])[None, :]
    alog = jnp.concatenate([hy_a_log[0].reshape(-1), jnp.zeros((LANES - 2 * SSM_HEADS,), F32)])[None, :]
    u, dtp = _ssdprep(xb, conv_w, conv_b, dtb, tm_in)
    u_c, dtp_c = _ssdprep(xb_c, conv_w, conv_b, dtb, nctx)
    yf, yb = _ssd(u, dtp, u_c, dtp_c, alog, rows=min(2 * SSM_CHUNK, t))

    dsk = jnp.repeat(hy_d_skip[0], SSM_HEAD_DIM)[None, :]
    w_out = hy_w_out[0].astype(BF16)
    x = _mix0(x, a_lat, yf, yb, u, z, mod0, dsk, hy_ssm_norm[0][None, :], norm_ffn[0][None, :],
              w_out[:Q_DIM], w_out[Q_DIM:], ffn_w_gate[0].astype(BF16), ffn_w_up[0].astype(BF16),
              ffn_w_down[0].astype(BF16), tm_mix)

    return _layer1(x, mod1, norm_mix[1][None, :], sc_w_in[0].astype(BF16), sc_conv_w[0], sc_w_out[0].astype(BF16),
                   norm_ffn[1][None, :], ffn_w_gate[1].astype(BF16), ffn_w_up[1].astype(BF16),
                   ffn_w_down[1].astype(BF16), final_norm[None, :], tm_mix)
```

```python
import functools

import jax
import jax.numpy as jnp
import numpy as np
from jax import lax
from jax.experimental import pallas as pl
from jax.experimental.pallas import tpu as pltpu

F32 = jnp.float32
BF16 = jnp.bfloat16

EPS = 1e-6
N_MOD = 6
GRID_W = 64
ROPE_THETA = 10000.0

ATTN_HEADS = 8
ATTN_KV_HEADS = 2
HEAD_DIM = 64
AXIS_DIM = HEAD_DIM // 2
KV_GROUP = ATTN_HEADS // ATTN_KV_HEADS
Q_DIM = ATTN_HEADS * HEAD_DIM
KV_DIM = ATTN_KV_HEADS * HEAD_DIM

SSM_HEADS = 8
SSM_HEAD_DIM = 64
SSM_D_INNER = SSM_HEADS * SSM_HEAD_DIM
SSM_GROUPS = 2
SSM_STATE = 64
SSM_CHUNK = 128
SSM_CONV_DIM = SSM_D_INNER + 2 * SSM_GROUPS * SSM_STATE
SSM_HEADS_PER_GROUP = SSM_HEADS // SSM_GROUPS

LANES = 128
SUBLANES = 8
XBCDT_W = SSM_CONV_DIM + LANES
NEG_BIG = -1e30
LOG2_E = 1.4426950408889634
ATTN_KEY_TILES = (768, 512, 256, 128)

VMEM_LIMIT = 56 * 1024 * 1024


def _cparams(n_axes):
    return pltpu.CompilerParams(dimension_semantics=("parallel",) * n_axes, vmem_limit_bytes=VMEM_LIMIT)


def _const_spec(shape):
    nd = len(shape)
    return pl.BlockSpec(shape, lambda *_: (0,) * nd, pipeline_mode=pl.Buffered(1))


def _rms(x):
    return x * lax.rsqrt(jnp.mean(x * x, axis=-1, keepdims=True) + EPS)


def _silu(x):
    return x * (1.0 / (1.0 + jnp.exp(-x)))


def _dot(a, b):
    return jnp.dot(a, b, preferred_element_type=F32)


def _dot_exact(a, b):
    return jnp.dot(a, b, preferred_element_type=F32, precision=lax.Precision.HIGHEST)


def _mod_kernel(c_ref, w_ref, b_ref, o_ref):
    c = c_ref[...]
    o_ref[0] = _dot_exact(_silu(c), w_ref[0]) + b_ref[0]


def _modulation(cc, ada_w, ada_b):
    depth, d, nd = ada_w.shape
    r = cc.shape[0]
    tn = d
    return pl.pallas_call(
        _mod_kernel,
        grid=(depth, nd // tn),
        in_specs=[
            pl.BlockSpec((r, d), lambda l, j: (0, 0)),
            pl.BlockSpec((1, d, tn), lambda l, j: (l, 0, j)),
            pl.BlockSpec((1, 1, tn), lambda l, j: (l, 0, j)),
        ],
        out_specs=pl.BlockSpec((1, r, tn), lambda l, j: (l, 0, j)),
        out_shape=jax.ShapeDtypeStruct((depth, r, nd), F32),
        compiler_params=_cparams(2),
        name="mod",
    )(cc, ada_w, ada_b.reshape(depth, 1, nd))


def _inproj_kernel(x_ref, mod_ref, nw_ref, wn_ref, wt_ref, aq_ref, bq_ref, ck_ref, sk_ref, bd_ref,
                   qt_ref, vt_ref, k_ref, z_ref, xb_ref):
    x = x_ref[0]
    m = mod_ref[0]
    h = _rms(x) * nw_ref[...]
    h = h * (1.0 + m[1:2]) + m[0:1]
    hb = h.astype(BF16)
    pn = _dot(hb, wn_ref[...])
    pt = lax.dot_general(wt_ref[...], hb, (((1,), (1,)), ((), ())), preferred_element_type=F32)
    tm = x.shape[0]

    kx = pn[:, :KV_DIM]
    sq = kx * kx
    hi = sq.astype(BF16)
    lo = (sq - hi.astype(F32)).astype(BF16)
    ss = _dot(hi, bd_ref[...]) + _dot(lo, bd_ref[...])
    rs = lax.rsqrt(ss * (1.0 / HEAD_DIM) + EPS)
    lane = lax.broadcasted_iota(jnp.int32, kx.shape, 1)
    half = HEAD_DIM // 2
    swap = jnp.where((lane & half) == 0, pltpu.roll(kx, KV_DIM - half, 1), pltpu.roll(kx, half, 1))
    k_ref[0] = (rs * (kx * ck_ref[...] + swap * sk_ref[...])).astype(BF16)

    q3 = pt[:Q_DIM].reshape(ATTN_HEADS, HEAD_DIM, tm)
    qrs = lax.rsqrt(jnp.mean(q3 * q3, axis=1, keepdims=True) + EPS)
    qsw = jnp.concatenate([q3[:, half:], q3[:, :half]], axis=1)
    qo = qrs * (q3 * aq_ref[...][None] + qsw * bq_ref[...][None])
    qt_ref[0] = qo.reshape(Q_DIM, tm).astype(BF16)

    vt_ref[0] = pt[Q_DIM:].astype(BF16)
    z_ref[0] = pn[:, KV_DIM:KV_DIM + SSM_D_INNER]
    xb_ref[0] = pn[:, KV_DIM + SSM_D_INNER:]


def _inproj(x, mod, per_batch_mod, nw, wn, wt, aq, bq, ck, sk, bd, tm):
    b, t, d = x.shape
    nt = t // tm
    mod_idx = (lambda bi, i: (bi, 0, 0)) if per_batch_mod else (lambda bi, i: (0, 0, 0))
    wn_cols = wn.shape[1]
    return pl.pallas_call(
        _inproj_kernel,
        grid=(b, nt),
        in_specs=[
            pl.BlockSpec((1, tm, d), lambda bi, i: (bi, i, 0)),
            pl.BlockSpec((1, N_MOD, d), mod_idx),
            _const_spec((1, d)),
            _const_spec((d, wn_cols)),
            _const_spec((Q_DIM + KV_DIM, d)),
            pl.BlockSpec((HEAD_DIM, tm), lambda bi, i: (0, i)),
            pl.BlockSpec((HEAD_DIM, tm), lambda bi, i: (0, i)),
            pl.BlockSpec((tm, KV_DIM), lambda bi, i: (i, 0)),
            pl.BlockSpec((tm, KV_DIM), lambda bi, i: (i, 0)),
            _const_spec((KV_DIM, KV_DIM)),
        ],
        out_specs=[
            pl.BlockSpec((1, Q_DIM, tm), lambda bi, i: (bi, 0, i)),
            pl.BlockSpec((1, KV_DIM, tm), lambda bi, i: (bi, 0, i)),
            pl.BlockSpec((1, tm, KV_DIM), lambda bi, i: (bi, i, 0)),
            pl.BlockSpec((1, tm, SSM_D_INNER), lambda bi, i: (bi, i, 0)),
            pl.BlockSpec((1, tm, XBCDT_W), lambda bi, i: (bi, i, 0)),
        ],
        out_shape=[
            jax.ShapeDtypeStruct((b, Q_DIM, t), BF16),
            jax.ShapeDtypeStruct((b, KV_DIM, t), BF16),
            jax.ShapeDtypeStruct((b, t, KV_DIM), BF16),
            jax.ShapeDtypeStruct((b, t, SSM_D_INNER), F32),
            jax.ShapeDtypeStruct((b, t, XBCDT_W), F32),
        ],
        compiler_params=_cparams(2),
        name="inproj",
    )(x, mod, nw, wn, wt, aq, bq, ck, sk, bd)


def _attn_kernel(qt_ref, k_ref, vt_ref, o_ref, s0_ref, s1_ref, p0_ref, p1_ref, m0_ref, m1_ref, *, nq):
    g = pl.program_id(1)
    i = pl.program_id(2)
    tq = qt_ref.shape[2]
    cols = KV_GROUP * tq
    nk, kt = s0_ref.shape[0], s0_ref.shape[1]
    s_refs, m_refs, p_refs = (s0_ref, s1_ref), (m0_ref, m1_ref), (p0_ref, p1_ref)

    def run(a_buf, b_buf):
        do_a, do_bc = a_buf is not None, b_buf is not None
        if do_a:
            q4 = qt_ref[0]
            qcat = jnp.concatenate([q4[h * HEAD_DIM:(h + 1) * HEAD_DIM] for h in range(KV_GROUP)], axis=1)
            qp = jnp.concatenate([qcat] * ATTN_KV_HEADS, axis=0)
            rowgrp = lax.broadcasted_iota(jnp.int32, qp.shape, 0) // HEAD_DIM
            qp = jnp.where(rowgrp == g, qp, jnp.zeros_like(qp))
        if do_bc:
            m8 = jnp.broadcast_to(jnp.max(m_refs[b_buf][...], axis=0, keepdims=True), (SUBLANES, cols))

        def stage_a(j, mx):
            ktile = k_ref[0, pl.ds(pl.multiple_of(j * kt, kt), kt), :]
            s = _dot(ktile, qp)
            s_refs[a_buf][j] = s
            return jnp.maximum(mx, jnp.max(s.reshape(kt // SUBLANES, SUBLANES, cols), axis=0))

        def stage_b(j, par, ls):
            s = s_refs[b_buf][j].reshape(kt // SUBLANES, SUBLANES, cols)
            p = jnp.exp2(s - m8[None])
            p_refs[par][...] = p.reshape(kt, cols).astype(BF16)
            return ls + jnp.sum(p, axis=0)

        def stage_c(j, par, acc):
            return acc + _dot(vt_ref[0, 0, j], p_refs[par][...])

        def tile(j, par, carry):
            mx, ls, acc = carry
            if do_a:
                mx = stage_a(j, mx)
            if do_bc:
                acc = stage_c(j - 1, 1 - par, acc)
                ls = stage_b(j, par, ls)
            return mx, ls, acc

        mx = jnp.full((SUBLANES, cols), NEG_BIG, F32)
        ls = jnp.zeros((SUBLANES, cols), F32)
        acc = jnp.zeros((HEAD_DIM, cols), F32)
        if do_a:
            mx = stage_a(0, mx)
        if do_bc:
            ls = stage_b(0, 0, ls)

        def body(t, carry):
            j = 1 + 2 * t
            return tile(j + 1, 0, tile(j, 1, carry))

        carry = lax.fori_loop(0, (nk - 1) // 2, body, (mx, ls, acc), unroll=True)
        if (nk - 1) % 2:
            carry = tile(nk - 1, (nk - 1) % 2, carry)
        mx, ls, acc = carry
        if do_a:
            m_refs[a_buf][...] = mx
        if do_bc:
            acc = stage_c(nk - 1, (nk - 1) % 2, acc)
            o = acc * (1.0 / jnp.sum(ls, axis=0, keepdims=True))
            o4 = jnp.concatenate([o[:, h * tq:(h + 1) * tq] for h in range(KV_GROUP)], axis=0)
            o_ref[0] = o4.T.astype(BF16)

    @pl.when(i == 0)
    def _():
        run(0, None)

    inner = jnp.logical_and(i > 0, i < nq)

    @pl.when(jnp.logical_and(inner, i % 2 == 1))
    def _():
        run(1, 0)

    @pl.when(jnp.logical_and(inner, i % 2 == 0))
    def _():
        run(0, 1)

    @pl.when(i == nq)
    def _():
        run(None, (nq - 1) % 2)


def _attention(qt, k_all, vt_tiles, tq):
    b, _, t = qt.shape
    tk = k_all.shape[1]
    nk, kt = vt_tiles.shape[2], vt_tiles.shape[4]
    nq = t // tq
    gw = KV_GROUP * HEAD_DIM
    cols = KV_GROUP * tq
    return pl.pallas_call(
        functools.partial(_attn_kernel, nq=nq),
        grid=(b, ATTN_KV_HEADS, nq + 1),
        in_specs=[
            pl.BlockSpec((1, gw, tq), lambda bi, g, i: (bi, g, jnp.minimum(i, nq - 1))),
            pl.BlockSpec((1, tk, KV_DIM), lambda bi, g, i: (bi, 0, 0)),
            pl.BlockSpec((1, 1, nk, HEAD_DIM, kt), lambda bi, g, i: (bi, g, 0, 0, 0)),
        ],
        out_specs=pl.BlockSpec((1, tq, gw), lambda bi, g, i: (bi, jnp.maximum(i - 1, 0), g)),
        out_shape=jax.ShapeDtypeStruct((b, t, Q_DIM), BF16),
        scratch_shapes=(
            [pltpu.VMEM((nk, kt, cols), F32)] * 2
            + [pltpu.VMEM((kt, cols), BF16)] * 2
            + [pltpu.VMEM((SUBLANES, cols), F32)] * 2
        ),
        compiler_params=pltpu.CompilerParams(dimension_semantics=("parallel", "parallel", "arbitrary"),
                                             vmem_limit_bytes=VMEM_LIMIT),
        name="attn",
    )(qt, k_all, vt_tiles)


def _ssdprep_kernel(x_ref, xp_ref, xn_ref, w_ref, b_ref, dtb_ref, xst_ref, bn_ref, ct_ref, dtt_ref, sc_ref):
    i = pl.program_id(1)
    last = pl.num_programs(1) - 1
    tp = x_ref.shape[1]
    x = x_ref[0]
    sc_ref[0:SUBLANES] = jnp.where(i > 0, xp_ref[0], 0.0)
    sc_ref[SUBLANES:SUBLANES + tp] = x
    sc_ref[SUBLANES + tp:] = jnp.where(i < last, xn_ref[0], 0.0)
    prev = sc_ref[SUBLANES - 1:SUBLANES - 1 + tp]
    nxt = sc_ref[SUBLANES + 1:SUBLANES + 1 + tp]
    w = w_ref[...]
    y = prev * w[0:1] + x * w[1:2] + nxt * w[2:3] + b_ref[...]
    u = _silu(y[:, :SSM_CONV_DIM])
    bc = SSM_D_INNER + SSM_GROUPS * SSM_STATE
    xst_ref[0] = u[:, :SSM_D_INNER].T
    bn_ref[0] = u[:, SSM_D_INNER:bc].astype(BF16)
    ct_ref[0] = u[:, bc:].T.astype(BF16)
    d = x[:, SSM_CONV_DIM:] + dtb_ref[...]
    dt = jnp.maximum(d, 0.0) + jnp.log1p(jnp.exp(-jnp.abs(d)))
    dtt_ref[0] = dt.T[:2 * SSM_HEADS]


def _ssdprep(xb, w, bias, dtb, tp):
    b, t, wd = xb.shape
    r8 = tp // SUBLANES
    nblk8 = t // SUBLANES
    gn = SSM_GROUPS * SSM_STATE
    col = lambda bi, i: (bi, 0, i)
    return pl.pallas_call(
        _ssdprep_kernel,
        grid=(b, t // tp),
        in_specs=[
            pl.BlockSpec((1, tp, wd), lambda bi, i: (bi, i, 0)),
            pl.BlockSpec((1, SUBLANES, wd), lambda bi, i: (bi, jnp.maximum(i * r8 - 1, 0), 0)),
            pl.BlockSpec((1, SUBLANES, wd), lambda bi, i: (bi, jnp.minimum((i + 1) * r8, nblk8 - 1), 0)),
            _const_spec((3, wd)),
            _const_spec((1, wd)),
            _const_spec((1, LANES)),
        ],
        out_specs=[
            pl.BlockSpec((1, SSM_D_INNER, tp), col),
            pl.BlockSpec((1, tp, gn), lambda bi, i: (bi, i, 0)),
            pl.BlockSpec((1, gn, tp), col),
            pl.BlockSpec((1, 2 * SSM_HEADS, tp), col),
        ],
        out_shape=[
            jax.ShapeDtypeStruct((b, SSM_D_INNER, t), F32),
            jax.ShapeDtypeStruct((b, t, gn), BF16),
            jax.ShapeDtypeStruct((b, gn, t), BF16),
            jax.ShapeDtypeStruct((b, 2 * SSM_HEADS, t), F32),
        ],
        scratch_shapes=[pltpu.VMEM((tp + 2 * SUBLANES, wd), F32)],
        compiler_params=_cparams(2),
        name="ssdprep",
    )(xb, xb, xb, w, bias, dtb)


def _ssd_chunk(xst, bn, ct, dtt, a_col, h_ref, direction):
    ln = xst.shape[1]
    n_hd = 2 * SSM_HEADS
    gn = SSM_GROUPS * SSM_STATE
    da = dtt * a_col
    ri = lax.broadcasted_iota(jnp.int32, (ln, ln), 0)
    ci = lax.broadcasted_iota(jnp.int32, (ln, ln), 1)
    reach = (ci >= ri) if direction == 0 else (ci <= ri)
    cs_row = _dot_exact(da, reach.astype(F32))
    tot = _dot_exact(da, jnp.ones((ln, ln), F32))
    cs_col = jnp.concatenate([cs_row, jnp.zeros((ln - n_hd, ln), F32)], axis=0).T
    rowgrp = lax.broadcasted_iota(jnp.int32, (gn, ln), 0) // SSM_STATE
    gp = SSM_HEADS_PER_GROUP * SSM_HEAD_DIM
    ys = []
    for grp in range(SSM_GROUPS):
        ctz = jnp.where(rowgrp == grp, ct, jnp.zeros_like(ct))
        cbt = _dot(bn, ctz)
        hg = h_ref[direction, grp * gp:(grp + 1) * gp, :]
        y_off = _dot(hg.astype(BF16), ctz)
        y_parts, xw_parts, h_parts = [], [], []
        for e in range(SSM_HEADS_PER_GROUP):
            hd = grp * SSM_HEADS_PER_GROUP + e
            c = direction * SSM_HEADS + hd
            rows = slice(e * SSM_HEAD_DIM, (e + 1) * SSM_HEAD_DIM)
            row = cs_row[c:c + 1, :]
            lmt = jnp.exp(jnp.where(reach, row - cs_col[:, c:c + 1], NEG_BIG))
            mt = (cbt * lmt).astype(BF16)
            xdt = xst[hd * SSM_HEAD_DIM:(hd + 1) * SSM_HEAD_DIM, :] * dtt[c:c + 1, :]
            y_parts.append(_dot(xdt.astype(BF16), mt) + y_off[rows] * jnp.exp(row))
            xw_parts.append((xdt * jnp.exp(tot[c:c + 1, :] - row)).astype(BF16))
            h_parts.append(hg[rows] * jnp.exp(tot[c:c + 1, :gn]))
        st = _dot(jnp.concatenate(xw_parts, axis=0), bn)
        h_ref[direction, grp * gp:(grp + 1) * gp, :] = jnp.concatenate(h_parts, axis=0) + st
        ys.append(jnp.concatenate(y_parts, axis=0))
    return jnp.concatenate(ys, axis=0)


def _ssd_kernel(xf_ref, xb_ref, bf_ref, bb_ref, cf_ref, cb_ref, df_ref, db_ref, xc_ref, bc_ref, cc_ref, dc_ref,
                alog_ref, dsk_ref, yf_ref, yb_ref, h_ref):
    c = pl.program_id(1)
    a_col = -jnp.exp(alog_ref[...])
    ln = SSM_CHUNK

    @pl.when(c == 0)
    def _():
        h_ref[...] = jnp.zeros_like(h_ref)
        n_ctx = xc_ref.shape[2] // ln
        order = list(range(n_ctx))
        for direction, chunks in ((0, order), (1, order[::-1])):
            for s in chunks:
                tok = slice(s * ln, (s + 1) * ln)
                _ssd_chunk(xc_ref[0, :, tok], bc_ref[0, tok, :], cc_ref[0, :, tok], dc_ref[0, :, tok], a_col,
                           h_ref, direction)

    n_sub = xf_ref.shape[2] // ln
    for s in range(n_sub):
        tok = slice(s * ln, (s + 1) * ln)
        xst = xf_ref[0, :, tok]
        y = _ssd_chunk(xst, bf_ref[0, tok, :], cf_ref[0, :, tok], df_ref[0, :, tok], a_col, h_ref, 0)
        yf_ref[0, :, tok] = y + xst * dsk_ref[...]
    for s in reversed(range(n_sub)):
        tok = slice(s * ln, (s + 1) * ln)
        yb_ref[0, :, tok] = _ssd_chunk(xb_ref[0, :, tok], bb_ref[0, tok, :], cb_ref[0, :, tok], db_ref[0, :, tok],
                                       a_col, h_ref, 1)


def _ssd(lat, ctx, alog_col, dsk_col, toks):
    xst, bn, ct, dtt = lat
    b, di, t = xst.shape
    gn = bn.shape[2]
    nh2 = dtt.shape[1]
    nctx = ctx[0].shape[2]
    ns = t // toks
    fcol = lambda bi, c: (bi, 0, c)
    bcol = lambda bi, c: (bi, 0, ns - 1 - c)
    frow = lambda bi, c: (bi, c, 0)
    brow = lambda bi, c: (bi, ns - 1 - c, 0)
    whole = lambda bi, c: (bi, 0, 0)
    return pl.pallas_call(
        _ssd_kernel,
        grid=(b, ns),
        in_specs=[
            pl.BlockSpec((1, di, toks), fcol), pl.BlockSpec((1, di, toks), bcol),
            pl.BlockSpec((1, toks, gn), frow), pl.BlockSpec((1, toks, gn), brow),
            pl.BlockSpec((1, gn, toks), fcol), pl.BlockSpec((1, gn, toks), bcol),
            pl.BlockSpec((1, nh2, toks), fcol), pl.BlockSpec((1, nh2, toks), bcol),
            pl.BlockSpec((1, di, nctx), whole), pl.BlockSpec((1, nctx, gn), whole),
            pl.BlockSpec((1, gn, nctx), whole), pl.BlockSpec((1, nh2, nctx), whole),
            _const_spec((nh2, SSM_CHUNK)),
            _const_spec((di, SSM_CHUNK)),
        ],
        out_specs=[pl.BlockSpec((1, di, toks), fcol), pl.BlockSpec((1, di, toks), bcol)],
        out_shape=[jax.ShapeDtypeStruct((b, di, t), F32), jax.ShapeDtypeStruct((b, di, t), F32)],
        scratch_shapes=[pltpu.VMEM((2, di, gn), F32)],
        compiler_params=pltpu.CompilerParams(dimension_semantics=("parallel", "arbitrary"),
                                             vmem_limit_bytes=VMEM_LIMIT),
        name="ssd",
    )(xst, xst, bn, bn, ct, ct, dtt, dtt, *ctx, alog_col, dsk_col)


def _ffn_tail(x1, m, nw, wg_ref, wu_ref, wd_ref):
    h = _rms(x1) * nw
    h = (h * (1.0 + m[4:5]) + m[3:4]).astype(BF16)
    gate = _dot(h, wg_ref[...])
    up = _dot(h, wu_ref[...])
    hid = (_silu(gate) * up).astype(BF16)
    return x1 + m[5:6] * _dot(hid, wd_ref[...])


def _mix0_kernel(x_ref, a_ref, yf_ref, yb_ref, z_ref, mod_ref, snw_ref, nw_ref,
                 wa_ref, wy_ref, wg_ref, wu_ref, wd_ref, o_ref):
    m = mod_ref[0]
    y = (yf_ref[0] + yb_ref[0]).T
    gy = y * _silu(z_ref[0])
    gw = SSM_D_INNER // SSM_GROUPS
    gn = jnp.concatenate([_rms(gy[:, i * gw:(i + 1) * gw]) for i in range(SSM_GROUPS)], axis=1)
    yn = (gn * snw_ref[...]).astype(BF16)
    mix = _dot(a_ref[0], wa_ref[...]) + _dot(yn, wy_ref[...])
    x1 = x_ref[0] + m[2:3] * mix
    o_ref[0] = _ffn_tail(x1, m, nw_ref[...], wg_ref, wu_ref, wd_ref)


def _mix0(x, a, yf, yb, z, mod, snw, nw, wa, wy, wg, wu, wd, tm):
    b, t, d = x.shape
    ffn = wg.shape[1]
    row = lambda bi, i: (bi, i, 0)
    col = lambda bi, i: (bi, 0, i)
    return pl.pallas_call(
        _mix0_kernel,
        grid=(b, t // tm),
        in_specs=[
            pl.BlockSpec((1, tm, d), row),
            pl.BlockSpec((1, tm, Q_DIM), row),
            pl.BlockSpec((1, SSM_D_INNER, tm), col),
            pl.BlockSpec((1, SSM_D_INNER, tm), col),
            pl.BlockSpec((1, tm, SSM_D_INNER), row),
            pl.BlockSpec((1, N_MOD, d), lambda bi, i: (bi, 0, 0)),
            _const_spec((1, SSM_D_INNER)),
            _const_spec((1, d)),
            _const_spec((Q_DIM, d)),
            _const_spec((SSM_D_INNER, d)),
            _const_spec((d, ffn)),
            _const_spec((d, ffn)),
            _const_spec((ffn, d)),
        ],
        out_specs=pl.BlockSpec((1, tm, d), row),
        out_shape=jax.ShapeDtypeStruct((b, t, d), F32),
        compiler_params=_cparams(2),
        name="mix0",
    )(x, a, yf, yb, z, mod, snw, nw, wa, wy, wg, wu, wd)


def _layer1_kernel(x_ref, xp_ref, xn_ref, mod_ref, nmix_ref, win_ref, cw_ref, wout_ref, nffn_ref,
                   wg_ref, wu_ref, wd_ref, fn_ref, o_ref, v_ref):
    i = pl.program_id(1)
    last = pl.num_programs(1) - 1
    m = mod_ref[0]
    tm, d = x_ref.shape[1], x_ref.shape[2]
    x = x_ref[0]
    xa = jnp.concatenate([xp_ref[0], x, xn_ref[0]], axis=0)
    h = _rms(xa) * nmix_ref[...]
    h = (h * (1.0 + m[1:2]) + m[0:1]).astype(BF16)
    p = _dot(h, win_ref[...])
    v = p[:, d:2 * d] * p[:, 2 * d:]
    rid = lax.broadcasted_iota(jnp.int32, (tm + 2 * SUBLANES, 1), 0)
    keep = jnp.logical_and(jnp.logical_or(rid >= SUBLANES, i > 0),
                           jnp.logical_or(rid < SUBLANES + tm, i < last))
    v_ref[...] = jnp.where(keep, v, 0.0)
    cw = cw_ref[...]
    conv = (v_ref[SUBLANES - 1:SUBLANES - 1 + tm] * cw[0:1] + v_ref[SUBLANES:SUBLANES + tm] * cw[1:2]
            + v_ref[SUBLANES + 1:SUBLANES + 1 + tm] * cw[2:3])
    gated = (p[SUBLANES:SUBLANES + tm, :d] * conv).astype(BF16)
    x1 = x + m[2:3] * _dot(gated, wout_ref[...])
    x2 = _ffn_tail(x1, m, nffn_ref[...], wg_ref, wu_ref, wd_ref)
    o_ref[0] = _rms(x2) * fn_ref[...]


def _layer1(x, mod, nmix, win, cw, wout, nffn, wg, wu, wd, fn, tm):
    b, t, d = x.shape
    ffn = wg.shape[1]
    r8 = tm // SUBLANES
    nblk8 = t // SUBLANES
    return pl.pallas_call(
        _layer1_kernel,
        grid=(b, t // tm),
        in_specs=[
            pl.BlockSpec((1, tm, d), lambda bi, i: (bi, i, 0)),
            pl.BlockSpec((1, SUBLANES, d), lambda bi, i: (bi, jnp.maximum(i * r8 - 1, 0), 0)),
            pl.BlockSpec((1, SUBLANES, d), lambda bi, i: (bi, jnp.minimum((i + 1) * r8, nblk8 - 1), 0)),
            pl.BlockSpec((1, N_MOD, d), lambda bi, i: (bi, 0, 0)),
            _const_spec((1, d)),
            _const_spec((d, 3 * d)),
            _const_spec((3, d)),
            _const_spec((d, d)),
            _const_spec((1, d)),
            _const_spec((d, ffn)),
            _const_spec((d, ffn)),
            _const_spec((ffn, d)),
            _const_spec((1, d)),
        ],
        out_specs=pl.BlockSpec((1, tm, d), lambda bi, i: (bi, i, 0)),
        out_shape=jax.ShapeDtypeStruct((b, t, d), F32),
        scratch_shapes=[pltpu.VMEM((tm + 2 * SUBLANES, d), F32)],
        compiler_params=_cparams(2),
        name="layer1",
    )(x, x, x, mod, nmix, win, cw, wout, nffn, wg, wu, wd, fn)


def _rope_tables(n_tokens):
    rows = n_tokens // GRID_W
    row = jnp.repeat(jnp.arange(rows), GRID_W).astype(F32)
    col = jnp.tile(jnp.arange(GRID_W), rows).astype(F32)
    inv = 1.0 / (ROPE_THETA ** (jnp.arange(0, AXIS_DIM, 2, dtype=F32) / AXIS_DIM))
    ang = jnp.concatenate([row[:, None] * inv, col[:, None] * inv], axis=-1)
    return jnp.cos(ang), jnp.sin(ang)


_HEAD_PERM = np.concatenate([np.arange(0, HEAD_DIM, 2), np.arange(1, HEAD_DIM, 2)])


def _qk_tables(cos, sin, q_gain, k_gain):
    half = HEAD_DIM // 2
    sign = jnp.concatenate([-jnp.ones((half,), F32), jnp.ones((half,), F32)])
    c64 = jnp.concatenate([cos, cos], axis=1)
    s64 = jnp.concatenate([sin, sin], axis=1) * sign
    swap = np.concatenate([np.arange(half, HEAD_DIM), np.arange(0, half)])

    def tables(gain, scale):
        gp = gain[_HEAD_PERM] * scale
        return c64 * gp[None, :], s64 * gp[swap][None, :]

    aq, bq = tables(q_gain, HEAD_DIM ** -0.5 * LOG2_E)
    ak, bk = tables(k_gain, 1.0)
    return aq.T, bq.T, jnp.tile(ak, (1, ATTN_KV_HEADS)), jnp.tile(bk, (1, ATTN_KV_HEADS))


def kernel(x, c, ctx, c_ctx, ada_w, ada_b, norm_mix, norm_ffn, ffn_w_gate, ffn_w_up, ffn_w_down, hy_w_in,
           hy_q_norm, hy_k_norm, hy_conv_w, hy_conv_b, hy_dt_bias, hy_a_log, hy_d_skip, hy_ssm_norm, hy_w_out,
           sc_w_in, sc_conv_w, sc_w_out, final_norm):
    b, t, d = x.shape
    nctx = ctx.shape[1]
    tm_in = min(512, t)
    tm_mix = min(256, t)

    n_rows = -(-(b + 1) // SUBLANES) * SUBLANES
    cc = jnp.zeros((n_rows, d), F32).at[:b].set(c).at[b].set(c_ctx)
    mods = _modulation(cc, ada_w, ada_b).reshape(ada_w.shape[0], n_rows, N_MOD, d)
    mod0, mod0_ctx, mod1 = mods[0, :b], mods[0, b:b + 1], mods[1, :b]

    w_in = hy_w_in[0]
    idx_k, idx_v, idx_z = Q_DIM, Q_DIM + KV_DIM, Q_DIM + 2 * KV_DIM
    idx_xbc = idx_z + SSM_D_INNER
    idx_dt = idx_xbc + SSM_CONV_DIM
    q_cols = (np.arange(ATTN_HEADS)[:, None] * HEAD_DIM + _HEAD_PERM[None, :]).reshape(-1)
    k_cols = idx_k + (np.arange(ATTN_KV_HEADS)[:, None] * HEAD_DIM + _HEAD_PERM[None, :]).reshape(-1)
    dt_pad = jnp.zeros((d, LANES - 2 * SSM_HEADS), F32)
    wn = jnp.concatenate([w_in[:, k_cols], w_in[:, idx_z:idx_dt], w_in[:, idx_dt:], dt_pad], axis=1).astype(BF16)
    wt = jnp.concatenate([w_in[:, q_cols], w_in[:, idx_v:idx_z]], axis=1).T.astype(BF16)
    nw0 = norm_mix[0][None, :]

    cos, sin = _rope_tables(t)
    aq, bq, ck, sk = _qk_tables(cos, sin, hy_q_norm[0], hy_k_norm[0])
    ones_c, zeros_c = jnp.ones((nctx, AXIS_DIM), F32), jnp.zeros((nctx, AXIS_DIM), F32)
    aq_c, bq_c, ck_c, sk_c = _qk_tables(ones_c, zeros_c, hy_q_norm[0], hy_k_norm[0])
    seg = np.arange(KV_DIM) // HEAD_DIM
    bd = jnp.asarray(seg[:, None] == seg[None, :], BF16)

    qt, vt, k, z, xb = _inproj(x, mod0, True, nw0, wn, wt, aq, bq, ck, sk, bd, tm_in)
    _, vct, kc, _, xb_c = _inproj(ctx, mod0_ctx, False, nw0, wn, wt, aq_c, bq_c, ck_c, sk_c, bd, nctx)

    tk = t + nctx
    kt = next(c for c in ATTN_KEY_TILES if tk % c == 0)
    k_all = jnp.concatenate([k, kc], axis=1)
    vt_all = jnp.concatenate([vt, vct], axis=2)
    vt_tiles = vt_all.reshape(b, ATTN_KV_HEADS, HEAD_DIM, tk // kt, kt).transpose(0, 1, 3, 2, 4)
    a_lat = _attention(qt, k_all, vt_tiles, tq=min(128, t))

    conv_w = jnp.concatenate([hy_conv_w[0], jnp.zeros((3, LANES), F32)], axis=1)
    conv_b = jnp.concatenate([hy_conv_b[0], jnp.zeros((LANES,), F32)])[None, :]
    dtb = jnp.concatenate([hy_dt_bias[0].reshape(-1), jnp.zeros((LANES - 2 * SSM_HEADS,), F32)])[None, :]
    alog_col = jnp.broadcast_to(hy_a_log[0].reshape(-1, 1), (2 * SSM_HEADS, SSM_CHUNK))
    dsk_col = jnp.broadcast_to(jnp.repeat(hy_d_skip[0], SSM_HEAD_DIM)[:, None], (SSM_D_INNER, SSM_CHUNK))
    ssd_lat = _ssdprep(xb, conv_w, conv_b, dtb, tm_in)
    ssd_ctx = _ssdprep(xb_c, conv_w, conv_b, dtb, nctx)
    yf, yb = _ssd(ssd_lat, ssd_ctx, alog_col, dsk_col, toks=min(2 * SSM_CHUNK, t))

    w_out = hy_w_out[0].astype(BF16)
    x = _mix0(x, a_lat, yf, yb, z, mod0, hy_ssm_norm[0][None, :], norm_ffn[0][None, :],
              w_out[:Q_DIM], w_out[Q_DIM:], ffn_w_gate[0].astype(BF16), ffn_w_up[0].astype(BF16),
              ffn_w_down[0].astype(BF16), tm_mix)

    return _layer1(x, mod1, norm_mix[1][None, :], sc_w_in[0].astype(BF16), sc_conv_w[0], sc_w_out[0].astype(BF16),
                   norm_ffn[1][None, :], ffn_w_gate[1].astype(BF16), ffn_w_up[1].astype(BF16),
                   ffn_w_down[1].astype(BF16), final_norm[None, :], tm_mix)
```

```python
import functools

import jax
import jax.numpy as jnp
import numpy as np
from jax import lax
from jax.experimental import pallas as pl
from jax.experimental.pallas import tpu as pltpu

F32 = jnp.float32
BF16 = jnp.bfloat16

EPS = 1e-6
N_MOD = 6
GRID_W = 64
ROPE_THETA = 10000.0

ATTN_HEADS = 8
ATTN_KV_HEADS = 2
HEAD_DIM = 64
AXIS_DIM = HEAD_DIM // 2
KV_GROUP = ATTN_HEADS // ATTN_KV_HEADS
Q_DIM = ATTN_HEADS * HEAD_DIM
KV_DIM = ATTN_KV_HEADS * HEAD_DIM

SSM_HEADS = 8
SSM_HEAD_DIM = 64
SSM_D_INNER = SSM_HEADS * SSM_HEAD_DIM
SSM_GROUPS = 2
SSM_STATE = 64
SSM_CHUNK = 128
SSM_CONV_DIM = SSM_D_INNER + 2 * SSM_GROUPS * SSM_STATE
SSM_HEADS_PER_GROUP = SSM_HEADS // SSM_GROUPS

LANES = 128
SUBLANES = 8
XBCDT_W = SSM_CONV_DIM + LANES
NEG_BIG = -1e30
LOG2_E = 1.4426950408889634
ATTN_KEY_TILES = (768, 512, 256, 128)
ATTN_SUB_TILE = 256
ATTN_PV_LAG = 3
SAFE_SCORE_BOUND = 40.0
SCORE_BOUND_SLACK = 1.001

VMEM_LIMIT = 56 * 1024 * 1024


def _cparams(n_axes):
    return pltpu.CompilerParams(dimension_semantics=("parallel",) * n_axes, vmem_limit_bytes=VMEM_LIMIT)


def _const_spec(shape):
    nd = len(shape)
    return pl.BlockSpec(shape, lambda *_: (0,) * nd, pipeline_mode=pl.Buffered(1))


def _rms(x):
    return x * lax.rsqrt(jnp.mean(x * x, axis=-1, keepdims=True) + EPS)


def _silu(x):
    return x * (1.0 / (1.0 + jnp.exp(-x)))


def _dot(a, b):
    return jnp.dot(a, b, preferred_element_type=F32)


def _dot_exact(a, b):
    return jnp.dot(a, b, preferred_element_type=F32, precision=lax.Precision.HIGHEST)


def _mod_kernel(c_ref, w_ref, b_ref, o_ref):
    c = c_ref[...]
    o_ref[0] = _dot_exact(_silu(c), w_ref[0]) + b_ref[0]


def _modulation(cc, ada_w, ada_b):
    depth, d, nd = ada_w.shape
    r = cc.shape[0]
    tn = d
    return pl.pallas_call(
        _mod_kernel,
        grid=(depth, nd // tn),
        in_specs=[
            pl.BlockSpec((r, d), lambda l, j: (0, 0)),
            pl.BlockSpec((1, d, tn), lambda l, j: (l, 0, j)),
            pl.BlockSpec((1, 1, tn), lambda l, j: (l, 0, j)),
        ],
        out_specs=pl.BlockSpec((1, r, tn), lambda l, j: (l, 0, j)),
        out_shape=jax.ShapeDtypeStruct((depth, r, nd), F32),
        compiler_params=_cparams(2),
        name="mod",
    )(cc, ada_w, ada_b.reshape(depth, 1, nd))


def _inproj_kernel(x_ref, mod_ref, nw_ref, wn_ref, wt_ref, aq_ref, bq_ref, ck_ref, sk_ref, bd_ref,
                   qt_ref, vt_ref, k_ref, z_ref, xb_ref):
    x = x_ref[0]
    m = mod_ref[0]
    h = _rms(x) * nw_ref[...]
    h = h * (1.0 + m[1:2]) + m[0:1]
    hb = h.astype(BF16)
    pn = _dot(hb, wn_ref[...])
    pt = lax.dot_general(wt_ref[...], hb, (((1,), (1,)), ((), ())), preferred_element_type=F32)
    tm = x.shape[0]

    kx = pn[:, :KV_DIM]
    sq = kx * kx
    hi = sq.astype(BF16)
    lo = (sq - hi.astype(F32)).astype(BF16)
    ss = _dot(hi, bd_ref[...]) + _dot(lo, bd_ref[...])
    rs = lax.rsqrt(ss * (1.0 / HEAD_DIM) + EPS)
    lane = lax.broadcasted_iota(jnp.int32, kx.shape, 1)
    half = HEAD_DIM // 2
    swap = jnp.where((lane & half) == 0, pltpu.roll(kx, KV_DIM - half, 1), pltpu.roll(kx, half, 1))
    k_ref[0] = (rs * (kx * ck_ref[...] + swap * sk_ref[...])).astype(BF16)

    q3 = pt[:Q_DIM].reshape(ATTN_HEADS, HEAD_DIM, tm)
    qrs = lax.rsqrt(jnp.mean(q3 * q3, axis=1, keepdims=True) + EPS)
    qsw = jnp.concatenate([q3[:, half:], q3[:, :half]], axis=1)
    qo = qrs * (q3 * aq_ref[...][None] + qsw * bq_ref[...][None])
    qt_ref[0] = qo.reshape(Q_DIM, tm).astype(BF16)

    vt_ref[0] = pt[Q_DIM:].astype(BF16)
    z_ref[0] = pn[:, KV_DIM:KV_DIM + SSM_D_INNER]
    xb_ref[0] = pn[:, KV_DIM + SSM_D_INNER:]


def _inproj(x, mod, per_batch_mod, nw, wn, wt, aq, bq, ck, sk, bd, tm):
    b, t, d = x.shape
    nt = t // tm
    mod_idx = (lambda bi, i: (bi, 0, 0)) if per_batch_mod else (lambda bi, i: (0, 0, 0))
    wn_cols = wn.shape[1]
    return pl.pallas_call(
        _inproj_kernel,
        grid=(b, nt),
        in_specs=[
            pl.BlockSpec((1, tm, d), lambda bi, i: (bi, i, 0)),
            pl.BlockSpec((1, N_MOD, d), mod_idx),
            _const_spec((1, d)),
            _const_spec((d, wn_cols)),
            _const_spec((Q_DIM + KV_DIM, d)),
            pl.BlockSpec((HEAD_DIM, tm), lambda bi, i: (0, i)),
            pl.BlockSpec((HEAD_DIM, tm), lambda bi, i: (0, i)),
            pl.BlockSpec((tm, KV_DIM), lambda bi, i: (i, 0)),
            pl.BlockSpec((tm, KV_DIM), lambda bi, i: (i, 0)),
            _const_spec((KV_DIM, KV_DIM)),
        ],
        out_specs=[
            pl.BlockSpec((1, Q_DIM, tm), lambda bi, i: (bi, 0, i)),
            pl.BlockSpec((1, KV_DIM, tm), lambda bi, i: (bi, 0, i)),
            pl.BlockSpec((1, tm, KV_DIM), lambda bi, i: (bi, i, 0)),
            pl.BlockSpec((1, tm, SSM_D_INNER), lambda bi, i: (bi, i, 0)),
            pl.BlockSpec((1, tm, XBCDT_W), lambda bi, i: (bi, i, 0)),
        ],
        out_shape=[
            jax.ShapeDtypeStruct((b, Q_DIM, t), BF16),
            jax.ShapeDtypeStruct((b, KV_DIM, t), BF16),
            jax.ShapeDtypeStruct((b, t, KV_DIM), BF16),
            jax.ShapeDtypeStruct((b, t, SSM_D_INNER), F32),
            jax.ShapeDtypeStruct((b, t, XBCDT_W), F32),
        ],
        compiler_params=_cparams(2),
        name="inproj",
    )(x, mod, nw, wn, wt, aq, bq, ck, sk, bd)


def _attn_kernel(qt_ref, k_ref, vt_ref, o_ref, s0_ref, s1_ref, p0_ref, p1_ref, m0_ref, m1_ref, *, nq):
    g = pl.program_id(1)
    i = pl.program_id(2)
    tq = qt_ref.shape[2]
    cols = KV_GROUP * tq
    nk, kt = s0_ref.shape[0], s0_ref.shape[1]
    s_refs, m_refs, p_refs = (s0_ref, s1_ref), (m0_ref, m1_ref), (p0_ref, p1_ref)

    def run(a_buf, b_buf):
        do_a, do_bc = a_buf is not None, b_buf is not None
        if do_a:
            q4 = qt_ref[0]
            qcat = jnp.concatenate([q4[h * HEAD_DIM:(h + 1) * HEAD_DIM] for h in range(KV_GROUP)], axis=1)
            qp = jnp.concatenate([qcat] * ATTN_KV_HEADS, axis=0)
            rowgrp = lax.broadcasted_iota(jnp.int32, qp.shape, 0) // HEAD_DIM
            qp = jnp.where(rowgrp == g, qp, jnp.zeros_like(qp))
        if do_bc:
            m8 = jnp.broadcast_to(jnp.max(m_refs[b_buf][...], axis=0, keepdims=True), (SUBLANES, cols))

        def stage_a(j, mx):
            ktile = k_ref[0, pl.ds(pl.multiple_of(j * kt, kt), kt), :]
            s = _dot(ktile, qp)
            s_refs[a_buf][j] = s
            return jnp.maximum(mx, jnp.max(s.reshape(kt // SUBLANES, SUBLANES, cols), axis=0))

        def stage_b(j, par, ls):
            s = s_refs[b_buf][j].reshape(kt // SUBLANES, SUBLANES, cols)
            p = jnp.exp2(s - m8[None])
            p_refs[par][...] = p.reshape(kt, cols).astype(BF16)
            return ls + jnp.sum(p, axis=0)

        def stage_c(j, par, acc):
            return acc + _dot(vt_ref[0, 0, j], p_refs[par][...])

        def tile(j, par, carry):
            mx, ls, acc = carry
            if do_a:
                mx = stage_a(j, mx)
            if do_bc:
                acc = stage_c(j - 1, 1 - par, acc)
                ls = stage_b(j, par, ls)
            return mx, ls, acc

        mx = jnp.full((SUBLANES, cols), NEG_BIG, F32)
        ls = jnp.zeros((SUBLANES, cols), F32)
        acc = jnp.zeros((HEAD_DIM, cols), F32)
        if do_a:
            mx = stage_a(0, mx)
        if do_bc:
            ls = stage_b(0, 0, ls)

        def body(t, carry):
            j = 1 + 2 * t
            return tile(j + 1, 0, tile(j, 1, carry))

        carry = lax.fori_loop(0, (nk - 1) // 2, body, (mx, ls, acc), unroll=True)
        if (nk - 1) % 2:
            carry = tile(nk - 1, (nk - 1) % 2, carry)
        mx, ls, acc = carry
        if do_a:
            m_refs[a_buf][...] = mx
        if do_bc:
            acc = stage_c(nk - 1, (nk - 1) % 2, acc)
            o = acc * (1.0 / jnp.sum(ls, axis=0, keepdims=True))
            o4 = jnp.concatenate([o[:, h * tq:(h + 1) * tq] for h in range(KV_GROUP)], axis=0)
            o_ref[0] = o4.T.astype(BF16)

    @pl.when(i == 0)
    def _():
        run(0, None)

    inner = jnp.logical_and(i > 0, i < nq)

    @pl.when(jnp.logical_and(inner, i % 2 == 1))
    def _():
        run(1, 0)

    @pl.when(jnp.logical_and(inner, i % 2 == 0))
    def _():
        run(0, 1)

    @pl.when(i == nq)
    def _():
        run(None, (nq - 1) % 2)


def _attention(qt, k_all, vt_tiles, tq):
    b, _, t = qt.shape
    tk = k_all.shape[1]
    nk, kt = vt_tiles.shape[2], vt_tiles.shape[4]
    nq = t // tq
    gw = KV_GROUP * HEAD_DIM
    cols = KV_GROUP * tq
    return pl.pallas_call(
        functools.partial(_attn_kernel, nq=nq),
        grid=(b, ATTN_KV_HEADS, nq + 1),
        in_specs=[
            pl.BlockSpec((1, gw, tq), lambda bi, g, i: (bi, g, jnp.minimum(i, nq - 1))),
            pl.BlockSpec((1, tk, KV_DIM), lambda bi, g, i: (bi, 0, 0)),
            pl.BlockSpec((1, 1, nk, HEAD_DIM, kt), lambda bi, g, i: (bi, g, 0, 0, 0)),
        ],
        out_specs=pl.BlockSpec((1, tq, gw), lambda bi, g, i: (bi, jnp.maximum(i - 1, 0), g)),
        out_shape=jax.ShapeDtypeStruct((b, t, Q_DIM), BF16),
        scratch_shapes=(
            [pltpu.VMEM((nk, kt, cols), F32)] * 2
            + [pltpu.VMEM((kt, cols), BF16)] * 2
            + [pltpu.VMEM((SUBLANES, cols), F32)] * 2
        ),
        compiler_params=pltpu.CompilerParams(dimension_semantics=("parallel", "parallel", "arbitrary"),
                                             vmem_limit_bytes=VMEM_LIMIT),
        name="attn",
    )(qt, k_all, vt_tiles)


def _attn_bounded_kernel(qt_ref, k_ref, vt_ref, o_ref, km_ref, *p_refs):
    g = pl.program_id(1)
    i = pl.program_id(2)
    tq = qt_ref.shape[2]
    cols = KV_GROUP * tq
    nk, kt = vt_ref.shape[2], vt_ref.shape[4]

    @pl.when(i == 0)
    def _():
        lanegrp = lax.broadcasted_iota(jnp.int32, (kt, KV_DIM), 1) // HEAD_DIM

        def body(j, mx):
            kf = k_ref[0, pl.ds(pl.multiple_of(j * kt, kt), kt), :].astype(F32)
            return jnp.maximum(mx, jnp.sum(jnp.where(lanegrp == g, kf * kf, 0.0), axis=1, keepdims=True))

        mx = lax.fori_loop(0, nk, body, jnp.zeros((kt, 1), F32))
        km_ref[...] = jnp.broadcast_to(jnp.sqrt(jnp.max(mx, axis=0, keepdims=True)), km_ref.shape)

    q4 = qt_ref[0]
    qf = q4.astype(F32)
    qn = jnp.sqrt(jnp.sum((qf * qf).reshape(KV_GROUP, HEAD_DIM, tq), axis=1))
    qn = jnp.concatenate([qn[h:h + 1] for h in range(KV_GROUP)], axis=1)
    kmax = jnp.concatenate([km_ref[0:1, :]] * (cols // LANES), axis=1)
    mb8 = jnp.broadcast_to(qn * kmax * SCORE_BOUND_SLACK, (SUBLANES, cols))
    qcat = jnp.concatenate([q4[h * HEAD_DIM:(h + 1) * HEAD_DIM] for h in range(KV_GROUP)], axis=1)
    qp = jnp.concatenate([qcat] * ATTN_KV_HEADS, axis=0)
    rowgrp = lax.broadcasted_iota(jnp.int32, qp.shape, 0) // HEAD_DIM
    qp = jnp.where(rowgrp == g, qp, jnp.zeros_like(qp))

    ls = jnp.zeros((SUBLANES, cols), F32)
    acc = jnp.zeros((HEAD_DIM, cols), F32)
    sub = p_refs[0].shape[0]
    n_sub = kt // sub
    n_all = nk * n_sub
    ring = len(p_refs)
    lag = ring - 1

    def pv(u, acc):
        vt = vt_ref[0, 0, u // n_sub][:, (u % n_sub) * sub:(u % n_sub + 1) * sub]
        return acc + _dot(vt, p_refs[u % ring][...])

    for u in range(n_all):
        s = _dot(k_ref[0, u * sub:(u + 1) * sub, :], qp)
        if u >= lag:
            acc = pv(u - lag, acc)
        p = jnp.exp2(s.reshape(sub // SUBLANES, SUBLANES, cols) - mb8[None])
        p_refs[u % ring][...] = p.reshape(sub, cols).astype(BF16)
        ls = ls + jnp.sum(p, axis=0)
    for u in range(max(n_all - lag, 0), n_all):
        acc = pv(u, acc)
    o = acc * (1.0 / jnp.sum(ls, axis=0, keepdims=True))
    o4 = jnp.concatenate([o[:, h * tq:(h + 1) * tq] for h in range(KV_GROUP)], axis=0)
    o_ref[0] = o4.T.astype(BF16)


def _attention_bounded(qt, k_all, vt_tiles, tq):
    b, _, t = qt.shape
    tk = k_all.shape[1]
    nk, kt = vt_tiles.shape[2], vt_tiles.shape[4]
    gw = KV_GROUP * HEAD_DIM
    cols = KV_GROUP * tq
    return pl.pallas_call(
        _attn_bounded_kernel,
        grid=(b, ATTN_KV_HEADS, t // tq),
        in_specs=[
            pl.BlockSpec((1, gw, tq), lambda bi, g, i: (bi, g, i)),
            pl.BlockSpec((1, tk, KV_DIM), lambda bi, g, i: (bi, 0, 0)),
            pl.BlockSpec((1, 1, nk, HEAD_DIM, kt), lambda bi, g, i: (bi, g, 0, 0, 0)),
        ],
        out_specs=pl.BlockSpec((1, tq, gw), lambda bi, g, i: (bi, i, g)),
        out_shape=jax.ShapeDtypeStruct((b, t, Q_DIM), BF16),
        scratch_shapes=([pltpu.VMEM((SUBLANES, LANES), F32)]
                        + [pltpu.VMEM((min(kt, ATTN_SUB_TILE), cols), BF16)] * (ATTN_PV_LAG + 1)),
        compiler_params=pltpu.CompilerParams(dimension_semantics=("parallel", "parallel", "arbitrary"),
                                             vmem_limit_bytes=VMEM_LIMIT),
        name="attn_bounded",
    )(qt, k_all, vt_tiles)


def _ssdprep_kernel(x_ref, xp_ref, xn_ref, w_ref, b_ref, dtb_ref, xst_ref, bn_ref, ct_ref, dtt_ref, sc_ref):
    i = pl.program_id(1)
    last = pl.num_programs(1) - 1
    tp = x_ref.shape[1]
    x = x_ref[0]
    sc_ref[0:SUBLANES] = jnp.where(i > 0, xp_ref[0], 0.0)
    sc_ref[SUBLANES:SUBLANES + tp] = x
    sc_ref[SUBLANES + tp:] = jnp.where(i < last, xn_ref[0], 0.0)
    prev = sc_ref[SUBLANES - 1:SUBLANES - 1 + tp]
    nxt = sc_ref[SUBLANES + 1:SUBLANES + 1 + tp]
    w = w_ref[...]
    y = prev * w[0:1] + x * w[1:2] + nxt * w[2:3] + b_ref[...]
    u = _silu(y[:, :SSM_CONV_DIM])
    bc = SSM_D_INNER + SSM_GROUPS * SSM_STATE
    xst_ref[0] = u[:, :SSM_D_INNER].T
    bn_ref[0] = u[:, SSM_D_INNER:bc].astype(BF16)
    ct_ref[0] = u[:, bc:].T.astype(BF16)
    d = x[:, SSM_CONV_DIM:] + dtb_ref[...]
    dt = jnp.maximum(d, 0.0) + jnp.log1p(jnp.exp(-jnp.abs(d)))
    dtt_ref[0] = dt.T[:2 * SSM_HEADS]


def _ssdprep(xb, w, bias, dtb, tp):
    b, t, wd = xb.shape
    r8 = tp // SUBLANES
    nblk8 = t // SUBLANES
    gn = SSM_GROUPS * SSM_STATE
    col = lambda bi, i: (bi, 0, i)
    return pl.pallas_call(
        _ssdprep_kernel,
        grid=(b, t // tp),
        in_specs=[
            pl.BlockSpec((1, tp, wd), lambda bi, i: (bi, i, 0)),
            pl.BlockSpec((1, SUBLANES, wd), lambda bi, i: (bi, jnp.maximum(i * r8 - 1, 0), 0)),
            pl.BlockSpec((1, SUBLANES, wd), lambda bi, i: (bi, jnp.minimum((i + 1) * r8, nblk8 - 1), 0)),
            _const_spec((3, wd)),
            _const_spec((1, wd)),
            _const_spec((1, LANES)),
        ],
        out_specs=[
            pl.BlockSpec((1, SSM_D_INNER, tp), col),
            pl.BlockSpec((1, tp, gn), lambda bi, i: (bi, i, 0)),
            pl.BlockSpec((1, gn, tp), col),
            pl.BlockSpec((1, 2 * SSM_HEADS, tp), col),
        ],
        out_shape=[
            jax.ShapeDtypeStruct((b, SSM_D_INNER, t), F32),
            jax.ShapeDtypeStruct((b, t, gn), BF16),
            jax.ShapeDtypeStruct((b, gn, t), BF16),
            jax.ShapeDtypeStruct((b, 2 * SSM_HEADS, t), F32),
        ],
        scratch_shapes=[pltpu.VMEM((tp + 2 * SUBLANES, wd), F32)],
        compiler_params=_cparams(2),
        name="ssdprep",
    )(xb, xb, xb, w, bias, dtb)


def _ssd_chunk(xst, bn, ct, dtt, a_col, h_ref, direction):
    ln = xst.shape[1]
    n_hd = 2 * SSM_HEADS
    gn = SSM_GROUPS * SSM_STATE
    da = dtt * a_col
    ri = lax.broadcasted_iota(jnp.int32, (ln, ln), 0)
    ci = lax.broadcasted_iota(jnp.int32, (ln, ln), 1)
    reach = (ci >= ri) if direction == 0 else (ci <= ri)
    cs_row = _dot_exact(da, reach.astype(F32))
    tot = _dot_exact(da, jnp.ones((ln, ln), F32))
    cs_col = jnp.concatenate([cs_row, jnp.zeros((ln - n_hd, ln), F32)], axis=0).T
    rowgrp = lax.broadcasted_iota(jnp.int32, (gn, ln), 0) // SSM_STATE
    gp = SSM_HEADS_PER_GROUP * SSM_HEAD_DIM
    ys = []
    for grp in range(SSM_GROUPS):
        ctz = jnp.where(rowgrp == grp, ct, jnp.zeros_like(ct))
        cbt = _dot(bn, ctz)
        hg = h_ref[direction, grp * gp:(grp + 1) * gp, :]
        y_off = _dot(hg.astype(BF16), ctz)
        y_parts, xw_parts, h_parts = [], [], []
        for e in range(SSM_HEADS_PER_GROUP):
            hd = grp * SSM_HEADS_PER_GROUP + e
            c = direction * SSM_HEADS + hd
            rows = slice(e * SSM_HEAD_DIM, (e + 1) * SSM_HEAD_DIM)
            row = cs_row[c:c + 1, :]
            lmt = jnp.exp(jnp.where(reach, row - cs_col[:, c:c + 1], NEG_BIG))
            mt = (cbt * lmt).astype(BF16)
            xdt = xst[hd * SSM_HEAD_DIM:(hd + 1) * SSM_HEAD_DIM, :] * dtt[c:c + 1, :]
            y_parts.append(_dot(xdt.astype(BF16), mt) + y_off[rows] * jnp.exp(row))
            xw_parts.append((xdt * jnp.exp(tot[c:c + 1, :] - row)).astype(BF16))
            h_parts.append(hg[rows] * jnp.exp(tot[c:c + 1, :gn]))
        st = _dot(jnp.concatenate(xw_parts, axis=0), bn)
        h_ref[direction, grp * gp:(grp + 1) * gp, :] = jnp.concatenate(h_parts, axis=0) + st
        ys.append(jnp.concatenate(y_parts, axis=0))
    return jnp.concatenate(ys, axis=0)


def _ssd_kernel(xf_ref, xb_ref, bf_ref, bb_ref, cf_ref, cb_ref, df_ref, db_ref, xc_ref, bc_ref, cc_ref, dc_ref,
                alog_ref, dsk_ref, yf_ref, yb_ref, h_ref):
    c = pl.program_id(1)
    a_col = -jnp.exp(alog_ref[...])
    ln = SSM_CHUNK

    @pl.when(c == 0)
    def _():
        h_ref[...] = jnp.zeros_like(h_ref)
        n_ctx = xc_ref.shape[2] // ln
        order = list(range(n_ctx))
        for direction, chunks in ((0, order), (1, order[::-1])):
            for s in chunks:
                tok = slice(s * ln, (s + 1) * ln)
                _ssd_chunk(xc_ref[0, :, tok], bc_ref[0, tok, :], cc_ref[0, :, tok], dc_ref[0, :, tok], a_col,
                           h_ref, direction)

    n_sub = xf_ref.shape[2] // ln
    for s in range(n_sub):
        tok = slice(s * ln, (s + 1) * ln)
        xst = xf_ref[0, :, tok]
        y = _ssd_chunk(xst, bf_ref[0, tok, :], cf_ref[0, :, tok], df_ref[0, :, tok], a_col, h_ref, 0)
        yf_ref[0, :, tok] = y + xst * dsk_ref[...]
    for s in reversed(range(n_sub)):
        tok = slice(s * ln, (s + 1) * ln)
        yb_ref[0, :, tok] = _ssd_chunk(xb_ref[0, :, tok], bb_ref[0, tok, :], cb_ref[0, :, tok], db_ref[0, :, tok],
                                       a_col, h_ref, 1)


def _ssd(lat, ctx, alog_col, dsk_col, toks):
    xst, bn, ct, dtt = lat
    b, di, t = xst.shape
    gn = bn.shape[2]
    nh2 = dtt.shape[1]
    nctx = ctx[0].shape[2]
    ns = t // toks
    fcol = lambda bi, c: (bi, 0, c)
    bcol = lambda bi, c: (bi, 0, ns - 1 - c)
    frow = lambda bi, c: (bi, c, 0)
    brow = lambda bi, c: (bi, ns - 1 - c, 0)
    whole = lambda bi, c: (bi, 0, 0)
    return pl.pallas_call(
        _ssd_kernel,
        grid=(b, ns),
        in_specs=[
            pl.BlockSpec((1, di, toks), fcol), pl.BlockSpec((1, di, toks), bcol),
            pl.BlockSpec((1, toks, gn), frow), pl.BlockSpec((1, toks, gn), brow),
            pl.BlockSpec((1, gn, toks), fcol), pl.BlockSpec((1, gn, toks), bcol),
            pl.BlockSpec((1, nh2, toks), fcol), pl.BlockSpec((1, nh2, toks), bcol),
            pl.BlockSpec((1, di, nctx), whole), pl.BlockSpec((1, nctx, gn), whole),
            pl.BlockSpec((1, gn, nctx), whole), pl.BlockSpec((1, nh2, nctx), whole),
            _const_spec((nh2, SSM_CHUNK)),
            _const_spec((di, SSM_CHUNK)),
        ],
        out_specs=[pl.BlockSpec((1, di, toks), fcol), pl.BlockSpec((1, di, toks), bcol)],
        out_shape=[jax.ShapeDtypeStruct((b, di, t), F32), jax.ShapeDtypeStruct((b, di, t), F32)],
        scratch_shapes=[pltpu.VMEM((2, di, gn), F32)],
        compiler_params=pltpu.CompilerParams(dimension_semantics=("parallel", "arbitrary"),
                                             vmem_limit_bytes=VMEM_LIMIT),
        name="ssd",
    )(xst, xst, bn, bn, ct, ct, dtt, dtt, *ctx, alog_col, dsk_col)


def _ffn_tail(x1, m, nw, wg_ref, wu_ref, wd_ref):
    h = _rms(x1) * nw
    h = (h * (1.0 + m[4:5]) + m[3:4]).astype(BF16)
    gate = _dot(h, wg_ref[...])
    up = _dot(h, wu_ref[...])
    hid = (_silu(gate) * up).astype(BF16)
    return x1 + m[5:6] * _dot(hid, wd_ref[...])


def _mix0_kernel(x_ref, a_ref, yf_ref, yb_ref, z_ref, mod_ref, snw_ref, nw_ref,
                 wa_ref, wy_ref, wg_ref, wu_ref, wd_ref, o_ref):
    m = mod_ref[0]
    y = (yf_ref[0] + yb_ref[0]).T
    gy = y * _silu(z_ref[0])
    gw = SSM_D_INNER // SSM_GROUPS
    gn = jnp.concatenate([_rms(gy[:, i * gw:(i + 1) * gw]) for i in range(SSM_GROUPS)], axis=1)
    yn = (gn * snw_ref[...]).astype(BF16)
    mix = _dot(a_ref[0], wa_ref[...]) + _dot(yn, wy_ref[...])
    x1 = x_ref[0] + m[2:3] * mix
    o_ref[0] = _ffn_tail(x1, m, nw_ref[...], wg_ref, wu_ref, wd_ref)


def _mix0(x, a, yf, yb, z, mod, snw, nw, wa, wy, wg, wu, wd, tm):
    b, t, d = x.shape
    ffn = wg.shape[1]
    row = lambda bi, i: (bi, i, 0)
    col = lambda bi, i: (bi, 0, i)
    return pl.pallas_call(
        _mix0_kernel,
        grid=(b, t // tm),
        in_specs=[
            pl.BlockSpec((1, tm, d), row),
            pl.BlockSpec((1, tm, Q_DIM), row),
            pl.BlockSpec((1, SSM_D_INNER, tm), col),
            pl.BlockSpec((1, SSM_D_INNER, tm), col),
            pl.BlockSpec((1, tm, SSM_D_INNER), row),
            pl.BlockSpec((1, N_MOD, d), lambda bi, i: (bi, 0, 0)),
            _const_spec((1, SSM_D_INNER)),
            _const_spec((1, d)),
            _const_spec((Q_DIM, d)),
            _const_spec((SSM_D_INNER, d)),
            _const_spec((d, ffn)),
            _const_spec((d, ffn)),
            _const_spec((ffn, d)),
        ],
        out_specs=pl.BlockSpec((1, tm, d), row),
        out_shape=jax.ShapeDtypeStruct((b, t, d), F32),
        compiler_params=_cparams(2),
        name="mix0",
    )(x, a, yf, yb, z, mod, snw, nw, wa, wy, wg, wu, wd)


def _layer1_kernel(x_ref, xp_ref, xn_ref, mod_ref, nmix_ref, win_ref, cw_ref, wout_ref, nffn_ref,
                   wg_ref, wu_ref, wd_ref, fn_ref, o_ref, v_ref):
    i = pl.program_id(1)
    last = pl.num_programs(1) - 1
    m = mod_ref[0]
    tm, d = x_ref.shape[1], x_ref.shape[2]
    x = x_ref[0]
    xa = jnp.concatenate([xp_ref[0], x, xn_ref[0]], axis=0)
    h = _rms(xa) * nmix_ref[...]
    h = (h * (1.0 + m[1:2]) + m[0:1]).astype(BF16)
    p = _dot(h, win_ref[...])
    v = p[:, d:2 * d] * p[:, 2 * d:]
    rid = lax.broadcasted_iota(jnp.int32, (tm + 2 * SUBLANES, 1), 0)
    keep = jnp.logical_and(jnp.logical_or(rid >= SUBLANES, i > 0),
                           jnp.logical_or(rid < SUBLANES + tm, i < last))
    v_ref[...] = jnp.where(keep, v, 0.0)
    cw = cw_ref[...]
    conv = (v_ref[SUBLANES - 1:SUBLANES - 1 + tm] * cw[0:1] + v_ref[SUBLANES:SUBLANES + tm] * cw[1:2]
            + v_ref[SUBLANES + 1:SUBLANES + 1 + tm] * cw[2:3])
    gated = (p[SUBLANES:SUBLANES + tm, :d] * conv).astype(BF16)
    x1 = x + m[2:3] * _dot(gated, wout_ref[...])
    x2 = _ffn_tail(x1, m, nffn_ref[...], wg_ref, wu_ref, wd_ref)
    o_ref[0] = _rms(x2) * fn_ref[...]


def _layer1(x, mod, nmix, win, cw, wout, nffn, wg, wu, wd, fn, tm):
    b, t, d = x.shape
    ffn = wg.shape[1]
    r8 = tm // SUBLANES
    nblk8 = t // SUBLANES
    return pl.pallas_call(
        _layer1_kernel,
        grid=(b, t // tm),
        in_specs=[
            pl.BlockSpec((1, tm, d), lambda bi, i: (bi, i, 0)),
            pl.BlockSpec((1, SUBLANES, d), lambda bi, i: (bi, jnp.maximum(i * r8 - 1, 0), 0)),
            pl.BlockSpec((1, SUBLANES, d), lambda bi, i: (bi, jnp.minimum((i + 1) * r8, nblk8 - 1), 0)),
            pl.BlockSpec((1, N_MOD, d), lambda bi, i: (bi, 0, 0)),
            _const_spec((1, d)),
            _const_spec((d, 3 * d)),
            _const_spec((3, d)),
            _const_spec((d, d)),
            _const_spec((1, d)),
            _const_spec((d, ffn)),
            _const_spec((d, ffn)),
            _const_spec((ffn, d)),
            _const_spec((1, d)),
        ],
        out_specs=pl.BlockSpec((1, tm, d), lambda bi, i: (bi, i, 0)),
        out_shape=jax.ShapeDtypeStruct((b, t, d), F32),
        scratch_shapes=[pltpu.VMEM((tm + 2 * SUBLANES, d), F32)],
        compiler_params=_cparams(2),
        name="layer1",
    )(x, x, x, mod, nmix, win, cw, wout, nffn, wg, wu, wd, fn)


def _rope_tables(n_tokens):
    rows = n_tokens // GRID_W
    row = jnp.repeat(jnp.arange(rows), GRID_W).astype(F32)
    col = jnp.tile(jnp.arange(GRID_W), rows).astype(F32)
    inv = 1.0 / (ROPE_THETA ** (jnp.arange(0, AXIS_DIM, 2, dtype=F32) / AXIS_DIM))
    ang = jnp.concatenate([row[:, None] * inv, col[:, None] * inv], axis=-1)
    return jnp.cos(ang), jnp.sin(ang)


_HEAD_PERM = np.concatenate([np.arange(0, HEAD_DIM, 2), np.arange(1, HEAD_DIM, 2)])


def _qk_tables(cos, sin, q_gain, k_gain):
    half = HEAD_DIM // 2
    sign = jnp.concatenate([-jnp.ones((half,), F32), jnp.ones((half,), F32)])
    c64 = jnp.concatenate([cos, cos], axis=1)
    s64 = jnp.concatenate([sin, sin], axis=1) * sign
    swap = np.concatenate([np.arange(half, HEAD_DIM), np.arange(0, half)])

    def tables(gain, scale):
        gp = gain[_HEAD_PERM] * scale
        return c64 * gp[None, :], s64 * gp[swap][None, :]

    aq, bq = tables(q_gain, HEAD_DIM ** -0.5 * LOG2_E)
    ak, bk = tables(k_gain, 1.0)
    return aq.T, bq.T, jnp.tile(ak, (1, ATTN_KV_HEADS)), jnp.tile(bk, (1, ATTN_KV_HEADS))


def kernel(x, c, ctx, c_ctx, ada_w, ada_b, norm_mix, norm_ffn, ffn_w_gate, ffn_w_up, ffn_w_down, hy_w_in,
           hy_q_norm, hy_k_norm, hy_conv_w, hy_conv_b, hy_dt_bias, hy_a_log, hy_d_skip, hy_ssm_norm, hy_w_out,
           sc_w_in, sc_conv_w, sc_w_out, final_norm):
    b, t, d = x.shape
    nctx = ctx.shape[1]
    tm_in = min(512, t)
    tm_mix = min(256, t)

    n_rows = -(-(b + 1) // SUBLANES) * SUBLANES
    cc = jnp.zeros((n_rows, d), F32).at[:b].set(c).at[b].set(c_ctx)
    mods = _modulation(cc, ada_w, ada_b).reshape(ada_w.shape[0], n_rows, N_MOD, d)
    mod0, mod0_ctx, mod1 = mods[0, :b], mods[0, b:b + 1], mods[1, :b]

    w_in = hy_w_in[0]
    idx_k, idx_v, idx_z = Q_DIM, Q_DIM + KV_DIM, Q_DIM + 2 * KV_DIM
    idx_xbc = idx_z + SSM_D_INNER
    idx_dt = idx_xbc + SSM_CONV_DIM
    q_cols = (np.arange(ATTN_HEADS)[:, None] * HEAD_DIM + _HEAD_PERM[None, :]).reshape(-1)
    k_cols = idx_k + (np.arange(ATTN_KV_HEADS)[:, None] * HEAD_DIM + _HEAD_PERM[None, :]).reshape(-1)
    dt_pad = jnp.zeros((d, LANES - 2 * SSM_HEADS), F32)
    wn = jnp.concatenate([w_in[:, k_cols], w_in[:, idx_z:idx_dt], w_in[:, idx_dt:], dt_pad], axis=1).astype(BF16)
    wt = jnp.concatenate([w_in[:, q_cols], w_in[:, idx_v:idx_z]], axis=1).T.astype(BF16)
    nw0 = norm_mix[0][None, :]

    cos, sin = _rope_tables(t)
    aq, bq, ck, sk = _qk_tables(cos, sin, hy_q_norm[0], hy_k_norm[0])
    ones_c, zeros_c = jnp.ones((nctx, AXIS_DIM), F32), jnp.zeros((nctx, AXIS_DIM), F32)
    aq_c, bq_c, ck_c, sk_c = _qk_tables(ones_c, zeros_c, hy_q_norm[0], hy_k_norm[0])
    seg = np.arange(KV_DIM) // HEAD_DIM
    bd = jnp.asarray(seg[:, None] == seg[None, :], BF16)

    qt, vt, k, z, xb = _inproj(x, mod0, True, nw0, wn, wt, aq, bq, ck, sk, bd, tm_in)
    _, vct, kc, _, xb_c = _inproj(ctx, mod0_ctx, False, nw0, wn, wt, aq_c, bq_c, ck_c, sk_c, bd, nctx)

    tk = t + nctx
    kt = next(c for c in ATTN_KEY_TILES if tk % c == 0)
    k_all = jnp.concatenate([k, kc], axis=1)
    vt_all = jnp.concatenate([vt, vct], axis=2)
    vt_tiles = vt_all.reshape(b, ATTN_KV_HEADS, HEAD_DIM, tk // kt, kt).transpose(0, 1, 3, 2, 4)
    score_bound = 1.02 * HEAD_DIM ** 0.5 * LOG2_E * jnp.max(jnp.abs(hy_q_norm[0])) * jnp.max(jnp.abs(hy_k_norm[0]))
    tq = min(128, t)
    a_lat = lax.cond(score_bound <= SAFE_SCORE_BOUND,
                     functools.partial(_attention_bounded, tq=tq), functools.partial(_attention, tq=tq),
                     qt, k_all, vt_tiles)

    conv_w = jnp.concatenate([hy_conv_w[0], jnp.zeros((3, LANES), F32)], axis=1)
    conv_b = jnp.concatenate([hy_conv_b[0], jnp.zeros((LANES,), F32)])[None, :]
    dtb = jnp.concatenate([hy_dt_bias[0].reshape(-1), jnp.zeros((LANES - 2 * SSM_HEADS,), F32)])[None, :]
    alog_col = jnp.broadcast_to(hy_a_log[0].reshape(-1, 1), (2 * SSM_HEADS, SSM_CHUNK))
    dsk_col = jnp.broadcast_to(jnp.repeat(hy_d_skip[0], SSM_HEAD_DIM)[:, None], (SSM_D_INNER, SSM_CHUNK))
    ssd_lat = _ssdprep(xb, conv_w, conv_b, dtb, tm_in)
    ssd_ctx = _ssdprep(xb_c, conv_w, conv_b, dtb, nctx)
    yf, yb = _ssd(ssd_lat, ssd_ctx, alog_col, dsk_col, toks=min(2 * SSM_CHUNK, t))

    w_out = hy_w_out[0].astype(BF16)
    x = _mix0(x, a_lat, yf, yb, z, mod0, hy_ssm_norm[0][None, :], norm_ffn[0][None, :],
              w_out[:Q_DIM], w_out[Q_DIM:], ffn_w_gate[0].astype(BF16), ffn_w_up[0].astype(BF16),
              ffn_w_down[0].astype(BF16), tm_mix)

    return _layer1(x, mod1, norm_mix[1][None, :], sc_w_in[0].astype(BF16), sc_conv_w[0], sc_w_out[0].astype(BF16),
                   norm_ffn[1][None, :], ffn_w_gate[1].astype(BF16), ffn_w_up[1].astype(BF16),
                   ffn_w_down[1].astype(BF16), final_norm[None, :], tm_mix)
```

```python
import functools

import jax
import jax.numpy as jnp
import numpy as np
from jax import lax
from jax.experimental import pallas as pl
from jax.experimental.pallas import tpu as pltpu

F32 = jnp.float32
BF16 = jnp.bfloat16

EPS = 1e-6
N_MOD = 6
GRID_W = 64
ROPE_THETA = 10000.0

ATTN_HEADS = 8
ATTN_KV_HEADS = 2
HEAD_DIM = 64
AXIS_DIM = HEAD_DIM // 2
KV_GROUP = ATTN_HEADS // ATTN_KV_HEADS
Q_DIM = ATTN_HEADS * HEAD_DIM
KV_DIM = ATTN_KV_HEADS * HEAD_DIM

SSM_HEADS = 8
SSM_HEAD_DIM = 64
SSM_D_INNER = SSM_HEADS * SSM_HEAD_DIM
SSM_GROUPS = 2
SSM_STATE = 64
SSM_CHUNK = 128
SSM_CONV_DIM = SSM_D_INNER + 2 * SSM_GROUPS * SSM_STATE
SSM_HEADS_PER_GROUP = SSM_HEADS // SSM_GROUPS

LANES = 128
SUBLANES = 8
XBCDT_W = SSM_CONV_DIM + LANES
NEG_BIG = -1e30
LOG2_E = 1.4426950408889634
ATTN_KEY_TILES = (768, 512, 256, 128)
ROW_GROUPS = 2
FFN_CHUNK = 768
ATTN_SUB_TILE = 256
ATTN_PV_LAG = 3
SAFE_SCORE_BOUND = 40.0
SCORE_BOUND_SLACK = 1.001

VMEM_LIMIT = 56 * 1024 * 1024


def _cparams(n_axes):
    return pltpu.CompilerParams(dimension_semantics=("parallel",) * n_axes, vmem_limit_bytes=VMEM_LIMIT)


def _const_spec(shape):
    nd = len(shape)
    return pl.BlockSpec(shape, lambda *_: (0,) * nd, pipeline_mode=pl.Buffered(1))


def _rms(x):
    return x * lax.rsqrt(jnp.mean(x * x, axis=-1, keepdims=True) + EPS)


def _silu(x):
    return x * (1.0 / (1.0 + jnp.exp(-x)))


def _dot(a, b):
    return jnp.dot(a, b, preferred_element_type=F32)


def _dot_exact(a, b):
    return jnp.dot(a, b, preferred_element_type=F32, precision=lax.Precision.HIGHEST)


def _mod_kernel(c_ref, w_ref, b_ref, o_ref):
    c = c_ref[...]
    o_ref[0] = _dot_exact(_silu(c), w_ref[0]) + b_ref[0]


def _modulation(cc, ada_w, ada_b):
    depth, d, nd = ada_w.shape
    r = cc.shape[0]
    tn = d
    return pl.pallas_call(
        _mod_kernel,
        grid=(depth, nd // tn),
        in_specs=[
            pl.BlockSpec((r, d), lambda l, j: (0, 0)),
            pl.BlockSpec((1, d, tn), lambda l, j: (l, 0, j)),
            pl.BlockSpec((1, 1, tn), lambda l, j: (l, 0, j)),
        ],
        out_specs=pl.BlockSpec((1, r, tn), lambda l, j: (l, 0, j)),
        out_shape=jax.ShapeDtypeStruct((depth, r, nd), F32),
        compiler_params=_cparams(2),
        name="mod",
    )(cc, ada_w, ada_b.reshape(depth, 1, nd))


def _inproj_kernel(x_ref, mod_ref, nw_ref, wn_ref, wt_ref, aq_ref, bq_ref, ck_ref, sk_ref, bd_ref,
                   qt_ref, vt_ref, k_ref, z_ref, xb_ref):
    m = mod_ref[0]
    half = HEAD_DIM // 2
    groups = _row_groups(x_ref.shape[1])
    hbs = []
    for rows in groups:
        h = _rms(x_ref[0, rows, :]) * nw_ref[...]
        hbs.append((h * (1.0 + m[1:2]) + m[0:1]).astype(BF16))
    proj = []
    for hb in hbs:
        pn = _dot(hb, wn_ref[...])
        pt = lax.dot_general(wt_ref[...], hb, (((1,), (1,)), ((), ())), preferred_element_type=F32)
        proj.append((pn, pt))

    for rows, (pn, pt) in zip(groups, proj):
        n = rows.stop - rows.start
        kx = pn[:, :KV_DIM]
        sq = kx * kx
        hi = sq.astype(BF16)
        lo = (sq - hi.astype(F32)).astype(BF16)
        ss = _dot(hi, bd_ref[...]) + _dot(lo, bd_ref[...])
        rs = lax.rsqrt(ss * (1.0 / HEAD_DIM) + EPS)
        lane = lax.broadcasted_iota(jnp.int32, kx.shape, 1)
        swap = jnp.where((lane & half) == 0, pltpu.roll(kx, KV_DIM - half, 1), pltpu.roll(kx, half, 1))
        k_ref[0, rows, :] = (rs * (kx * ck_ref[rows, :] + swap * sk_ref[rows, :])).astype(BF16)

        q3 = pt[:Q_DIM].reshape(ATTN_HEADS, HEAD_DIM, n)
        qrs = lax.rsqrt(jnp.mean(q3 * q3, axis=1, keepdims=True) + EPS)
        qsw = jnp.concatenate([q3[:, half:], q3[:, :half]], axis=1)
        qo = qrs * (q3 * aq_ref[:, rows][None] + qsw * bq_ref[:, rows][None])
        qt_ref[0, :, rows] = qo.reshape(Q_DIM, n).astype(BF16)

        vt_ref[0, :, rows] = pt[Q_DIM:].astype(BF16)
        z_ref[0, rows, :] = pn[:, KV_DIM:KV_DIM + SSM_D_INNER]
        xb_ref[0, rows, :] = pn[:, KV_DIM + SSM_D_INNER:]


def _inproj(x, mod, per_batch_mod, nw, wn, wt, aq, bq, ck, sk, bd, tm):
    b, t, d = x.shape
    nt = t // tm
    mod_idx = (lambda bi, i: (bi, 0, 0)) if per_batch_mod else (lambda bi, i: (0, 0, 0))
    wn_cols = wn.shape[1]
    return pl.pallas_call(
        _inproj_kernel,
        grid=(b, nt),
        in_specs=[
            pl.BlockSpec((1, tm, d), lambda bi, i: (bi, i, 0)),
            pl.BlockSpec((1, N_MOD, d), mod_idx),
            _const_spec((1, d)),
            _const_spec((d, wn_cols)),
            _const_spec((Q_DIM + KV_DIM, d)),
            pl.BlockSpec((HEAD_DIM, tm), lambda bi, i: (0, i)),
            pl.BlockSpec((HEAD_DIM, tm), lambda bi, i: (0, i)),
            pl.BlockSpec((tm, KV_DIM), lambda bi, i: (i, 0)),
            pl.BlockSpec((tm, KV_DIM), lambda bi, i: (i, 0)),
            _const_spec((KV_DIM, KV_DIM)),
        ],
        out_specs=[
            pl.BlockSpec((1, Q_DIM, tm), lambda bi, i: (bi, 0, i)),
            pl.BlockSpec((1, KV_DIM, tm), lambda bi, i: (bi, 0, i)),
            pl.BlockSpec((1, tm, KV_DIM), lambda bi, i: (bi, i, 0)),
            pl.BlockSpec((1, tm, SSM_D_INNER), lambda bi, i: (bi, i, 0)),
            pl.BlockSpec((1, tm, XBCDT_W), lambda bi, i: (bi, i, 0)),
        ],
        out_shape=[
            jax.ShapeDtypeStruct((b, Q_DIM, t), BF16),
            jax.ShapeDtypeStruct((b, KV_DIM, t), BF16),
            jax.ShapeDtypeStruct((b, t, KV_DIM), BF16),
            jax.ShapeDtypeStruct((b, t, SSM_D_INNER), F32),
            jax.ShapeDtypeStruct((b, t, XBCDT_W), F32),
        ],
        compiler_params=_cparams(2),
        name="inproj",
    )(x, mod, nw, wn, wt, aq, bq, ck, sk, bd)


def _attn_kernel(qt_ref, k_ref, vt_ref, o_ref, s0_ref, s1_ref, p0_ref, p1_ref, m0_ref, m1_ref, *, nq):
    g = pl.program_id(1)
    i = pl.program_id(2)
    tq = qt_ref.shape[2]
    cols = KV_GROUP * tq
    nk, kt = s0_ref.shape[0], s0_ref.shape[1]
    s_refs, m_refs, p_refs = (s0_ref, s1_ref), (m0_ref, m1_ref), (p0_ref, p1_ref)

    def run(a_buf, b_buf):
        do_a, do_bc = a_buf is not None, b_buf is not None
        if do_a:
            q4 = qt_ref[0]
            qcat = jnp.concatenate([q4[h * HEAD_DIM:(h + 1) * HEAD_DIM] for h in range(KV_GROUP)], axis=1)
            qp = jnp.concatenate([qcat] * ATTN_KV_HEADS, axis=0)
            rowgrp = lax.broadcasted_iota(jnp.int32, qp.shape, 0) // HEAD_DIM
            qp = jnp.where(rowgrp == g, qp, jnp.zeros_like(qp))
        if do_bc:
            m8 = jnp.broadcast_to(jnp.max(m_refs[b_buf][...], axis=0, keepdims=True), (SUBLANES, cols))

        def stage_a(j, mx):
            ktile = k_ref[0, pl.ds(pl.multiple_of(j * kt, kt), kt), :]
            s = _dot(ktile, qp)
            s_refs[a_buf][j] = s
            return jnp.maximum(mx, jnp.max(s.reshape(kt // SUBLANES, SUBLANES, cols), axis=0))

        def stage_b(j, par, ls):
            s = s_refs[b_buf][j].reshape(kt // SUBLANES, SUBLANES, cols)
            p = jnp.exp2(s - m8[None])
            p_refs[par][...] = p.reshape(kt, cols).astype(BF16)
            return ls + jnp.sum(p, axis=0)

        def stage_c(j, par, acc):
            return acc + _dot(vt_ref[0, 0, j], p_refs[par][...])

        def tile(j, par, carry):
            mx, ls, acc = carry
            if do_a:
                mx = stage_a(j, mx)
            if do_bc:
                acc = stage_c(j - 1, 1 - par, acc)
                ls = stage_b(j, par, ls)
            return mx, ls, acc

        mx = jnp.full((SUBLANES, cols), NEG_BIG, F32)
        ls = jnp.zeros((SUBLANES, cols), F32)
        acc = jnp.zeros((HEAD_DIM, cols), F32)
        if do_a:
            mx = stage_a(0, mx)
        if do_bc:
            ls = stage_b(0, 0, ls)

        def body(t, carry):
            j = 1 + 2 * t
            return tile(j + 1, 0, tile(j, 1, carry))

        carry = lax.fori_loop(0, (nk - 1) // 2, body, (mx, ls, acc), unroll=True)
        if (nk - 1) % 2:
            carry = tile(nk - 1, (nk - 1) % 2, carry)
        mx, ls, acc = carry
        if do_a:
            m_refs[a_buf][...] = mx
        if do_bc:
            acc = stage_c(nk - 1, (nk - 1) % 2, acc)
            o = acc * (1.0 / jnp.sum(ls, axis=0, keepdims=True))
            o4 = jnp.concatenate([o[:, h * tq:(h + 1) * tq] for h in range(KV_GROUP)], axis=0)
            o_ref[0] = o4.T.astype(BF16)

    @pl.when(i == 0)
    def _():
        run(0, None)

    inner = jnp.logical_and(i > 0, i < nq)

    @pl.when(jnp.logical_and(inner, i % 2 == 1))
    def _():
        run(1, 0)

    @pl.when(jnp.logical_and(inner, i % 2 == 0))
    def _():
        run(0, 1)

    @pl.when(i == nq)
    def _():
        run(None, (nq - 1) % 2)


def _attention(qt, k_all, vt_tiles, tq):
    b, _, t = qt.shape
    tk = k_all.shape[1]
    nk, kt = vt_tiles.shape[2], vt_tiles.shape[4]
    nq = t // tq
    gw = KV_GROUP * HEAD_DIM
    cols = KV_GROUP * tq
    return pl.pallas_call(
        functools.partial(_attn_kernel, nq=nq),
        grid=(b, ATTN_KV_HEADS, nq + 1),
        in_specs=[
            pl.BlockSpec((1, gw, tq), lambda bi, g, i: (bi, g, jnp.minimum(i, nq - 1))),
            pl.BlockSpec((1, tk, KV_DIM), lambda bi, g, i: (bi, 0, 0)),
            pl.BlockSpec((1, 1, nk, HEAD_DIM, kt), lambda bi, g, i: (bi, g, 0, 0, 0)),
        ],
        out_specs=pl.BlockSpec((1, tq, gw), lambda bi, g, i: (bi, jnp.maximum(i - 1, 0), g)),
        out_shape=jax.ShapeDtypeStruct((b, t, Q_DIM), BF16),
        scratch_shapes=(
            [pltpu.VMEM((nk, kt, cols), F32)] * 2
            + [pltpu.VMEM((kt, cols), BF16)] * 2
            + [pltpu.VMEM((SUBLANES, cols), F32)] * 2
        ),
        compiler_params=pltpu.CompilerParams(dimension_semantics=("parallel", "parallel", "arbitrary"),
                                             vmem_limit_bytes=VMEM_LIMIT),
        name="attn",
    )(qt, k_all, vt_tiles)


def _attn_bounded_kernel(qt_ref, k_ref, vt_ref, o_ref, km_ref, *p_refs):
    g = pl.program_id(1)
    i = pl.program_id(2)
    tq = qt_ref.shape[2]
    cols = KV_GROUP * tq
    sub = p_refs[0].shape[0]
    n_all = k_ref.shape[1] // sub

    @pl.when(i == 0)
    def _():
        lanegrp = lax.broadcasted_iota(jnp.int32, (sub, KV_DIM), 1) // HEAD_DIM

        def body(j, mx):
            kf = k_ref[0, pl.ds(pl.multiple_of(j * sub, sub), sub), :].astype(F32)
            return jnp.maximum(mx, jnp.sum(jnp.where(lanegrp == g, kf * kf, 0.0), axis=1, keepdims=True))

        mx = lax.fori_loop(0, n_all, body, jnp.zeros((sub, 1), F32))
        km_ref[...] = jnp.broadcast_to(jnp.sqrt(jnp.max(mx, axis=0, keepdims=True)), km_ref.shape)

    q4 = qt_ref[0]
    qf = q4.astype(F32)
    qn = jnp.sqrt(jnp.sum((qf * qf).reshape(KV_GROUP, HEAD_DIM, tq), axis=1))
    qn = jnp.concatenate([qn[h:h + 1] for h in range(KV_GROUP)], axis=1)
    kmax = jnp.concatenate([km_ref[0:1, :]] * (cols // LANES), axis=1)
    mb8 = jnp.broadcast_to(qn * kmax * SCORE_BOUND_SLACK, (SUBLANES, cols))
    qcat = jnp.concatenate([q4[h * HEAD_DIM:(h + 1) * HEAD_DIM] for h in range(KV_GROUP)], axis=1)
    qp = jnp.concatenate([qcat] * ATTN_KV_HEADS, axis=0)
    rowgrp = lax.broadcasted_iota(jnp.int32, qp.shape, 0) // HEAD_DIM
    qp = jnp.where(rowgrp == g, qp, jnp.zeros_like(qp))

    ls = jnp.zeros((SUBLANES, cols), F32)
    acc = jnp.zeros((HEAD_DIM, cols), F32)
    ring = len(p_refs)
    lag = ring - 1

    def pv(u, acc):
        return acc + _dot(vt_ref[0, :, u * sub:(u + 1) * sub], p_refs[u % ring][...])

    for u in range(n_all):
        s = _dot(k_ref[0, u * sub:(u + 1) * sub, :], qp)
        if u >= lag:
            acc = pv(u - lag, acc)
        p = jnp.exp2(s.reshape(sub // SUBLANES, SUBLANES, cols) - mb8[None])
        p_refs[u % ring][...] = p.reshape(sub, cols).astype(BF16)
        ls = ls + jnp.sum(p, axis=0)
    for u in range(max(n_all - lag, 0), n_all):
        acc = pv(u, acc)
    o = acc * (1.0 / jnp.sum(ls, axis=0, keepdims=True))
    o4 = jnp.concatenate([o[:, h * tq:(h + 1) * tq] for h in range(KV_GROUP)], axis=0)
    o_ref[0] = o4.T.astype(BF16)


def _attention_bounded(qt, k_all, vt_all, tq):
    b, _, t = qt.shape
    tk = k_all.shape[1]
    sub = next(c for c in (ATTN_SUB_TILE, LANES) if tk % c == 0)
    gw = KV_GROUP * HEAD_DIM
    cols = KV_GROUP * tq
    return pl.pallas_call(
        _attn_bounded_kernel,
        grid=(b, ATTN_KV_HEADS, t // tq),
        in_specs=[
            pl.BlockSpec((1, gw, tq), lambda bi, g, i: (bi, g, i)),
            pl.BlockSpec((1, tk, KV_DIM), lambda bi, g, i: (bi, 0, 0)),
            pl.BlockSpec((1, HEAD_DIM, tk), lambda bi, g, i: (bi, g, 0)),
        ],
        out_specs=pl.BlockSpec((1, tq, gw), lambda bi, g, i: (bi, i, g)),
        out_shape=jax.ShapeDtypeStruct((b, t, Q_DIM), BF16),
        scratch_shapes=([pltpu.VMEM((SUBLANES, LANES), F32)]
                        + [pltpu.VMEM((sub, cols), BF16)] * (ATTN_PV_LAG + 1)),
        compiler_params=pltpu.CompilerParams(dimension_semantics=("parallel", "parallel", "arbitrary"),
                                             vmem_limit_bytes=VMEM_LIMIT),
        name="attn_bounded",
    )(qt, k_all, vt_all)


def _attention_exact(qt, k_all, vt_all, tq):
    b, _, tk = vt_all.shape
    kt = next(c for c in ATTN_KEY_TILES if tk % c == 0)
    vt_tiles = vt_all.reshape(b, ATTN_KV_HEADS, HEAD_DIM, tk // kt, kt).transpose(0, 1, 3, 2, 4)
    return _attention(qt, k_all, vt_tiles, tq)


def _ssdprep_kernel(x_ref, xp_ref, xn_ref, w_ref, b_ref, dtb_ref, xst_ref, bn_ref, ct_ref, dtt_ref, sc_ref):
    i = pl.program_id(1)
    last = pl.num_programs(1) - 1
    tp = x_ref.shape[1]
    x = x_ref[0]
    sc_ref[0:SUBLANES] = jnp.where(i > 0, xp_ref[0], 0.0)
    sc_ref[SUBLANES:SUBLANES + tp] = x
    sc_ref[SUBLANES + tp:] = jnp.where(i < last, xn_ref[0], 0.0)
    prev = sc_ref[SUBLANES - 1:SUBLANES - 1 + tp]
    nxt = sc_ref[SUBLANES + 1:SUBLANES + 1 + tp]
    w = w_ref[...]
    y = prev * w[0:1] + x * w[1:2] + nxt * w[2:3] + b_ref[...]
    u = _silu(y[:, :SSM_CONV_DIM])
    bc = SSM_D_INNER + SSM_GROUPS * SSM_STATE
    xst_ref[0] = u[:, :SSM_D_INNER].T
    bn_ref[0] = u[:, SSM_D_INNER:bc].astype(BF16)
    ct_ref[0] = u[:, bc:].T.astype(BF16)
    d = x[:, SSM_CONV_DIM:] + dtb_ref[...]
    dt = jnp.maximum(d, 0.0) + jnp.log1p(jnp.exp(-jnp.abs(d)))
    dtt_ref[0] = dt.T[:2 * SSM_HEADS]


def _ssdprep(xb, w, bias, dtb, tp):
    b, t, wd = xb.shape
    r8 = tp // SUBLANES
    nblk8 = t // SUBLANES
    gn = SSM_GROUPS * SSM_STATE
    col = lambda bi, i: (bi, 0, i)
    return pl.pallas_call(
        _ssdprep_kernel,
        grid=(b, t // tp),
        in_specs=[
            pl.BlockSpec((1, tp, wd), lambda bi, i: (bi, i, 0)),
            pl.BlockSpec((1, SUBLANES, wd), lambda bi, i: (bi, jnp.maximum(i * r8 - 1, 0), 0)),
            pl.BlockSpec((1, SUBLANES, wd), lambda bi, i: (bi, jnp.minimum((i + 1) * r8, nblk8 - 1), 0)),
            _const_spec((3, wd)),
            _const_spec((1, wd)),
            _const_spec((1, LANES)),
        ],
        out_specs=[
            pl.BlockSpec((1, SSM_D_INNER, tp), col),
            pl.BlockSpec((1, tp, gn), lambda bi, i: (bi, i, 0)),
            pl.BlockSpec((1, gn, tp), col),
            pl.BlockSpec((1, 2 * SSM_HEADS, tp), col),
        ],
        out_shape=[
            jax.ShapeDtypeStruct((b, SSM_D_INNER, t), F32),
            jax.ShapeDtypeStruct((b, t, gn), BF16),
            jax.ShapeDtypeStruct((b, gn, t), BF16),
            jax.ShapeDtypeStruct((b, 2 * SSM_HEADS, t), F32),
        ],
        scratch_shapes=[pltpu.VMEM((tp + 2 * SUBLANES, wd), F32)],
        compiler_params=_cparams(2),
        name="ssdprep",
    )(xb, xb, xb, w, bias, dtb)


class _SsdChunk:
    def __init__(self, xst, bn, ct, dtt, a_col, direction):
        self.xst, self.bn, self.ct, self.dtt, self.a_col, self.direction = xst, bn, ct, dtt, a_col, direction

    def decay(self):
        ln = self.xst.shape[1]
        da = self.dtt * self.a_col
        ri = lax.broadcasted_iota(jnp.int32, (ln, ln), 0)
        ci = lax.broadcasted_iota(jnp.int32, (ln, ln), 1)
        self.reach = (ci >= ri) if self.direction == 0 else (ci <= ri)
        self.cs_row = _dot_exact(da, self.reach.astype(F32))
        self.tot = _dot_exact(da, jnp.ones((ln, ln), F32))

    def local(self):
        ln = self.xst.shape[1]
        gn = SSM_GROUPS * SSM_STATE
        cs_col = jnp.concatenate([self.cs_row, jnp.zeros((ln - 2 * SSM_HEADS, ln), F32)], axis=0).T
        rowgrp = lax.broadcasted_iota(jnp.int32, (gn, ln), 0) // SSM_STATE
        self.ctz, self.y_diag, self.xw, self.e_row, self.e_tot = [], [], [], [], []
        for grp in range(SSM_GROUPS):
            ctz = jnp.where(rowgrp == grp, self.ct, jnp.zeros_like(self.ct))
            cbt = _dot(self.bn, ctz)
            y_parts, xw_parts = [], []
            for e in range(SSM_HEADS_PER_GROUP):
                hd = grp * SSM_HEADS_PER_GROUP + e
                c = self.direction * SSM_HEADS + hd
                row = self.cs_row[c:c + 1, :]
                lmt = jnp.exp(jnp.where(self.reach, row - cs_col[:, c:c + 1], NEG_BIG))
                mt = (cbt * lmt).astype(BF16)
                xdt = self.xst[hd * SSM_HEAD_DIM:(hd + 1) * SSM_HEAD_DIM, :] * self.dtt[c:c + 1, :]
                y_parts.append(_dot(xdt.astype(BF16), mt))
                xw_parts.append((xdt * jnp.exp(self.tot[c:c + 1, :] - row)).astype(BF16))
                self.e_row.append(jnp.exp(row))
                self.e_tot.append(jnp.exp(self.tot[c:c + 1, :gn]))
            self.ctz.append(ctz)
            self.y_diag.append(y_parts)
            self.xw.append(jnp.concatenate(xw_parts, axis=0))

    def carry(self, h_ref):
        gp = SSM_HEADS_PER_GROUP * SSM_HEAD_DIM
        ys = []
        for grp in range(SSM_GROUPS):
            hg = h_ref[grp * gp:(grp + 1) * gp, :]
            y_off = _dot(hg.astype(BF16), self.ctz[grp])
            st = _dot(self.xw[grp], self.bn)
            h_parts = []
            for e in range(SSM_HEADS_PER_GROUP):
                hd = grp * SSM_HEADS_PER_GROUP + e
                rows = slice(e * SSM_HEAD_DIM, (e + 1) * SSM_HEAD_DIM)
                ys.append(self.y_diag[grp][e] + y_off[rows] * self.e_row[hd])
                h_parts.append(hg[rows] * self.e_tot[hd])
            h_ref[grp * gp:(grp + 1) * gp, :] = jnp.concatenate(h_parts, axis=0) + st
        return jnp.concatenate(ys, axis=0)


def _ssd_run(chunks, h_refs):
    for ch in chunks:
        ch.decay()
    for ch in chunks:
        ch.local()
    return [ch.carry(h_refs[ch.direction]) for ch in chunks]


def _ssd_kernel(xf_ref, xb_ref, bf_ref, bb_ref, cf_ref, cb_ref, df_ref, db_ref, xc_ref, bc_ref, cc_ref, dc_ref,
                alog_ref, dsk_ref, yf_ref, yb_ref, hf_ref, hb_ref):
    c = pl.program_id(1)
    a_col = -jnp.exp(alog_ref[...])
    ln = SSM_CHUNK
    h_refs = (hf_ref, hb_ref)

    def chunk(x_ref, b_ref, c_ref, d_ref, s, direction):
        tok = slice(s * ln, (s + 1) * ln)
        return _SsdChunk(x_ref[0, :, tok], b_ref[0, tok, :], c_ref[0, :, tok], d_ref[0, :, tok], a_col, direction)

    @pl.when(c == 0)
    def _():
        n_ctx = xc_ref.shape[2] // ln
        for direction in range(2):
            h_refs[direction][...] = jnp.zeros_like(h_refs[direction])
        chunks = []
        for k in range(n_ctx):
            chunks.append(chunk(xc_ref, bc_ref, cc_ref, dc_ref, k, 0))
            chunks.append(chunk(xc_ref, bc_ref, cc_ref, dc_ref, n_ctx - 1 - k, 1))
        _ssd_run(chunks, h_refs)

    n_sub = xf_ref.shape[2] // ln
    chunks = []
    for k in range(n_sub):
        chunks.append(chunk(xf_ref, bf_ref, cf_ref, df_ref, k, 0))
        chunks.append(chunk(xb_ref, bb_ref, cb_ref, db_ref, n_sub - 1 - k, 1))
    ys = _ssd_run(chunks, h_refs)
    for k in range(n_sub):
        tok = slice(k * ln, (k + 1) * ln)
        yf_ref[0, :, tok] = ys[2 * k] + xf_ref[0, :, tok] * dsk_ref[...]
        tok = slice((n_sub - 1 - k) * ln, (n_sub - k) * ln)
        yb_ref[0, :, tok] = ys[2 * k + 1]


def _ssd(lat, ctx, alog_col, dsk_col, toks):
    xst, bn, ct, dtt = lat
    b, di, t = xst.shape
    gn = bn.shape[2]
    nh2 = dtt.shape[1]
    nctx = ctx[0].shape[2]
    ns = t // toks
    fcol = lambda bi, c: (bi, 0, c)
    bcol = lambda bi, c: (bi, 0, ns - 1 - c)
    frow = lambda bi, c: (bi, c, 0)
    brow = lambda bi, c: (bi, ns - 1 - c, 0)
    whole = lambda bi, c: (bi, 0, 0)
    return pl.pallas_call(
        _ssd_kernel,
        grid=(b, ns),
        in_specs=[
            pl.BlockSpec((1, di, toks), fcol), pl.BlockSpec((1, di, toks), bcol),
            pl.BlockSpec((1, toks, gn), frow), pl.BlockSpec((1, toks, gn), brow),
            pl.BlockSpec((1, gn, toks), fcol), pl.BlockSpec((1, gn, toks), bcol),
            pl.BlockSpec((1, nh2, toks), fcol), pl.BlockSpec((1, nh2, toks), bcol),
            pl.BlockSpec((1, di, nctx), whole), pl.BlockSpec((1, nctx, gn), whole),
            pl.BlockSpec((1, gn, nctx), whole), pl.BlockSpec((1, nh2, nctx), whole),
            _const_spec((nh2, SSM_CHUNK)),
            _const_spec((di, SSM_CHUNK)),
        ],
        out_specs=[pl.BlockSpec((1, di, toks), fcol), pl.BlockSpec((1, di, toks), bcol)],
        out_shape=[jax.ShapeDtypeStruct((b, di, t), F32), jax.ShapeDtypeStruct((b, di, t), F32)],
        scratch_shapes=[pltpu.VMEM((di, gn), F32)] * 2,
        compiler_params=pltpu.CompilerParams(dimension_semantics=("parallel", "arbitrary"),
                                             vmem_limit_bytes=VMEM_LIMIT),
        name="ssd",
    )(xst, xst, bn, bn, ct, ct, dtt, dtt, *ctx, alog_col, dsk_col)


def _row_groups(tm):
    n = ROW_GROUPS if tm % (ROW_GROUPS * SUBLANES * 2) == 0 else 1
    return [slice(i * (tm // n), (i + 1) * (tm // n)) for i in range(n)]


def _ffn_tail(x1s, m, nw, wg_ref, wu_ref, wd_ref):
    hs = [((_rms(x1) * nw) * (1.0 + m[4:5]) + m[3:4]).astype(BF16) for x1 in x1s]
    ffn = wg_ref.shape[1]
    fs = [None] * len(x1s)
    for c0 in range(0, ffn, FFN_CHUNK):
        c1 = min(c0 + FFN_CHUNK, ffn)
        hids = [(_silu(_dot(h, wg_ref[:, c0:c1])) * _dot(h, wu_ref[:, c0:c1])).astype(BF16) for h in hs]
        for i, hid in enumerate(hids):
            part = _dot(hid, wd_ref[c0:c1, :])
            fs[i] = part if fs[i] is None else fs[i] + part
    return [x1 + m[5:6] * f for x1, f in zip(x1s, fs)]


def _mix0_kernel(x_ref, a_ref, yf_ref, yb_ref, z_ref, mod_ref, snw_ref, nw_ref,
                 wa_ref, wy_ref, wg_ref, wu_ref, wd_ref, o_ref):
    m = mod_ref[0]
    gw = SSM_D_INNER // SSM_GROUPS
    groups = _row_groups(x_ref.shape[1])
    yns = []
    for rows in groups:
        y = (yf_ref[0, :, rows] + yb_ref[0, :, rows]).T
        gy = y * _silu(z_ref[0, rows, :])
        gn = jnp.concatenate([_rms(gy[:, i * gw:(i + 1) * gw]) for i in range(SSM_GROUPS)], axis=1)
        yns.append((gn * snw_ref[...]).astype(BF16))
    mix_a = [_dot(a_ref[0, rows, :], wa_ref[...]) for rows in groups]
    x1s = []
    for rows, yn, ma in zip(groups, yns, mix_a):
        x1s.append(x_ref[0, rows, :] + m[2:3] * (ma + _dot(yn, wy_ref[...])))
    for rows, out in zip(groups, _ffn_tail(x1s, m, nw_ref[...], wg_ref, wu_ref, wd_ref)):
        o_ref[0, rows, :] = out


def _mix0(x, a, yf, yb, z, mod, snw, nw, wa, wy, wg, wu, wd, tm):
    b, t, d = x.shape
    ffn = wg.shape[1]
    row = lambda bi, i: (bi, i, 0)
    col = lambda bi, i: (bi, 0, i)
    return pl.pallas_call(
        _mix0_kernel,
        grid=(b, t // tm),
        in_specs=[
            pl.BlockSpec((1, tm, d), row),
            pl.BlockSpec((1, tm, Q_DIM), row),
            pl.BlockSpec((1, SSM_D_INNER, tm), col),
            pl.BlockSpec((1, SSM_D_INNER, tm), col),
            pl.BlockSpec((1, tm, SSM_D_INNER), row),
            pl.BlockSpec((1, N_MOD, d), lambda bi, i: (bi, 0, 0)),
            _const_spec((1, SSM_D_INNER)),
            _const_spec((1, d)),
            _const_spec((Q_DIM, d)),
            _const_spec((SSM_D_INNER, d)),
            _const_spec((d, ffn)),
            _const_spec((d, ffn)),
            _const_spec((ffn, d)),
        ],
        out_specs=pl.BlockSpec((1, tm, d), row),
        out_shape=jax.ShapeDtypeStruct((b, t, d), F32),
        compiler_params=_cparams(2),
        name="mix0",
    )(x, a, yf, yb, z, mod, snw, nw, wa, wy, wg, wu, wd)


def _layer1_kernel(x_ref, xp_ref, xn_ref, mod_ref, nmix_ref, win_ref, cw_ref, wout_ref, nffn_ref,
                   wg_ref, wu_ref, wd_ref, fn_ref, o_ref, v_ref):
    i = pl.program_id(1)
    last = pl.num_programs(1) - 1
    m = mod_ref[0]
    tm, d = x_ref.shape[1], x_ref.shape[2]
    x = x_ref[0]
    xa = jnp.concatenate([xp_ref[0], x, xn_ref[0]], axis=0)
    h = _rms(xa) * nmix_ref[...]
    h = h * (1.0 + m[1:2]) + m[0:1]
    pcu = _dot(h.astype(BF16), win_ref[:, d:])
    pgb = _dot(h[SUBLANES:SUBLANES + tm].astype(BF16), win_ref[:, :d])
    v = pcu[:, :d] * pcu[:, d:]
    rid = lax.broadcasted_iota(jnp.int32, (tm + 2 * SUBLANES, 1), 0)
    keep = jnp.logical_and(jnp.logical_or(rid >= SUBLANES, i > 0),
                           jnp.logical_or(rid < SUBLANES + tm, i < last))
    v_ref[...] = jnp.where(keep, v, 0.0)
    cw = cw_ref[...]
    groups = _row_groups(tm)
    x1s = []
    for rows in groups:
        r0 = rows.start + SUBLANES
        n = rows.stop - rows.start
        conv = (v_ref[r0 - 1:r0 - 1 + n] * cw[0:1] + v_ref[r0:r0 + n] * cw[1:2] + v_ref[r0 + 1:r0 + 1 + n] * cw[2:3])
        gated = (pgb[rows] * conv).astype(BF16)
        x1s.append(x[rows] + m[2:3] * _dot(gated, wout_ref[...]))
    for rows, x2 in zip(groups, _ffn_tail(x1s, m, nffn_ref[...], wg_ref, wu_ref, wd_ref)):
        o_ref[0, rows, :] = _rms(x2) * fn_ref[...]


def _layer1(x, mod, nmix, win, cw, wout, nffn, wg, wu, wd, fn, tm):
    b, t, d = x.shape
    ffn = wg.shape[1]
    r8 = tm // SUBLANES
    nblk8 = t // SUBLANES
    return pl.pallas_call(
        _layer1_kernel,
        grid=(b, t // tm),
        in_specs=[
            pl.BlockSpec((1, tm, d), lambda bi, i: (bi, i, 0)),
            pl.BlockSpec((1, SUBLANES, d), lambda bi, i: (bi, jnp.maximum(i * r8 - 1, 0), 0)),
            pl.BlockSpec((1, SUBLANES, d), lambda bi, i: (bi, jnp.minimum((i + 1) * r8, nblk8 - 1), 0)),
            pl.BlockSpec((1, N_MOD, d), lambda bi, i: (bi, 0, 0)),
            _const_spec((1, d)),
            _const_spec((d, 3 * d)),
            _const_spec((3, d)),
            _const_spec((d, d)),
            _const_spec((1, d)),
            _const_spec((d, ffn)),
            _const_spec((d, ffn)),
            _const_spec((ffn, d)),
            _const_spec((1, d)),
        ],
        out_specs=pl.BlockSpec((1, tm, d), lambda bi, i: (bi, i, 0)),
        out_shape=jax.ShapeDtypeStruct((b, t, d), F32),
        scratch_shapes=[pltpu.VMEM((tm + 2 * SUBLANES, d), F32)],
        compiler_params=_cparams(2),
        name="layer1",
    )(x, x, x, mod, nmix, win, cw, wout, nffn, wg, wu, wd, fn)


def _rope_tables(n_tokens):
    rows = n_tokens // GRID_W
    row = jnp.repeat(jnp.arange(rows), GRID_W).astype(F32)
    col = jnp.tile(jnp.arange(GRID_W), rows).astype(F32)
    inv = 1.0 / (ROPE_THETA ** (jnp.arange(0, AXIS_DIM, 2, dtype=F32) / AXIS_DIM))
    ang = jnp.concatenate([row[:, None] * inv, col[:, None] * inv], axis=-1)
    return jnp.cos(ang), jnp.sin(ang)


_HEAD_PERM = np.concatenate([np.arange(0, HEAD_DIM, 2), np.arange(1, HEAD_DIM, 2)])


def _qk_tables(cos, sin, q_gain, k_gain):
    half = HEAD_DIM // 2
    sign = jnp.concatenate([-jnp.ones((half,), F32), jnp.ones((half,), F32)])
    c64 = jnp.concatenate([cos, cos], axis=1)
    s64 = jnp.concatenate([sin, sin], axis=1) * sign
    swap = np.concatenate([np.arange(half, HEAD_DIM), np.arange(0, half)])

    def tables(gain, scale):
        gp = gain[_HEAD_PERM] * scale
        return c64 * gp[None, :], s64 * gp[swap][None, :]

    aq, bq = tables(q_gain, HEAD_DIM ** -0.5 * LOG2_E)
    ak, bk = tables(k_gain, 1.0)
    return aq.T, bq.T, jnp.tile(ak, (1, ATTN_KV_HEADS)), jnp.tile(bk, (1, ATTN_KV_HEADS))


def kernel(x, c, ctx, c_ctx, ada_w, ada_b, norm_mix, norm_ffn, ffn_w_gate, ffn_w_up, ffn_w_down, hy_w_in,
           hy_q_norm, hy_k_norm, hy_conv_w, hy_conv_b, hy_dt_bias, hy_a_log, hy_d_skip, hy_ssm_norm, hy_w_out,
           sc_w_in, sc_conv_w, sc_w_out, final_norm):
    b, t, d = x.shape
    nctx = ctx.shape[1]
    tm_in = min(512, t)
    tm_mix = min(512, t)

    n_rows = -(-(b + 1) // SUBLANES) * SUBLANES
    cc = jnp.zeros((n_rows, d), F32).at[:b].set(c).at[b].set(c_ctx)
    mods = _modulation(cc, ada_w, ada_b).reshape(ada_w.shape[0], n_rows, N_MOD, d)
    mod0, mod0_ctx, mod1 = mods[0, :b], mods[0, b:b + 1], mods[1, :b]

    w_in = hy_w_in[0]
    idx_k, idx_v, idx_z = Q_DIM, Q_DIM + KV_DIM, Q_DIM + 2 * KV_DIM
    idx_xbc = idx_z + SSM_D_INNER
    idx_dt = idx_xbc + SSM_CONV_DIM
    q_cols = (np.arange(ATTN_HEADS)[:, None] * HEAD_DIM + _HEAD_PERM[None, :]).reshape(-1)
    k_cols = idx_k + (np.arange(ATTN_KV_HEADS)[:, None] * HEAD_DIM + _HEAD_PERM[None, :]).reshape(-1)
    dt_pad = jnp.zeros((d, LANES - 2 * SSM_HEADS), F32)
    wn = jnp.concatenate([w_in[:, k_cols], w_in[:, idx_z:idx_dt], w_in[:, idx_dt:], dt_pad], axis=1).astype(BF16)
    wt = jnp.concatenate([w_in[:, q_cols], w_in[:, idx_v:idx_z]], axis=1).T.astype(BF16)
    nw0 = norm_mix[0][None, :]

    cos, sin = _rope_tables(t)
    aq, bq, ck, sk = _qk_tables(cos, sin, hy_q_norm[0], hy_k_norm[0])
    ones_c, zeros_c = jnp.ones((nctx, AXIS_DIM), F32), jnp.zeros((nctx, AXIS_DIM), F32)
    aq_c, bq_c, ck_c, sk_c = _qk_tables(ones_c, zeros_c, hy_q_norm[0], hy_k_norm[0])
    seg = np.arange(KV_DIM) // HEAD_DIM
    bd = jnp.asarray(seg[:, None] == seg[None, :], BF16)

    qt, vt, k, z, xb = _inproj(x, mod0, True, nw0, wn, wt, aq, bq, ck, sk, bd, tm_in)
    _, vct, kc, _, xb_c = _inproj(ctx, mod0_ctx, False, nw0, wn, wt, aq_c, bq_c, ck_c, sk_c, bd, nctx)

    k_all = jnp.concatenate([k, kc], axis=1)
    vt_all = jnp.concatenate([vt, vct], axis=2)
    score_bound = 1.02 * HEAD_DIM ** 0.5 * LOG2_E * jnp.max(jnp.abs(hy_q_norm[0])) * jnp.max(jnp.abs(hy_k_norm[0]))
    tq = min(128, t)
    a_lat = lax.cond(score_bound <= SAFE_SCORE_BOUND,
                     functools.partial(_attention_bounded, tq=tq), functools.partial(_attention_exact, tq=tq),
                     qt, k_all, vt_all)

    conv_w = jnp.concatenate([hy_conv_w[0], jnp.zeros((3, LANES), F32)], axis=1)
    conv_b = jnp.concatenate([hy_conv_b[0], jnp.zeros((LANES,), F32)])[None, :]
    dtb = jnp.concatenate([hy_dt_bias[0].reshape(-1), jnp.zeros((LANES - 2 * SSM_HEADS,), F32)])[None, :]
    alog_col = jnp.broadcast_to(hy_a_log[0].reshape(-1, 1), (2 * SSM_HEADS, SSM_CHUNK))
    dsk_col = jnp.broadcast_to(jnp.repeat(hy_d_skip[0], SSM_HEAD_DIM)[:, None], (SSM_D_INNER, SSM_CHUNK))
    ssd_lat = _ssdprep(xb, conv_w, conv_b, dtb, tm_in)
    ssd_ctx = _ssdprep(xb_c, conv_w, conv_b, dtb, nctx)
    yf, yb = _ssd(ssd_lat, ssd_ctx, alog_col, dsk_col, toks=min(2 * SSM_CHUNK, t))

    w_out = hy_w_out[0].astype(BF16)
    x = _mix0(x, a_lat, yf, yb, z, mod0, hy_ssm_norm[0][None, :], norm_ffn[0][None, :],
              w_out[:Q_DIM], w_out[Q_DIM:], ffn_w_gate[0].astype(BF16), ffn_w_up[0].astype(BF16),
              ffn_w_down[0].astype(BF16), tm_mix)

    return _layer1(x, mod1, norm_mix[1][None, :], sc_w_in[0].astype(BF16), sc_conv_w[0], sc_w_out[0].astype(BF16),
                   norm_ffn[1][None, :], ffn_w_gate[1].astype(BF16), ffn_w_up[1].astype(BF16),
                   ffn_w_down[1].astype(BF16), final_norm[None, :], tm_mix)
```

```python
import functools

import jax
import jax.numpy as jnp
import numpy as np
from jax import lax
from jax.experimental import pallas as pl
from jax.experimental.pallas import tpu as pltpu

F32 = jnp.float32
BF16 = jnp.bfloat16

EPS = 1e-6
N_MOD = 6
GRID_W = 64
ROPE_THETA = 10000.0

ATTN_HEADS = 8
ATTN_KV_HEADS = 2
HEAD_DIM = 64
AXIS_DIM = HEAD_DIM // 2
KV_GROUP = ATTN_HEADS // ATTN_KV_HEADS
Q_DIM = ATTN_HEADS * HEAD_DIM
KV_DIM = ATTN_KV_HEADS * HEAD_DIM

SSM_HEADS = 8
SSM_HEAD_DIM = 64
SSM_D_INNER = SSM_HEADS * SSM_HEAD_DIM
SSM_GROUPS = 2
SSM_STATE = 64
SSM_CHUNK = 128
SSM_CONV_DIM = SSM_D_INNER + 2 * SSM_GROUPS * SSM_STATE
SSM_HEADS_PER_GROUP = SSM_HEADS // SSM_GROUPS

LANES = 128
SUBLANES = 8
XBCDT_W = SSM_CONV_DIM + LANES
NEG_BIG = -1e30
LOG2_E = 1.4426950408889634
ATTN_KEY_TILES = (768, 512, 256, 128)
ROW_GROUPS = 2
FFN_CHUNK = 768
ATTN_Q_BLOCKS = 4
ATTN_SUB_TILE = 256
ATTN_PV_LAG = 3
SAFE_SCORE_BOUND = 40.0
SCORE_BOUND_SLACK = 1.001

VMEM_LIMIT = 56 * 1024 * 1024


def _cparams(n_axes):
    return pltpu.CompilerParams(dimension_semantics=("parallel",) * n_axes, vmem_limit_bytes=VMEM_LIMIT)


def _const_spec(shape):
    nd = len(shape)
    return pl.BlockSpec(shape, lambda *_: (0,) * nd, pipeline_mode=pl.Buffered(1))


def _rms(x):
    return x * lax.rsqrt(jnp.mean(x * x, axis=-1, keepdims=True) + EPS)


def _silu(x):
    return x * (1.0 / (1.0 + jnp.exp(-x)))


def _dot(a, b):
    return jnp.dot(a, b, preferred_element_type=F32)


def _dot_exact(a, b):
    return jnp.dot(a, b, preferred_element_type=F32, precision=lax.Precision.HIGHEST)


def _mod_kernel(c_ref, w_ref, b_ref, o_ref):
    c = c_ref[...]
    o_ref[0] = _dot_exact(_silu(c), w_ref[0]) + b_ref[0]


def _modulation(cc, ada_w, ada_b):
    depth, d, nd = ada_w.shape
    r = cc.shape[0]
    tn = d
    return pl.pallas_call(
        _mod_kernel,
        grid=(depth, nd // tn),
        in_specs=[
            pl.BlockSpec((r, d), lambda l, j: (0, 0)),
            pl.BlockSpec((1, d, tn), lambda l, j: (l, 0, j)),
            pl.BlockSpec((1, 1, tn), lambda l, j: (l, 0, j)),
        ],
        out_specs=pl.BlockSpec((1, r, tn), lambda l, j: (l, 0, j)),
        out_shape=jax.ShapeDtypeStruct((depth, r, nd), F32),
        compiler_params=_cparams(2),
        name="mod",
    )(cc, ada_w, ada_b.reshape(depth, 1, nd))


def _inproj_kernel(x_ref, xp_ref, xn_ref, mod_ref, nw_ref, wx_ref, wkz_ref, wt_ref, aq_ref, bq_ref, ck_ref, sk_ref,
                   bd_ref, cw_ref, cb_ref, dtb_ref,
                   qt_ref, vt_ref, k_ref, z_ref, xst_ref, bn_ref, ct_ref, dtt_ref, sc_ref):
    i = pl.program_id(1)
    last = pl.num_programs(1) - 1
    m = mod_ref[0]
    tm = x_ref.shape[1]
    half = HEAD_DIM // 2
    xa = jnp.concatenate([xp_ref[0], x_ref[0], xn_ref[0]], axis=0)
    h = _rms(xa) * nw_ref[...]
    h = h * (1.0 + m[1:2]) + m[0:1]
    hc = h[SUBLANES:SUBLANES + tm].astype(BF16)
    px = _dot(h.astype(BF16), wx_ref[...])
    pkz = _dot(hc, wkz_ref[...])
    pt = lax.dot_general(wt_ref[...], hc, (((1,), (1,)), ((), ())), preferred_element_type=F32)

    rid = lax.broadcasted_iota(jnp.int32, (tm + 2 * SUBLANES, 1), 0)
    keep = jnp.logical_and(jnp.logical_or(rid >= SUBLANES, i > 0), jnp.logical_or(rid < SUBLANES + tm, i < last))
    sc_ref[...] = jnp.where(keep, px, 0.0)
    cur = sc_ref[SUBLANES:SUBLANES + tm]
    w = cw_ref[...]
    y = (sc_ref[SUBLANES - 1:SUBLANES - 1 + tm] * w[0:1] + cur * w[1:2]
         + sc_ref[SUBLANES + 1:SUBLANES + 1 + tm] * w[2:3] + cb_ref[...])
    u = _silu(y[:, :SSM_CONV_DIM])
    bc = SSM_D_INNER + SSM_GROUPS * SSM_STATE
    xst_ref[0] = u[:, :SSM_D_INNER].T
    bn_ref[0] = u[:, SSM_D_INNER:bc].astype(BF16)
    ct_ref[0] = u[:, bc:].T.astype(BF16)
    d = cur[:, SSM_CONV_DIM:] + dtb_ref[...]
    dt = jnp.maximum(d, 0.0) + jnp.log1p(jnp.exp(-jnp.abs(d)))
    dtt_ref[0] = dt.T[:2 * SSM_HEADS]

    kx = pkz[:, :KV_DIM]
    sq = kx * kx
    hi = sq.astype(BF16)
    lo = (sq - hi.astype(F32)).astype(BF16)
    ss = _dot(hi, bd_ref[...]) + _dot(lo, bd_ref[...])
    rs = lax.rsqrt(ss * (1.0 / HEAD_DIM) + EPS)
    lane = lax.broadcasted_iota(jnp.int32, kx.shape, 1)
    swap = jnp.where((lane & half) == 0, pltpu.roll(kx, KV_DIM - half, 1), pltpu.roll(kx, half, 1))
    k_ref[0] = (rs * (kx * ck_ref[...] + swap * sk_ref[...])).astype(BF16)

    q3 = pt[:Q_DIM].reshape(ATTN_HEADS, HEAD_DIM, tm)
    qrs = lax.rsqrt(jnp.mean(q3 * q3, axis=1, keepdims=True) + EPS)
    qsw = jnp.concatenate([q3[:, half:], q3[:, :half]], axis=1)
    qo = qrs * (q3 * aq_ref[...][None] + qsw * bq_ref[...][None])
    qt_ref[0] = qo.reshape(Q_DIM, tm).astype(BF16)

    vt_ref[0] = pt[Q_DIM:].astype(BF16)
    z_ref[0] = pkz[:, KV_DIM:]


def _inproj(x, mod, per_batch_mod, nw, wx, wkz, wt, aq, bq, ck, sk, bd, cw, cb, dtb, tm):
    b, t, d = x.shape
    r8 = tm // SUBLANES
    nblk8 = t // SUBLANES
    gn = SSM_GROUPS * SSM_STATE
    mod_idx = (lambda bi, i: (bi, 0, 0)) if per_batch_mod else (lambda bi, i: (0, 0, 0))
    row = lambda bi, i: (bi, i, 0)
    col = lambda bi, i: (bi, 0, i)
    return pl.pallas_call(
        _inproj_kernel,
        grid=(b, t // tm),
        in_specs=[
            pl.BlockSpec((1, tm, d), row),
            pl.BlockSpec((1, SUBLANES, d), lambda bi, i: (bi, jnp.maximum(i * r8 - 1, 0), 0)),
            pl.BlockSpec((1, SUBLANES, d), lambda bi, i: (bi, jnp.minimum((i + 1) * r8, nblk8 - 1), 0)),
            pl.BlockSpec((1, N_MOD, d), mod_idx),
            _const_spec((1, d)),
            _const_spec((d, XBCDT_W)),
            _const_spec((d, KV_DIM + SSM_D_INNER)),
            _const_spec((Q_DIM + KV_DIM, d)),
            pl.BlockSpec((HEAD_DIM, tm), lambda bi, i: (0, i)),
            pl.BlockSpec((HEAD_DIM, tm), lambda bi, i: (0, i)),
            pl.BlockSpec((tm, KV_DIM), lambda bi, i: (i, 0)),
            pl.BlockSpec((tm, KV_DIM), lambda bi, i: (i, 0)),
            _const_spec((KV_DIM, KV_DIM)),
            _const_spec((3, XBCDT_W)),
            _const_spec((1, XBCDT_W)),
            _const_spec((1, LANES)),
        ],
        out_specs=[
            pl.BlockSpec((1, Q_DIM, tm), col),
            pl.BlockSpec((1, KV_DIM, tm), col),
            pl.BlockSpec((1, tm, KV_DIM), row),
            pl.BlockSpec((1, tm, SSM_D_INNER), row),
            pl.BlockSpec((1, SSM_D_INNER, tm), col),
            pl.BlockSpec((1, tm, gn), row),
            pl.BlockSpec((1, gn, tm), col),
            pl.BlockSpec((1, 2 * SSM_HEADS, tm), col),
        ],
        out_shape=[
            jax.ShapeDtypeStruct((b, Q_DIM, t), BF16),
            jax.ShapeDtypeStruct((b, KV_DIM, t), BF16),
            jax.ShapeDtypeStruct((b, t, KV_DIM), BF16),
            jax.ShapeDtypeStruct((b, t, SSM_D_INNER), F32),
            jax.ShapeDtypeStruct((b, SSM_D_INNER, t), F32),
            jax.ShapeDtypeStruct((b, t, gn), BF16),
            jax.ShapeDtypeStruct((b, gn, t), BF16),
            jax.ShapeDtypeStruct((b, 2 * SSM_HEADS, t), F32),
        ],
        scratch_shapes=[pltpu.VMEM((tm + 2 * SUBLANES, XBCDT_W), F32)],
        compiler_params=_cparams(2),
        name="inproj",
    )(x, x, x, mod, nw, wx, wkz, wt, aq, bq, ck, sk, bd, cw, cb, dtb)


def _attn_kernel(qt_ref, k_ref, vt_ref, o_ref, s0_ref, s1_ref, p0_ref, p1_ref, m0_ref, m1_ref, *, nq):
    g = pl.program_id(1)
    i = pl.program_id(2)
    tq = qt_ref.shape[2]
    cols = KV_GROUP * tq
    nk, kt = s0_ref.shape[0], s0_ref.shape[1]
    s_refs, m_refs, p_refs = (s0_ref, s1_ref), (m0_ref, m1_ref), (p0_ref, p1_ref)

    def run(a_buf, b_buf):
        do_a, do_bc = a_buf is not None, b_buf is not None
        if do_a:
            q4 = qt_ref[0]
            qcat = jnp.concatenate([q4[h * HEAD_DIM:(h + 1) * HEAD_DIM] for h in range(KV_GROUP)], axis=1)
            qp = jnp.concatenate([qcat] * ATTN_KV_HEADS, axis=0)
            rowgrp = lax.broadcasted_iota(jnp.int32, qp.shape, 0) // HEAD_DIM
            qp = jnp.where(rowgrp == g, qp, jnp.zeros_like(qp))
        if do_bc:
            m8 = jnp.broadcast_to(jnp.max(m_refs[b_buf][...], axis=0, keepdims=True), (SUBLANES, cols))

        def stage_a(j, mx):
            ktile = k_ref[0, pl.ds(pl.multiple_of(j * kt, kt), kt), :]
            s = _dot(ktile, qp)
            s_refs[a_buf][j] = s
            return jnp.maximum(mx, jnp.max(s.reshape(kt // SUBLANES, SUBLANES, cols), axis=0))

        def stage_b(j, par, ls):
            s = s_refs[b_buf][j].reshape(kt // SUBLANES, SUBLANES, cols)
            p = jnp.exp2(s - m8[None])
            p_refs[par][...] = p.reshape(kt, cols).astype(BF16)
            return ls + jnp.sum(p, axis=0)

        def stage_c(j, par, acc):
            return acc + _dot(vt_ref[0, 0, j], p_refs[par][...])

        def tile(j, par, carry):
            mx, ls, acc = carry
            if do_a:
                mx = stage_a(j, mx)
            if do_bc:
                acc = stage_c(j - 1, 1 - par, acc)
                ls = stage_b(j, par, ls)
            return mx, ls, acc

        mx = jnp.full((SUBLANES, cols), NEG_BIG, F32)
        ls = jnp.zeros((SUBLANES, cols), F32)
        acc = jnp.zeros((HEAD_DIM, cols), F32)
        if do_a:
            mx = stage_a(0, mx)
        if do_bc:
            ls = stage_b(0, 0, ls)

        def body(t, carry):
            j = 1 + 2 * t
            return tile(j + 1, 0, tile(j, 1, carry))

        carry = lax.fori_loop(0, (nk - 1) // 2, body, (mx, ls, acc), unroll=True)
        if (nk - 1) % 2:
            carry = tile(nk - 1, (nk - 1) % 2, carry)
        mx, ls, acc = carry
        if do_a:
            m_refs[a_buf][...] = mx
        if do_bc:
            acc = stage_c(nk - 1, (nk - 1) % 2, acc)
            o = acc * (1.0 / jnp.sum(ls, axis=0, keepdims=True))
            o4 = jnp.concatenate([o[:, h * tq:(h + 1) * tq] for h in range(KV_GROUP)], axis=0)
            o_ref[0] = o4.T.astype(BF16)

    @pl.when(i == 0)
    def _():
        run(0, None)

    inner = jnp.logical_and(i > 0, i < nq)

    @pl.when(jnp.logical_and(inner, i % 2 == 1))
    def _():
        run(1, 0)

    @pl.when(jnp.logical_and(inner, i % 2 == 0))
    def _():
        run(0, 1)

    @pl.when(i == nq)
    def _():
        run(None, (nq - 1) % 2)


def _attention(qt, k_all, vt_tiles, tq):
    b, _, t = qt.shape
    tk = k_all.shape[1]
    nk, kt = vt_tiles.shape[2], vt_tiles.shape[4]
    nq = t // tq
    gw = KV_GROUP * HEAD_DIM
    cols = KV_GROUP * tq
    return pl.pallas_call(
        functools.partial(_attn_kernel, nq=nq),
        grid=(b, ATTN_KV_HEADS, nq + 1),
        in_specs=[
            pl.BlockSpec((1, gw, tq), lambda bi, g, i: (bi, g, jnp.minimum(i, nq - 1))),
            pl.BlockSpec((1, tk, KV_DIM), lambda bi, g, i: (bi, 0, 0)),
            pl.BlockSpec((1, 1, nk, HEAD_DIM, kt), lambda bi, g, i: (bi, g, 0, 0, 0)),
        ],
        out_specs=pl.BlockSpec((1, tq, gw), lambda bi, g, i: (bi, jnp.maximum(i - 1, 0), g)),
        out_shape=jax.ShapeDtypeStruct((b, t, Q_DIM), BF16),
        scratch_shapes=(
            [pltpu.VMEM((nk, kt, cols), F32)] * 2
            + [pltpu.VMEM((kt, cols), BF16)] * 2
            + [pltpu.VMEM((SUBLANES, cols), F32)] * 2
        ),
        compiler_params=pltpu.CompilerParams(dimension_semantics=("parallel", "parallel", "arbitrary"),
                                             vmem_limit_bytes=VMEM_LIMIT),
        name="attn",
    )(qt, k_all, vt_tiles)


def _attn_bounded_kernel(qt_ref, k_ref, vt_ref, o_ref, km_ref, *p_refs):
    g = pl.program_id(1)
    i = pl.program_id(2)
    cols = p_refs[0].shape[1]
    tq = cols // KV_GROUP
    n_blk = qt_ref.shape[2] // tq
    sub = p_refs[0].shape[0]
    n_all = k_ref.shape[1] // sub

    @pl.when(i == 0)
    def _():
        lanegrp = lax.broadcasted_iota(jnp.int32, (sub, KV_DIM), 1) // HEAD_DIM

        def body(j, mx):
            kf = k_ref[0, pl.ds(pl.multiple_of(j * sub, sub), sub), :].astype(F32)
            return jnp.maximum(mx, jnp.sum(jnp.where(lanegrp == g, kf * kf, 0.0), axis=1, keepdims=True))

        mx = lax.fori_loop(0, n_all, body, jnp.zeros((sub, 1), F32))
        km_ref[...] = jnp.broadcast_to(jnp.sqrt(jnp.max(mx, axis=0, keepdims=True)), km_ref.shape)

    kmax = jnp.concatenate([km_ref[0:1, :]] * (cols // LANES), axis=1)
    qps, mb8s = [], []
    for blk in range(n_blk):
        q4 = qt_ref[0, :, blk * tq:(blk + 1) * tq]
        qf = q4.astype(F32)
        qn = jnp.sqrt(jnp.sum((qf * qf).reshape(KV_GROUP, HEAD_DIM, tq), axis=1))
        qn = jnp.concatenate([qn[h:h + 1] for h in range(KV_GROUP)], axis=1)
        mb8s.append(jnp.broadcast_to(qn * kmax * SCORE_BOUND_SLACK, (SUBLANES, cols)))
        qcat = jnp.concatenate([q4[h * HEAD_DIM:(h + 1) * HEAD_DIM] for h in range(KV_GROUP)], axis=1)
        qp = jnp.concatenate([qcat] * ATTN_KV_HEADS, axis=0)
        rowgrp = lax.broadcasted_iota(jnp.int32, qp.shape, 0) // HEAD_DIM
        qps.append(jnp.where(rowgrp == g, qp, jnp.zeros_like(qp)))

    ring = len(p_refs)
    lag = ring - 1
    ls = [jnp.zeros((SUBLANES, cols), F32) for _ in range(n_blk)]
    acc = [jnp.zeros((HEAD_DIM, cols), F32) for _ in range(n_blk)]

    def pv(w):
        blk, u = divmod(w, n_all)
        acc[blk] = acc[blk] + _dot(vt_ref[0, :, u * sub:(u + 1) * sub], p_refs[w % ring][...])
        if u == n_all - 1:
            o = acc[blk] * (1.0 / jnp.sum(ls[blk], axis=0, keepdims=True))
            o4 = jnp.concatenate([o[:, h * tq:(h + 1) * tq] for h in range(KV_GROUP)], axis=0)
            o_ref[0, blk * tq:(blk + 1) * tq, :] = o4.T.astype(BF16)

    n_items = n_blk * n_all
    for w in range(n_items):
        blk, u = divmod(w, n_all)
        s = _dot(k_ref[0, u * sub:(u + 1) * sub, :], qps[blk])
        if w >= lag:
            pv(w - lag)
        p = jnp.exp2(s.reshape(sub // SUBLANES, SUBLANES, cols) - mb8s[blk][None])
        p_refs[w % ring][...] = p.reshape(sub, cols).astype(BF16)
        ls[blk] = ls[blk] + jnp.sum(p, axis=0)
    for w in range(max(n_items - lag, 0), n_items):
        pv(w)


def _attention_bounded(qt, k_all, vt_all, tq):
    b, _, t = qt.shape
    tk = k_all.shape[1]
    sub = next(c for c in (ATTN_SUB_TILE, LANES) if tk % c == 0)
    gw = KV_GROUP * HEAD_DIM
    cols = KV_GROUP * tq
    tstep = tq * (ATTN_Q_BLOCKS if t % (tq * ATTN_Q_BLOCKS) == 0 else 1)
    return pl.pallas_call(
        _attn_bounded_kernel,
        grid=(b, ATTN_KV_HEADS, t // tstep),
        in_specs=[
            pl.BlockSpec((1, gw, tstep), lambda bi, g, i: (bi, g, i)),
            pl.BlockSpec((1, tk, KV_DIM), lambda bi, g, i: (bi, 0, 0)),
            pl.BlockSpec((1, HEAD_DIM, tk), lambda bi, g, i: (bi, g, 0)),
        ],
        out_specs=pl.BlockSpec((1, tstep, gw), lambda bi, g, i: (bi, i, g)),
        out_shape=jax.ShapeDtypeStruct((b, t, Q_DIM), BF16),
        scratch_shapes=([pltpu.VMEM((SUBLANES, LANES), F32)]
                        + [pltpu.VMEM((sub, cols), BF16)] * (ATTN_PV_LAG + 1)),
        compiler_params=pltpu.CompilerParams(dimension_semantics=("parallel", "parallel", "arbitrary"),
                                             vmem_limit_bytes=VMEM_LIMIT),
        name="attn_bounded",
    )(qt, k_all, vt_all)


def _attention_exact(qt, k_all, vt_all, tq):
    b, _, tk = vt_all.shape
    kt = next(c for c in ATTN_KEY_TILES if tk % c == 0)
    vt_tiles = vt_all.reshape(b, ATTN_KV_HEADS, HEAD_DIM, tk // kt, kt).transpose(0, 1, 3, 2, 4)
    return _attention(qt, k_all, vt_tiles, tq)


class _SsdChunk:
    def __init__(self, xst, bn, ct, dtt, a_col, direction):
        self.xst, self.bn, self.ct, self.dtt, self.a_col, self.direction = xst, bn, ct, dtt, a_col, direction

    def decay(self):
        ln = self.xst.shape[1]
        da = self.dtt * self.a_col
        ri = lax.broadcasted_iota(jnp.int32, (ln, ln), 0)
        ci = lax.broadcasted_iota(jnp.int32, (ln, ln), 1)
        self.reach = (ci >= ri) if self.direction == 0 else (ci <= ri)
        self.cs_row = _dot_exact(da, self.reach.astype(F32))
        self.tot = _dot_exact(da, jnp.ones((ln, ln), F32))

    def local(self):
        ln = self.xst.shape[1]
        gn = SSM_GROUPS * SSM_STATE
        cs_col = jnp.concatenate([self.cs_row, jnp.zeros((ln - 2 * SSM_HEADS, ln), F32)], axis=0).T
        rowgrp = lax.broadcasted_iota(jnp.int32, (gn, ln), 0) // SSM_STATE
        self.ctz, self.y_diag, self.xw, self.e_row, self.e_tot = [], [], [], [], []
        for grp in range(SSM_GROUPS):
            ctz = jnp.where(rowgrp == grp, self.ct, jnp.zeros_like(self.ct))
            cbt = _dot(self.bn, ctz)
            y_parts, xw_parts = [], []
            for e in range(SSM_HEADS_PER_GROUP):
                hd = grp * SSM_HEADS_PER_GROUP + e
                c = self.direction * SSM_HEADS + hd
                row = self.cs_row[c:c + 1, :]
                lmt = jnp.exp(jnp.where(self.reach, row - cs_col[:, c:c + 1], NEG_BIG))
                mt = (cbt * lmt).astype(BF16)
                xdt = self.xst[hd * SSM_HEAD_DIM:(hd + 1) * SSM_HEAD_DIM, :] * self.dtt[c:c + 1, :]
                y_parts.append(_dot(xdt.astype(BF16), mt))
                xw_parts.append((xdt * jnp.exp(self.tot[c:c + 1, :] - row)).astype(BF16))
                self.e_row.append(jnp.exp(row))
                self.e_tot.append(jnp.exp(self.tot[c:c + 1, :gn]))
            self.ctz.append(ctz)
            self.y_diag.append(y_parts)
            self.xw.append(jnp.concatenate(xw_parts, axis=0))

    def carry(self, h_ref):
        gp = SSM_HEADS_PER_GROUP * SSM_HEAD_DIM
        ys = []
        for grp in range(SSM_GROUPS):
            hg = h_ref[grp * gp:(grp + 1) * gp, :]
            y_off = _dot(hg.astype(BF16), self.ctz[grp])
            st = _dot(self.xw[grp], self.bn)
            h_parts = []
            for e in range(SSM_HEADS_PER_GROUP):
                hd = grp * SSM_HEADS_PER_GROUP + e
                rows = slice(e * SSM_HEAD_DIM, (e + 1) * SSM_HEAD_DIM)
                ys.append(self.y_diag[grp][e] + y_off[rows] * self.e_row[hd])
                h_parts.append(hg[rows] * self.e_tot[hd])
            h_ref[grp * gp:(grp + 1) * gp, :] = jnp.concatenate(h_parts, axis=0) + st
        return jnp.concatenate(ys, axis=0)


def _ssd_run(chunks, h_refs):
    for ch in chunks:
        ch.decay()
    for ch in chunks:
        ch.local()
    return [ch.carry(h_refs[ch.direction]) for ch in chunks]


def _ssd_kernel(xf_ref, xb_ref, bf_ref, bb_ref, cf_ref, cb_ref, df_ref, db_ref, xc_ref, bc_ref, cc_ref, dc_ref,
                alog_ref, dsk_ref, yf_ref, yb_ref, hf_ref, hb_ref):
    c = pl.program_id(1)
    a_col = -jnp.exp(alog_ref[...])
    ln = SSM_CHUNK
    h_refs = (hf_ref, hb_ref)

    def chunk(x_ref, b_ref, c_ref, d_ref, s, direction):
        tok = slice(s * ln, (s + 1) * ln)
        return _SsdChunk(x_ref[0, :, tok], b_ref[0, tok, :], c_ref[0, :, tok], d_ref[0, :, tok], a_col, direction)

    @pl.when(c == 0)
    def _():
        n_ctx = xc_ref.shape[2] // ln
        for direction in range(2):
            h_refs[direction][...] = jnp.zeros_like(h_refs[direction])
        chunks = []
        for k in range(n_ctx):
            chunks.append(chunk(xc_ref, bc_ref, cc_ref, dc_ref, k, 0))
            chunks.append(chunk(xc_ref, bc_ref, cc_ref, dc_ref, n_ctx - 1 - k, 1))
        _ssd_run(chunks, h_refs)

    n_sub = xf_ref.shape[2] // ln
    chunks = []
    for k in range(n_sub):
        chunks.append(chunk(xf_ref, bf_ref, cf_ref, df_ref, k, 0))
        chunks.append(chunk(xb_ref, bb_ref, cb_ref, db_ref, n_sub - 1 - k, 1))
    ys = _ssd_run(chunks, h_refs)
    for k in range(n_sub):
        tok = slice(k * ln, (k + 1) * ln)
        yf_ref[0, :, tok] = ys[2 * k] + xf_ref[0, :, tok] * dsk_ref[...]
        tok = slice((n_sub - 1 - k) * ln, (n_sub - k) * ln)
        yb_ref[0, :, tok] = ys[2 * k + 1]


def _ssd(lat, ctx, alog_col, dsk_col, toks):
    xst, bn, ct, dtt = lat
    b, di, t = xst.shape
    gn = bn.shape[2]
    nh2 = dtt.shape[1]
    nctx = ctx[0].shape[2]
    ns = t // toks
    fcol = lambda bi, c: (bi, 0, c)
    bcol = lambda bi, c: (bi, 0, ns - 1 - c)
    frow = lambda bi, c: (bi, c, 0)
    brow = lambda bi, c: (bi, ns - 1 - c, 0)
    whole = lambda bi, c: (bi, 0, 0)
    return pl.pallas_call(
        _ssd_kernel,
        grid=(b, ns),
        in_specs=[
            pl.BlockSpec((1, di, toks), fcol), pl.BlockSpec((1, di, toks), bcol),
            pl.BlockSpec((1, toks, gn), frow), pl.BlockSpec((1, toks, gn), brow),
            pl.BlockSpec((1, gn, toks), fcol), pl.BlockSpec((1, gn, toks), bcol),
            pl.BlockSpec((1, nh2, toks), fcol), pl.BlockSpec((1, nh2, toks), bcol),
            pl.BlockSpec((1, di, nctx), whole), pl.BlockSpec((1, nctx, gn), whole),
            pl.BlockSpec((1, gn, nctx), whole), pl.BlockSpec((1, nh2, nctx), whole),
            _const_spec((nh2, SSM_CHUNK)),
            _const_spec((di, SSM_CHUNK)),
        ],
        out_specs=[pl.BlockSpec((1, di, toks), fcol), pl.BlockSpec((1, di, toks), bcol)],
        out_shape=[jax.ShapeDtypeStruct((b, di, t), F32), jax.ShapeDtypeStruct((b, di, t), F32)],
        scratch_shapes=[pltpu.VMEM((di, gn), F32)] * 2,
        compiler_params=pltpu.CompilerParams(dimension_semantics=("parallel", "arbitrary"),
                                             vmem_limit_bytes=VMEM_LIMIT),
        name="ssd",
    )(xst, xst, bn, bn, ct, ct, dtt, dtt, *ctx, alog_col, dsk_col)


def _row_groups(tm):
    n = ROW_GROUPS if tm % (ROW_GROUPS * SUBLANES * 2) == 0 else 1
    return [slice(i * (tm // n), (i + 1) * (tm // n)) for i in range(n)]


def _ffn_tail(x1s, m, nw, wg_ref, wu_ref, wd_ref):
    hs = [((_rms(x1) * nw) * (1.0 + m[4:5]) + m[3:4]).astype(BF16) for x1 in x1s]
    ffn = wg_ref.shape[1]
    fs = [None] * len(x1s)
    for c0 in range(0, ffn, FFN_CHUNK):
        c1 = min(c0 + FFN_CHUNK, ffn)
        hids = [(_silu(_dot(h, wg_ref[:, c0:c1])) * _dot(h, wu_ref[:, c0:c1])).astype(BF16) for h in hs]
        for i, hid in enumerate(hids):
            part = _dot(hid, wd_ref[c0:c1, :])
            fs[i] = part if fs[i] is None else fs[i] + part
    return [x1 + m[5:6] * f for x1, f in zip(x1s, fs)]


def _mix0_kernel(x_ref, a_ref, yf_ref, yb_ref, z_ref, mod_ref, snw_ref, nw_ref,
                 wa_ref, wy_ref, wg_ref, wu_ref, wd_ref, o_ref):
    m = mod_ref[0]
    gw = SSM_D_INNER // SSM_GROUPS
    groups = _row_groups(x_ref.shape[1])
    yns = []
    for rows in groups:
        y = (yf_ref[0, :, rows] + yb_ref[0, :, rows]).T
        gy = y * _silu(z_ref[0, rows, :])
        gn = jnp.concatenate([_rms(gy[:, i * gw:(i + 1) * gw]) for i in range(SSM_GROUPS)], axis=1)
        yns.append((gn * snw_ref[...]).astype(BF16))
    mix_a = [_dot(a_ref[0, rows, :], wa_ref[...]) for rows in groups]
    x1s = []
    for rows, yn, ma in zip(groups, yns, mix_a):
        x1s.append(x_ref[0, rows, :] + m[2:3] * (ma + _dot(yn, wy_ref[...])))
    for rows, out in zip(groups, _ffn_tail(x1s, m, nw_ref[...], wg_ref, wu_ref, wd_ref)):
        o_ref[0, rows, :] = out


def _mix0(x, a, yf, yb, z, mod, snw, nw, wa, wy, wg, wu, wd, tm):
    b, t, d = x.shape
    ffn = wg.shape[1]
    row = lambda bi, i: (bi, i, 0)
    col = lambda bi, i: (bi, 0, i)
    return pl.pallas_call(
        _mix0_kernel,
        grid=(b, t // tm),
        in_specs=[
            pl.BlockSpec((1, tm, d), row),
            pl.BlockSpec((1, tm, Q_DIM), row),
            pl.BlockSpec((1, SSM_D_INNER, tm), col),
            pl.BlockSpec((1, SSM_D_INNER, tm), col),
            pl.BlockSpec((1, tm, SSM_D_INNER), row),
            pl.BlockSpec((1, N_MOD, d), lambda bi, i: (bi, 0, 0)),
            _const_spec((1, SSM_D_INNER)),
            _const_spec((1, d)),
            _const_spec((Q_DIM, d)),
            _const_spec((SSM_D_INNER, d)),
            _const_spec((d, ffn)),
            _const_spec((d, ffn)),
            _const_spec((ffn, d)),
        ],
        out_specs=pl.BlockSpec((1, tm, d), row),
        out_shape=jax.ShapeDtypeStruct((b, t, d), F32),
        compiler_params=_cparams(2),
        name="mix0",
    )(x, a, yf, yb, z, mod, snw, nw, wa, wy, wg, wu, wd)


def _layer1_kernel(x_ref, xp_ref, xn_ref, mod_ref, nmix_ref, win_ref, cw_ref, wout_ref, nffn_ref,
                   wg_ref, wu_ref, wd_ref, fn_ref, o_ref, v_ref):
    i = pl.program_id(1)
    last = pl.num_programs(1) - 1
    m = mod_ref[0]
    tm, d = x_ref.shape[1], x_ref.shape[2]
    x = x_ref[0]
    xa = jnp.concatenate([xp_ref[0], x, xn_ref[0]], axis=0)
    h = _rms(xa) * nmix_ref[...]
    h = h * (1.0 + m[1:2]) + m[0:1]
    pcu = _dot(h.astype(BF16), win_ref[:, d:])
    pgb = _dot(h[SUBLANES:SUBLANES + tm].astype(BF16), win_ref[:, :d])
    v = pcu[:, :d] * pcu[:, d:]
    rid = lax.broadcasted_iota(jnp.int32, (tm + 2 * SUBLANES, 1), 0)
    keep = jnp.logical_and(jnp.logical_or(rid >= SUBLANES, i > 0),
                           jnp.logical_or(rid < SUBLANES + tm, i < last))
    v_ref[...] = jnp.where(keep, v, 0.0)
    cw = cw_ref[...]
    groups = _row_groups(tm)
    x1s = []
    for rows in groups:
        r0 = rows.start + SUBLANES
        n = rows.stop - rows.start
        conv = (v_ref[r0 - 1:r0 - 1 + n] * cw[0:1] + v_ref[r0:r0 + n] * cw[1:2] + v_ref[r0 + 1:r0 + 1 + n] * cw[2:3])
        gated = (pgb[rows] * conv).astype(BF16)
        x1s.append(x[rows] + m[2:3] * _dot(gated, wout_ref[...]))
    for rows, x2 in zip(groups, _ffn_tail(x1s, m, nffn_ref[...], wg_ref, wu_ref, wd_ref)):
        o_ref[0, rows, :] = _rms(x2) * fn_ref[...]


def _layer1(x, mod, nmix, win, cw, wout, nffn, wg, wu, wd, fn, tm):
    b, t, d = x.shape
    ffn = wg.shape[1]
    r8 = tm // SUBLANES
    nblk8 = t // SUBLANES
    return pl.pallas_call(
        _layer1_kernel,
        grid=(b, t // tm),
        in_specs=[
            pl.BlockSpec((1, tm, d), lambda bi, i: (bi, i, 0)),
            pl.BlockSpec((1, SUBLANES, d), lambda bi, i: (bi, jnp.maximum(i * r8 - 1, 0), 0)),
            pl.BlockSpec((1, SUBLANES, d), lambda bi, i: (bi, jnp.minimum((i + 1) * r8, nblk8 - 1), 0)),
            pl.BlockSpec((1, N_MOD, d), lambda bi, i: (bi, 0, 0)),
            _const_spec((1, d)),
            _const_spec((d, 3 * d)),
            _const_spec((3, d)),
            _const_spec((d, d)),
            _const_spec((1, d)),
            _const_spec((d, ffn)),
            _const_spec((d, ffn)),
            _const_spec((ffn, d)),
            _const_spec((1, d)),
        ],
        out_specs=pl.BlockSpec((1, tm, d), lambda bi, i: (bi, i, 0)),
        out_shape=jax.ShapeDtypeStruct((b, t, d), F32),
        scratch_shapes=[pltpu.VMEM((tm + 2 * SUBLANES, d), F32)],
        compiler_params=_cparams(2),
        name="layer1",
    )(x, x, x, mod, nmix, win, cw, wout, nffn, wg, wu, wd, fn)


def _rope_tables(n_tokens):
    rows = n_tokens // GRID_W
    row = jnp.repeat(jnp.arange(rows), GRID_W).astype(F32)
    col = jnp.tile(jnp.arange(GRID_W), rows).astype(F32)
    inv = 1.0 / (ROPE_THETA ** (jnp.arange(0, AXIS_DIM, 2, dtype=F32) / AXIS_DIM))
    ang = jnp.concatenate([row[:, None] * inv, col[:, None] * inv], axis=-1)
    return jnp.cos(ang), jnp.sin(ang)


_HEAD_PERM = np.concatenate([np.arange(0, HEAD_DIM, 2), np.arange(1, HEAD_DIM, 2)])


def _qk_tables(cos, sin, q_gain, k_gain):
    half = HEAD_DIM // 2
    sign = jnp.concatenate([-jnp.ones((half,), F32), jnp.ones((half,), F32)])
    c64 = jnp.concatenate([cos, cos], axis=1)
    s64 = jnp.concatenate([sin, sin], axis=1) * sign
    swap = np.concatenate([np.arange(half, HEAD_DIM), np.arange(0, half)])

    def tables(gain, scale):
        gp = gain[_HEAD_PERM] * scale
        return c64 * gp[None, :], s64 * gp[swap][None, :]

    aq, bq = tables(q_gain, HEAD_DIM ** -0.5 * LOG2_E)
    ak, bk = tables(k_gain, 1.0)
    return aq.T, bq.T, jnp.tile(ak, (1, ATTN_KV_HEADS)), jnp.tile(bk, (1, ATTN_KV_HEADS))


def kernel(x, c, ctx, c_ctx, ada_w, ada_b, norm_mix, norm_ffn, ffn_w_gate, ffn_w_up, ffn_w_down, hy_w_in,
           hy_q_norm, hy_k_norm, hy_conv_w, hy_conv_b, hy_dt_bias, hy_a_log, hy_d_skip, hy_ssm_norm, hy_w_out,
           sc_w_in, sc_conv_w, sc_w_out, final_norm):
    b, t, d = x.shape
    nctx = ctx.shape[1]
    tm_in = min(512, t)
    tm_mix = min(512, t)

    n_rows = -(-(b + 1) // SUBLANES) * SUBLANES
    cc = jnp.zeros((n_rows, d), F32).at[:b].set(c).at[b].set(c_ctx)
    mods = _modulation(cc, ada_w, ada_b).reshape(ada_w.shape[0], n_rows, N_MOD, d)
    mod0, mod0_ctx, mod1 = mods[0, :b], mods[0, b:b + 1], mods[1, :b]

    w_in = hy_w_in[0]
    idx_k, idx_v, idx_z = Q_DIM, Q_DIM + KV_DIM, Q_DIM + 2 * KV_DIM
    idx_xbc = idx_z + SSM_D_INNER
    idx_dt = idx_xbc + SSM_CONV_DIM
    q_cols = (np.arange(ATTN_HEADS)[:, None] * HEAD_DIM + _HEAD_PERM[None, :]).reshape(-1)
    k_cols = idx_k + (np.arange(ATTN_KV_HEADS)[:, None] * HEAD_DIM + _HEAD_PERM[None, :]).reshape(-1)
    dt_pad = jnp.zeros((d, LANES - 2 * SSM_HEADS), F32)
    wkz = jnp.concatenate([w_in[:, k_cols], w_in[:, idx_z:idx_xbc]], axis=1).astype(BF16)
    wx = jnp.concatenate([w_in[:, idx_xbc:], dt_pad], axis=1).astype(BF16)
    wt = jnp.concatenate([w_in[:, q_cols], w_in[:, idx_v:idx_z]], axis=1).T.astype(BF16)
    nw0 = norm_mix[0][None, :]

    cos, sin = _rope_tables(t)
    aq, bq, ck, sk = _qk_tables(cos, sin, hy_q_norm[0], hy_k_norm[0])
    ones_c, zeros_c = jnp.ones((nctx, AXIS_DIM), F32), jnp.zeros((nctx, AXIS_DIM), F32)
    aq_c, bq_c, ck_c, sk_c = _qk_tables(ones_c, zeros_c, hy_q_norm[0], hy_k_norm[0])
    seg = np.arange(KV_DIM) // HEAD_DIM
    bd = jnp.asarray(seg[:, None] == seg[None, :], BF16)

    conv_w = jnp.concatenate([hy_conv_w[0], jnp.zeros((3, LANES), F32)], axis=1)
    conv_b = jnp.concatenate([hy_conv_b[0], jnp.zeros((LANES,), F32)])[None, :]
    dtb = jnp.concatenate([hy_dt_bias[0].reshape(-1), jnp.zeros((LANES - 2 * SSM_HEADS,), F32)])[None, :]
    shared = (nw0, wx, wkz, wt)
    qt, vt, k, z, *ssd_lat = _inproj(x, mod0, True, *shared, aq, bq, ck, sk, bd, conv_w, conv_b, dtb, tm_in)
    _, vct, kc, _, *ssd_ctx = _inproj(ctx, mod0_ctx, False, *shared, aq_c, bq_c, ck_c, sk_c, bd, conv_w, conv_b, dtb,
                                      nctx)

    k_all = jnp.concatenate([k, kc], axis=1)
    vt_all = jnp.concatenate([vt, vct], axis=2)
    score_bound = 1.02 * HEAD_DIM ** 0.5 * LOG2_E * jnp.max(jnp.abs(hy_q_norm[0])) * jnp.max(jnp.abs(hy_k_norm[0]))
    tq = min(128, t)
    a_lat = lax.cond(score_bound <= SAFE_SCORE_BOUND,
                     functools.partial(_attention_bounded, tq=tq), functools.partial(_attention_exact, tq=tq),
                     qt, k_all, vt_all)

    alog_col = jnp.broadcast_to(hy_a_log[0].reshape(-1, 1), (2 * SSM_HEADS, SSM_CHUNK))
    dsk_col = jnp.broadcast_to(jnp.repeat(hy_d_skip[0], SSM_HEAD_DIM)[:, None], (SSM_D_INNER, SSM_CHUNK))
    yf, yb = _ssd(ssd_lat, ssd_ctx, alog_col, dsk_col, toks=min(2 * SSM_CHUNK, t))

    w_out = hy_w_out[0].astype(BF16)
    x = _mix0(x, a_lat, yf, yb, z, mod0, hy_ssm_norm[0][None, :], norm_ffn[0][None, :],
              w_out[:Q_DIM], w_out[Q_DIM:], ffn_w_gate[0].astype(BF16), ffn_w_up[0].astype(BF16),
              ffn_w_down[0].astype(BF16), tm_mix)

    return _layer1(x, mod1, norm_mix[1][None, :], sc_w_in[0].astype(BF16), sc_conv_w[0], sc_w_out[0].astype(BF16),
                   norm_ffn[1][None, :], ffn_w_gate[1].astype(BF16), ffn_w_up[1].astype(BF16),
                   ffn_w_down[1].astype(BF16), final_norm[None, :], tm_mix)
```

```python
import functools

import jax
import jax.numpy as jnp
import numpy as np
from jax import lax
from jax.experimental import pallas as pl
from jax.experimental.pallas import tpu as pltpu

F32 = jnp.float32
BF16 = jnp.bfloat16

EPS = 1e-6
N_MOD = 6
GRID_W = 64
ROPE_THETA = 10000.0

ATTN_HEADS = 8
ATTN_KV_HEADS = 2
HEAD_DIM = 64
AXIS_DIM = HEAD_DIM // 2
KV_GROUP = ATTN_HEADS // ATTN_KV_HEADS
Q_DIM = ATTN_HEADS * HEAD_DIM
KV_DIM = ATTN_KV_HEADS * HEAD_DIM

SSM_HEADS = 8
SSM_HEAD_DIM = 64
SSM_D_INNER = SSM_HEADS * SSM_HEAD_DIM
SSM_GROUPS = 2
SSM_STATE = 64
SSM_CHUNK = 128
SSM_CONV_DIM = SSM_D_INNER + 2 * SSM_GROUPS * SSM_STATE
SSM_HEADS_PER_GROUP = SSM_HEADS // SSM_GROUPS

LANES = 128
SUBLANES = 8
XBCDT_W = SSM_CONV_DIM + LANES
NEG_BIG = -1e30
LOG2_E = 1.4426950408889634
ATTN_KEY_TILES = (768, 512, 256, 128)
ROW_GROUPS = 2
FFN_CHUNK = 768
ATTN_Q_BLOCKS = 8
ATTN_SUB_TILE = 256
ATTN_PV_LAG = 3
SAFE_SCORE_BOUND = 40.0
SCORE_BOUND_SLACK = 1.001

VMEM_LIMIT = 56 * 1024 * 1024


def _cparams(n_axes):
    return pltpu.CompilerParams(dimension_semantics=("parallel",) * n_axes, vmem_limit_bytes=VMEM_LIMIT)


def _const_spec(shape):
    nd = len(shape)
    return pl.BlockSpec(shape, lambda *_: (0,) * nd, pipeline_mode=pl.Buffered(1))


def _rms(x):
    return x * lax.rsqrt(jnp.mean(x * x, axis=-1, keepdims=True) + EPS)


def _silu(x):
    return x * (1.0 / (1.0 + jnp.exp(-x)))


def _dot(a, b):
    return jnp.dot(a, b, preferred_element_type=F32)


def _dot_exact(a, b):
    return jnp.dot(a, b, preferred_element_type=F32, precision=lax.Precision.HIGHEST)


def _mod_kernel(c_ref, w_ref, b_ref, o_ref):
    c = c_ref[...]
    o_ref[0] = _dot_exact(_silu(c), w_ref[0]) + b_ref[0]


def _modulation(cc, ada_w, ada_b):
    depth, d, nd = ada_w.shape
    r = cc.shape[0]
    tn = d
    return pl.pallas_call(
        _mod_kernel,
        grid=(depth, nd // tn),
        in_specs=[
            pl.BlockSpec((r, d), lambda l, j: (0, 0)),
            pl.BlockSpec((1, d, tn), lambda l, j: (l, 0, j)),
            pl.BlockSpec((1, 1, tn), lambda l, j: (l, 0, j)),
        ],
        out_specs=pl.BlockSpec((1, r, tn), lambda l, j: (l, 0, j)),
        out_shape=jax.ShapeDtypeStruct((depth, r, nd), F32),
        compiler_params=_cparams(2),
        name="mod",
    )(cc, ada_w, ada_b.reshape(depth, 1, nd))


def _inproj_kernel(x_ref, xp_ref, xn_ref, mod_ref, nw_ref, wx_ref, wkz_ref, wt_ref, aq_ref, bq_ref, ck_ref, sk_ref,
                   bd_ref, cw_ref, cb_ref, dtb_ref,
                   qt_ref, vt_ref, k_ref, z_ref, xst_ref, bn_ref, ct_ref, dtt_ref, sc_ref):
    i = pl.program_id(1)
    last = pl.num_programs(1) - 1
    m = mod_ref[0]
    tm = x_ref.shape[1]
    half = HEAD_DIM // 2
    xa = jnp.concatenate([xp_ref[0], x_ref[0], xn_ref[0]], axis=0)
    h = _rms(xa) * nw_ref[...]
    h = h * (1.0 + m[1:2]) + m[0:1]
    px = _dot(h.astype(BF16), wx_ref[...])
    groups = [slice(0, tm)]
    proj = []
    for rows in groups:
        hc = h[SUBLANES + rows.start:SUBLANES + rows.stop].astype(BF16)
        pkz = _dot(hc, wkz_ref[...])
        pt = lax.dot_general(wt_ref[...], hc, (((1,), (1,)), ((), ())), preferred_element_type=F32)
        sq = pkz[:, :KV_DIM] * pkz[:, :KV_DIM]
        hi = sq.astype(BF16)
        lo = (sq - hi.astype(F32)).astype(BF16)
        proj.append((pkz, pt, _dot(hi, bd_ref[...]) + _dot(lo, bd_ref[...])))

    rid = lax.broadcasted_iota(jnp.int32, (tm + 2 * SUBLANES, 1), 0)
    keep = jnp.logical_and(jnp.logical_or(rid >= SUBLANES, i > 0), jnp.logical_or(rid < SUBLANES + tm, i < last))
    sc_ref[...] = jnp.where(keep, px, 0.0)
    cur = sc_ref[SUBLANES:SUBLANES + tm]
    w = cw_ref[...]
    y = (sc_ref[SUBLANES - 1:SUBLANES - 1 + tm] * w[0:1] + cur * w[1:2]
         + sc_ref[SUBLANES + 1:SUBLANES + 1 + tm] * w[2:3] + cb_ref[...])
    u = _silu(y[:, :SSM_CONV_DIM])
    bc = SSM_D_INNER + SSM_GROUPS * SSM_STATE
    xst_ref[0] = u[:, :SSM_D_INNER].T
    bn_ref[0] = u[:, SSM_D_INNER:bc].astype(BF16)
    ct_ref[0] = u[:, bc:].T.astype(BF16)
    d = cur[:, SSM_CONV_DIM:] + dtb_ref[...]
    dt = jnp.maximum(d, 0.0) + jnp.log1p(jnp.exp(-jnp.abs(d)))
    dtt_ref[0] = dt.T[:2 * SSM_HEADS]

    for rows, (pkz, pt, ss) in zip(groups, proj):
        n = rows.stop - rows.start
        kx = pkz[:, :KV_DIM]
        rs = lax.rsqrt(ss * (1.0 / HEAD_DIM) + EPS)
        lane = lax.broadcasted_iota(jnp.int32, kx.shape, 1)
        swap = jnp.where((lane & half) == 0, pltpu.roll(kx, KV_DIM - half, 1), pltpu.roll(kx, half, 1))
        k_ref[0, rows, :] = (rs * (kx * ck_ref[rows, :] + swap * sk_ref[rows, :])).astype(BF16)

        q3 = pt[:Q_DIM].reshape(ATTN_HEADS, HEAD_DIM, n)
        qrs = lax.rsqrt(jnp.mean(q3 * q3, axis=1, keepdims=True) + EPS)
        qsw = jnp.concatenate([q3[:, half:], q3[:, :half]], axis=1)
        qo = qrs * (q3 * aq_ref[:, rows][None] + qsw * bq_ref[:, rows][None])
        qt_ref[0, :, rows] = qo.reshape(Q_DIM, n).astype(BF16)

        vt_ref[0, :, rows] = pt[Q_DIM:].astype(BF16)
        z_ref[0, rows, :] = pkz[:, KV_DIM:]


def _inproj(x, mod, per_batch_mod, nw, wx, wkz, wt, aq, bq, ck, sk, bd, cw, cb, dtb, tm):
    b, t, d = x.shape
    r8 = tm // SUBLANES
    nblk8 = t // SUBLANES
    gn = SSM_GROUPS * SSM_STATE
    mod_idx = (lambda bi, i: (bi, 0, 0)) if per_batch_mod else (lambda bi, i: (0, 0, 0))
    row = lambda bi, i: (bi, i, 0)
    col = lambda bi, i: (bi, 0, i)
    return pl.pallas_call(
        _inproj_kernel,
        grid=(b, t // tm),
        in_specs=[
            pl.BlockSpec((1, tm, d), row),
            pl.BlockSpec((1, SUBLANES, d), lambda bi, i: (bi, jnp.maximum(i * r8 - 1, 0), 0)),
            pl.BlockSpec((1, SUBLANES, d), lambda bi, i: (bi, jnp.minimum((i + 1) * r8, nblk8 - 1), 0)),
            pl.BlockSpec((1, N_MOD, d), mod_idx),
            _const_spec((1, d)),
            _const_spec((d, XBCDT_W)),
            _const_spec((d, KV_DIM + SSM_D_INNER)),
            _const_spec((Q_DIM + KV_DIM, d)),
            pl.BlockSpec((HEAD_DIM, tm), lambda bi, i: (0, i)),
            pl.BlockSpec((HEAD_DIM, tm), lambda bi, i: (0, i)),
            pl.BlockSpec((tm, KV_DIM), lambda bi, i: (i, 0)),
            pl.BlockSpec((tm, KV_DIM), lambda bi, i: (i, 0)),
            _const_spec((KV_DIM, KV_DIM)),
            _const_spec((3, XBCDT_W)),
            _const_spec((1, XBCDT_W)),
            _const_spec((1, LANES)),
        ],
        out_specs=[
            pl.BlockSpec((1, Q_DIM, tm), col),
            pl.BlockSpec((1, KV_DIM, tm), col),
            pl.BlockSpec((1, tm, KV_DIM), row),
            pl.BlockSpec((1, tm, SSM_D_INNER), row),
            pl.BlockSpec((1, SSM_D_INNER, tm), col),
            pl.BlockSpec((1, tm, gn), row),
            pl.BlockSpec((1, gn, tm), col),
            pl.BlockSpec((1, 2 * SSM_HEADS, tm), col),
        ],
        out_shape=[
            jax.ShapeDtypeStruct((b, Q_DIM, t), BF16),
            jax.ShapeDtypeStruct((b, KV_DIM, t), BF16),
            jax.ShapeDtypeStruct((b, t, KV_DIM), BF16),
            jax.ShapeDtypeStruct((b, t, SSM_D_INNER), F32),
            jax.ShapeDtypeStruct((b, SSM_D_INNER, t), F32),
            jax.ShapeDtypeStruct((b, t, gn), BF16),
            jax.ShapeDtypeStruct((b, gn, t), BF16),
            jax.ShapeDtypeStruct((b, 2 * SSM_HEADS, t), F32),
        ],
        scratch_shapes=[pltpu.VMEM((tm + 2 * SUBLANES, XBCDT_W), F32)],
        compiler_params=_cparams(2),
        name="inproj",
    )(x, x, x, mod, nw, wx, wkz, wt, aq, bq, ck, sk, bd, cw, cb, dtb)


def _attn_kernel(qt_ref, k_ref, vt_ref, o_ref, s0_ref, s1_ref, p0_ref, p1_ref, m0_ref, m1_ref, *, nq):
    g = pl.program_id(1)
    i = pl.program_id(2)
    tq = qt_ref.shape[2]
    cols = KV_GROUP * tq
    nk, kt = s0_ref.shape[0], s0_ref.shape[1]
    s_refs, m_refs, p_refs = (s0_ref, s1_ref), (m0_ref, m1_ref), (p0_ref, p1_ref)

    def run(a_buf, b_buf):
        do_a, do_bc = a_buf is not None, b_buf is not None
        if do_a:
            q4 = qt_ref[0]
            qcat = jnp.concatenate([q4[h * HEAD_DIM:(h + 1) * HEAD_DIM] for h in range(KV_GROUP)], axis=1)
            qp = jnp.concatenate([qcat] * ATTN_KV_HEADS, axis=0)
            rowgrp = lax.broadcasted_iota(jnp.int32, qp.shape, 0) // HEAD_DIM
            qp = jnp.where(rowgrp == g, qp, jnp.zeros_like(qp))
        if do_bc:
            m8 = jnp.broadcast_to(jnp.max(m_refs[b_buf][...], axis=0, keepdims=True), (SUBLANES, cols))

        def stage_a(j, mx):
            ktile = k_ref[0, pl.ds(pl.multiple_of(j * kt, kt), kt), :]
            s = _dot(ktile, qp)
            s_refs[a_buf][j] = s
            return jnp.maximum(mx, jnp.max(s.reshape(kt // SUBLANES, SUBLANES, cols), axis=0))

        def stage_b(j, par, ls):
            s = s_refs[b_buf][j].reshape(kt // SUBLANES, SUBLANES, cols)
            p = jnp.exp2(s - m8[None])
            p_refs[par][...] = p.reshape(kt, cols).astype(BF16)
            return ls + jnp.sum(p, axis=0)

        def stage_c(j, par, acc):
            return acc + _dot(vt_ref[0, 0, j], p_refs[par][...])

        def tile(j, par, carry):
            mx, ls, acc = carry
            if do_a:
                mx = stage_a(j, mx)
            if do_bc:
                acc = stage_c(j - 1, 1 - par, acc)
                ls = stage_b(j, par, ls)
            return mx, ls, acc

        mx = jnp.full((SUBLANES, cols), NEG_BIG, F32)
        ls = jnp.zeros((SUBLANES, cols), F32)
        acc = jnp.zeros((HEAD_DIM, cols), F32)
        if do_a:
            mx = stage_a(0, mx)
        if do_bc:
            ls = stage_b(0, 0, ls)

        def body(t, carry):
            j = 1 + 2 * t
            return tile(j + 1, 0, tile(j, 1, carry))

        carry = lax.fori_loop(0, (nk - 1) // 2, body, (mx, ls, acc), unroll=True)
        if (nk - 1) % 2:
            carry = tile(nk - 1, (nk - 1) % 2, carry)
        mx, ls, acc = carry
        if do_a:
            m_refs[a_buf][...] = mx
        if do_bc:
            acc = stage_c(nk - 1, (nk - 1) % 2, acc)
            o = acc * (1.0 / jnp.sum(ls, axis=0, keepdims=True))
            o4 = jnp.concatenate([o[:, h * tq:(h + 1) * tq] for h in range(KV_GROUP)], axis=0)
            o_ref[0] = o4.T.astype(BF16)

    @pl.when(i == 0)
    def _():
        run(0, None)

    inner = jnp.logical_and(i > 0, i < nq)

    @pl.when(jnp.logical_and(inner, i % 2 == 1))
    def _():
        run(1, 0)

    @pl.when(jnp.logical_and(inner, i % 2 == 0))
    def _():
        run(0, 1)

    @pl.when(i == nq)
    def _():
        run(None, (nq - 1) % 2)


def _attention(qt, k_all, vt_tiles, tq):
    b, _, t = qt.shape
    tk = k_all.shape[1]
    nk, kt = vt_tiles.shape[2], vt_tiles.shape[4]
    nq = t // tq
    gw = KV_GROUP * HEAD_DIM
    cols = KV_GROUP * tq
    return pl.pallas_call(
        functools.partial(_attn_kernel, nq=nq),
        grid=(b, ATTN_KV_HEADS, nq + 1),
        in_specs=[
            pl.BlockSpec((1, gw, tq), lambda bi, g, i: (bi, g, jnp.minimum(i, nq - 1))),
            pl.BlockSpec((1, tk, KV_DIM), lambda bi, g, i: (bi, 0, 0)),
            pl.BlockSpec((1, 1, nk, HEAD_DIM, kt), lambda bi, g, i: (bi, g, 0, 0, 0)),
        ],
        out_specs=pl.BlockSpec((1, tq, gw), lambda bi, g, i: (bi, jnp.maximum(i - 1, 0), g)),
        out_shape=jax.ShapeDtypeStruct((b, t, Q_DIM), BF16),
        scratch_shapes=(
            [pltpu.VMEM((nk, kt, cols), F32)] * 2
            + [pltpu.VMEM((kt, cols), BF16)] * 2
            + [pltpu.VMEM((SUBLANES, cols), F32)] * 2
        ),
        compiler_params=pltpu.CompilerParams(dimension_semantics=("parallel", "parallel", "arbitrary"),
                                             vmem_limit_bytes=VMEM_LIMIT),
        name="attn",
    )(qt, k_all, vt_tiles)


def _attn_bounded_kernel(qt_ref, k_ref, vt_ref, km_ref, o_ref, *p_refs):
    g = pl.program_id(1)
    cols = p_refs[0].shape[1]
    tq = cols // KV_GROUP
    n_blk = qt_ref.shape[2] // tq
    sub = p_refs[0].shape[0]
    n_all = k_ref.shape[1] // sub

    kmax = jnp.concatenate([km_ref[0:1, :]] * (cols // LANES), axis=1)
    qps, mb8s = [], []
    for blk in range(n_blk):
        q4 = qt_ref[0, :, blk * tq:(blk + 1) * tq]
        qf = q4.astype(F32)
        qn = jnp.sqrt(jnp.sum((qf * qf).reshape(KV_GROUP, HEAD_DIM, tq), axis=1))
        qn = jnp.concatenate([qn[h:h + 1] for h in range(KV_GROUP)], axis=1)
        mb8s.append(jnp.broadcast_to(qn * kmax * SCORE_BOUND_SLACK, (SUBLANES, cols)))
        qcat = jnp.concatenate([q4[h * HEAD_DIM:(h + 1) * HEAD_DIM] for h in range(KV_GROUP)], axis=1)
        qp = jnp.concatenate([qcat] * ATTN_KV_HEADS, axis=0)
        rowgrp = lax.broadcasted_iota(jnp.int32, qp.shape, 0) // HEAD_DIM
        qps.append(jnp.where(rowgrp == g, qp, jnp.zeros_like(qp)))

    ring = len(p_refs)
    lag = ring - 1
    ls = [jnp.zeros((SUBLANES, cols), F32) for _ in range(n_blk)]
    acc = [jnp.zeros((HEAD_DIM, cols), F32) for _ in range(n_blk)]

    def pv(w):
        blk, u = divmod(w, n_all)
        acc[blk] = acc[blk] + _dot(vt_ref[0, :, u * sub:(u + 1) * sub], p_refs[w % ring][...])
        if u == n_all - 1:
            o = acc[blk] * (1.0 / jnp.sum(ls[blk], axis=0, keepdims=True))
            o4 = jnp.concatenate([o[:, h * tq:(h + 1) * tq] for h in range(KV_GROUP)], axis=0)
            o_ref[0, blk * tq:(blk + 1) * tq, :] = o4.T.astype(BF16)

    n_items = n_blk * n_all
    for w in range(n_items):
        blk, u = divmod(w, n_all)
        s = _dot(k_ref[0, u * sub:(u + 1) * sub, :], qps[blk])
        if w >= lag:
            pv(w - lag)
        p = jnp.exp2(s.reshape(sub // SUBLANES, SUBLANES, cols) - mb8s[blk][None])
        p_refs[w % ring][...] = p.reshape(sub, cols).astype(BF16)
        ls[blk] = ls[blk] + jnp.sum(p, axis=0)
    for w in range(max(n_items - lag, 0), n_items):
        pv(w)


def _attention_bounded(qt, k_all, vt_all, kmax, tq):
    b, _, t = qt.shape
    tk = k_all.shape[1]
    sub = next(c for c in (ATTN_SUB_TILE, LANES) if tk % c == 0)
    gw = KV_GROUP * HEAD_DIM
    cols = KV_GROUP * tq
    tstep = tq * (ATTN_Q_BLOCKS if t % (tq * ATTN_Q_BLOCKS) == 0 else 1)
    return pl.pallas_call(
        _attn_bounded_kernel,
        grid=(b, ATTN_KV_HEADS, t // tstep),
        in_specs=[
            pl.BlockSpec((1, gw, tstep), lambda bi, g, i: (bi, g, i)),
            pl.BlockSpec((1, tk, KV_DIM), lambda bi, g, i: (bi, 0, 0)),
            pl.BlockSpec((1, HEAD_DIM, tk), lambda bi, g, i: (bi, g, 0)),
            _const_spec((SUBLANES, LANES)),
        ],
        out_specs=pl.BlockSpec((1, tstep, gw), lambda bi, g, i: (bi, i, g)),
        out_shape=jax.ShapeDtypeStruct((b, t, Q_DIM), BF16),
        scratch_shapes=[pltpu.VMEM((sub, cols), BF16)] * (ATTN_PV_LAG + 1),
        compiler_params=_cparams(3),
        name="attn_bounded",
    )(qt, k_all, vt_all, jnp.full((SUBLANES, LANES), kmax, F32))


def _attention_exact(qt, k_all, vt_all, kmax, tq):
    del kmax
    b, _, tk = vt_all.shape
    kt = next(c for c in ATTN_KEY_TILES if tk % c == 0)
    vt_tiles = vt_all.reshape(b, ATTN_KV_HEADS, HEAD_DIM, tk // kt, kt).transpose(0, 1, 3, 2, 4)
    return _attention(qt, k_all, vt_tiles, tq)


class _SsdChunk:
    def __init__(self, xst, bn, ct, dtt, a_col, direction):
        self.xst, self.bn, self.ct, self.dtt, self.a_col, self.direction = xst, bn, ct, dtt, a_col, direction

    def decay(self):
        ln = self.xst.shape[1]
        da = self.dtt * self.a_col
        ri = lax.broadcasted_iota(jnp.int32, (ln, ln), 0)
        ci = lax.broadcasted_iota(jnp.int32, (ln, ln), 1)
        self.reach = (ci >= ri) if self.direction == 0 else (ci <= ri)
        self.cs_row = _dot_exact(da, self.reach.astype(F32))
        self.tot = _dot_exact(da, jnp.ones((ln, ln), F32))

    def local(self):
        ln = self.xst.shape[1]
        gn = SSM_GROUPS * SSM_STATE
        cs_col = jnp.concatenate([self.cs_row, jnp.zeros((ln - 2 * SSM_HEADS, ln), F32)], axis=0).T
        rowgrp = lax.broadcasted_iota(jnp.int32, (gn, ln), 0) // SSM_STATE
        self.ctz, self.y_diag, self.xw, self.e_row, self.e_tot = [], [], [], [], []
        for grp in range(SSM_GROUPS):
            ctz = jnp.where(rowgrp == grp, self.ct, jnp.zeros_like(self.ct))
            cbt = _dot(self.bn, ctz)
            y_parts, xw_parts = [], []
            for e in range(SSM_HEADS_PER_GROUP):
                hd = grp * SSM_HEADS_PER_GROUP + e
                c = self.direction * SSM_HEADS + hd
                row = self.cs_row[c:c + 1, :]
                lmt = jnp.exp(jnp.where(self.reach, row - cs_col[:, c:c + 1], NEG_BIG))
                mt = (cbt * lmt).astype(BF16)
                xdt = self.xst[hd * SSM_HEAD_DIM:(hd + 1) * SSM_HEAD_DIM, :] * self.dtt[c:c + 1, :]
                y_parts.append(_dot(xdt.astype(BF16), mt))
                xw_parts.append((xdt * jnp.exp(self.tot[c:c + 1, :] - row)).astype(BF16))
                self.e_row.append(jnp.exp(row))
                self.e_tot.append(jnp.exp(self.tot[c:c + 1, :gn]))
            self.ctz.append(ctz)
            self.y_diag.append(y_parts)
            self.xw.append(jnp.concatenate(xw_parts, axis=0))

    def carry(self, h_ref):
        gp = SSM_HEADS_PER_GROUP * SSM_HEAD_DIM
        ys = []
        for grp in range(SSM_GROUPS):
            hg = h_ref[grp * gp:(grp + 1) * gp, :]
            y_off = _dot(hg.astype(BF16), self.ctz[grp])
            st = _dot(self.xw[grp], self.bn)
            h_parts = []
            for e in range(SSM_HEADS_PER_GROUP):
                hd = grp * SSM_HEADS_PER_GROUP + e
                rows = slice(e * SSM_HEAD_DIM, (e + 1) * SSM_HEAD_DIM)
                ys.append(self.y_diag[grp][e] + y_off[rows] * self.e_row[hd])
                h_parts.append(hg[rows] * self.e_tot[hd])
            h_ref[grp * gp:(grp + 1) * gp, :] = jnp.concatenate(h_parts, axis=0) + st
        return jnp.concatenate(ys, axis=0)


def _ssd_run(chunks, h_refs):
    for ch in chunks:
        ch.decay()
    for ch in chunks:
        ch.local()
    return [ch.carry(h_refs[ch.direction]) for ch in chunks]


def _ssd_kernel(xf_ref, xb_ref, bf_ref, bb_ref, cf_ref, cb_ref, df_ref, db_ref, xc_ref, bc_ref, cc_ref, dc_ref,
                alog_ref, dsk_ref, yf_ref, yb_ref, hf_ref, hb_ref):
    c = pl.program_id(1)
    a_col = -jnp.exp(alog_ref[...])
    ln = SSM_CHUNK
    h_refs = (hf_ref, hb_ref)

    def chunk(x_ref, b_ref, c_ref, d_ref, s, direction):
        tok = slice(s * ln, (s + 1) * ln)
        return _SsdChunk(x_ref[0, :, tok], b_ref[0, tok, :], c_ref[0, :, tok], d_ref[0, :, tok], a_col, direction)

    @pl.when(c == 0)
    def _():
        n_ctx = xc_ref.shape[2] // ln
        for direction in range(2):
            h_refs[direction][...] = jnp.zeros_like(h_refs[direction])
        chunks = []
        for k in range(n_ctx):
            chunks.append(chunk(xc_ref, bc_ref, cc_ref, dc_ref, k, 0))
            chunks.append(chunk(xc_ref, bc_ref, cc_ref, dc_ref, n_ctx - 1 - k, 1))
        _ssd_run(chunks, h_refs)

    n_sub = xf_ref.shape[2] // ln
    chunks = []
    for k in range(n_sub):
        chunks.append(chunk(xf_ref, bf_ref, cf_ref, df_ref, k, 0))
        chunks.append(chunk(xb_ref, bb_ref, cb_ref, db_ref, n_sub - 1 - k, 1))
    ys = _ssd_run(chunks, h_refs)
    for k in range(n_sub):
        tok = slice(k * ln, (k + 1) * ln)
        yf_ref[0, :, tok] = ys[2 * k] + xf_ref[0, :, tok] * dsk_ref[...]
        tok = slice((n_sub - 1 - k) * ln, (n_sub - k) * ln)
        yb_ref[0, :, tok] = ys[2 * k + 1]


def _ssd(lat, ctx, alog_col, dsk_col, toks):
    xst, bn, ct, dtt = lat
    b, di, t = xst.shape
    gn = bn.shape[2]
    nh2 = dtt.shape[1]
    nctx = ctx[0].shape[2]
    ns = t // toks
    fcol = lambda bi, c: (bi, 0, c)
    bcol = lambda bi, c: (bi, 0, ns - 1 - c)
    frow = lambda bi, c: (bi, c, 0)
    brow = lambda bi, c: (bi, ns - 1 - c, 0)
    whole = lambda bi, c: (bi, 0, 0)
    return pl.pallas_call(
        _ssd_kernel,
        grid=(b, ns),
        in_specs=[
            pl.BlockSpec((1, di, toks), fcol), pl.BlockSpec((1, di, toks), bcol),
            pl.BlockSpec((1, toks, gn), frow), pl.BlockSpec((1, toks, gn), brow),
            pl.BlockSpec((1, gn, toks), fcol), pl.BlockSpec((1, gn, toks), bcol),
            pl.BlockSpec((1, nh2, toks), fcol), pl.BlockSpec((1, nh2, toks), bcol),
            pl.BlockSpec((1, di, nctx), whole), pl.BlockSpec((1, nctx, gn), whole),
            pl.BlockSpec((1, gn, nctx), whole), pl.BlockSpec((1, nh2, nctx), whole),
            _const_spec((nh2, SSM_CHUNK)),
            _const_spec((di, SSM_CHUNK)),
        ],
        out_specs=[pl.BlockSpec((1, di, toks), fcol), pl.BlockSpec((1, di, toks), bcol)],
        out_shape=[jax.ShapeDtypeStruct((b, di, t), F32), jax.ShapeDtypeStruct((b, di, t), F32)],
        scratch_shapes=[pltpu.VMEM((di, gn), F32)] * 2,
        compiler_params=pltpu.CompilerParams(dimension_semantics=("parallel", "arbitrary"),
                                             vmem_limit_bytes=VMEM_LIMIT),
        name="ssd",
    )(xst, xst, bn, bn, ct, ct, dtt, dtt, *ctx, alog_col, dsk_col)


def _row_groups(tm):
    n = ROW_GROUPS if tm % (ROW_GROUPS * SUBLANES * 2) == 0 else 1
    return [slice(i * (tm // n), (i + 1) * (tm // n)) for i in range(n)]


def _ffn_tail(x1s, m, nw, wg_ref, wu_ref, wd_ref):
    hs = [((_rms(x1) * nw) * (1.0 + m[4:5]) + m[3:4]).astype(BF16) for x1 in x1s]
    ffn = wg_ref.shape[1]
    fs = [None] * len(x1s)
    for c0 in range(0, ffn, FFN_CHUNK):
        c1 = min(c0 + FFN_CHUNK, ffn)
        hids = [(_silu(_dot(h, wg_ref[:, c0:c1])) * _dot(h, wu_ref[:, c0:c1])).astype(BF16) for h in hs]
        for i, hid in enumerate(hids):
            part = _dot(hid, wd_ref[c0:c1, :])
            fs[i] = part if fs[i] is None else fs[i] + part
    return [x1 + m[5:6] * f for x1, f in zip(x1s, fs)]


def _mix0_kernel(x_ref, a_ref, yf_ref, yb_ref, z_ref, mod_ref, snw_ref, nw_ref,
                 wa_ref, wy_ref, wg_ref, wu_ref, wd_ref, o_ref):
    m = mod_ref[0]
    gw = SSM_D_INNER // SSM_GROUPS
    groups = _row_groups(x_ref.shape[1])
    yns = []
    for rows in groups:
        y = (yf_ref[0, :, rows] + yb_ref[0, :, rows]).T
        gy = y * _silu(z_ref[0, rows, :])
        gn = jnp.concatenate([_rms(gy[:, i * gw:(i + 1) * gw]) for i in range(SSM_GROUPS)], axis=1)
        yns.append((gn * snw_ref[...]).astype(BF16))
    mix_a = [_dot(a_ref[0, rows, :], wa_ref[...]) for rows in groups]
    x1s = []
    for rows, yn, ma in zip(groups, yns, mix_a):
        x1s.append(x_ref[0, rows, :] + m[2:3] * (ma + _dot(yn, wy_ref[...])))
    for rows, out in zip(groups, _ffn_tail(x1s, m, nw_ref[...], wg_ref, wu_ref, wd_ref)):
        o_ref[0, rows, :] = out


def _mix0(x, a, yf, yb, z, mod, snw, nw, wa, wy, wg, wu, wd, tm):
    b, t, d = x.shape
    ffn = wg.shape[1]
    row = lambda bi, i: (bi, i, 0)
    col = lambda bi, i: (bi, 0, i)
    return pl.pallas_call(
        _mix0_kernel,
        grid=(b, t // tm),
        in_specs=[
            pl.BlockSpec((1, tm, d), row),
            pl.BlockSpec((1, tm, Q_DIM), row),
            pl.BlockSpec((1, SSM_D_INNER, tm), col),
            pl.BlockSpec((1, SSM_D_INNER, tm), col),
            pl.BlockSpec((1, tm, SSM_D_INNER), row),
            pl.BlockSpec((1, N_MOD, d), lambda bi, i: (bi, 0, 0)),
            _const_spec((1, SSM_D_INNER)),
            _const_spec((1, d)),
            _const_spec((Q_DIM, d)),
            _const_spec((SSM_D_INNER, d)),
            _const_spec((d, ffn)),
            _const_spec((d, ffn)),
            _const_spec((ffn, d)),
        ],
        out_specs=pl.BlockSpec((1, tm, d), row),
        out_shape=jax.ShapeDtypeStruct((b, t, d), F32),
        compiler_params=_cparams(2),
        name="mix0",
    )(x, a, yf, yb, z, mod, snw, nw, wa, wy, wg, wu, wd)


def _layer1_kernel(x_ref, xp_ref, xn_ref, mod_ref, nmix_ref, win_ref, cw_ref, wout_ref, nffn_ref,
                   wg_ref, wu_ref, wd_ref, fn_ref, o_ref, v_ref):
    i = pl.program_id(1)
    last = pl.num_programs(1) - 1
    m = mod_ref[0]
    tm, d = x_ref.shape[1], x_ref.shape[2]
    x = x_ref[0]
    xa = jnp.concatenate([xp_ref[0], x, xn_ref[0]], axis=0)
    h = _rms(xa) * nmix_ref[...]
    h = h * (1.0 + m[1:2]) + m[0:1]
    pcu = _dot(h.astype(BF16), win_ref[:, d:])
    pgb = _dot(h[SUBLANES:SUBLANES + tm].astype(BF16), win_ref[:, :d])
    v = pcu[:, :d] * pcu[:, d:]
    rid = lax.broadcasted_iota(jnp.int32, (tm + 2 * SUBLANES, 1), 0)
    keep = jnp.logical_and(jnp.logical_or(rid >= SUBLANES, i > 0),
                           jnp.logical_or(rid < SUBLANES + tm, i < last))
    v_ref[...] = jnp.where(keep, v, 0.0)
    cw = cw_ref[...]
    groups = _row_groups(tm)
    x1s = []
    for rows in groups:
        r0 = rows.start + SUBLANES
        n = rows.stop - rows.start
        conv = (v_ref[r0 - 1:r0 - 1 + n] * cw[0:1] + v_ref[r0:r0 + n] * cw[1:2] + v_ref[r0 + 1:r0 + 1 + n] * cw[2:3])
        gated = (pgb[rows] * conv).astype(BF16)
        x1s.append(x[rows] + m[2:3] * _dot(gated, wout_ref[...]))
    for rows, x2 in zip(groups, _ffn_tail(x1s, m, nffn_ref[...], wg_ref, wu_ref, wd_ref)):
        o_ref[0, rows, :] = _rms(x2) * fn_ref[...]


def _layer1(x, mod, nmix, win, cw, wout, nffn, wg, wu, wd, fn, tm):
    b, t, d = x.shape
    ffn = wg.shape[1]
    r8 = tm // SUBLANES
    nblk8 = t // SUBLANES
    return pl.pallas_call(
        _layer1_kernel,
        grid=(b, t // tm),
        in_specs=[
            pl.BlockSpec((1, tm, d), lambda bi, i: (bi, i, 0)),
            pl.BlockSpec((1, SUBLANES, d), lambda bi, i: (bi, jnp.maximum(i * r8 - 1, 0), 0)),
            pl.BlockSpec((1, SUBLANES, d), lambda bi, i: (bi, jnp.minimum((i + 1) * r8, nblk8 - 1), 0)),
            pl.BlockSpec((1, N_MOD, d), lambda bi, i: (bi, 0, 0)),
            _const_spec((1, d)),
            _const_spec((d, 3 * d)),
            _const_spec((3, d)),
            _const_spec((d, d)),
            _const_spec((1, d)),
            _const_spec((d, ffn)),
            _const_spec((d, ffn)),
            _const_spec((ffn, d)),
            _const_spec((1, d)),
        ],
        out_specs=pl.BlockSpec((1, tm, d), lambda bi, i: (bi, i, 0)),
        out_shape=jax.ShapeDtypeStruct((b, t, d), F32),
        scratch_shapes=[pltpu.VMEM((tm + 2 * SUBLANES, d), F32)],
        compiler_params=_cparams(2),
        name="layer1",
    )(x, x, x, mod, nmix, win, cw, wout, nffn, wg, wu, wd, fn)


def _rope_tables(n_tokens):
    rows = n_tokens // GRID_W
    row = jnp.repeat(jnp.arange(rows), GRID_W).astype(F32)
    col = jnp.tile(jnp.arange(GRID_W), rows).astype(F32)
    inv = 1.0 / (ROPE_THETA ** (jnp.arange(0, AXIS_DIM, 2, dtype=F32) / AXIS_DIM))
    ang = jnp.concatenate([row[:, None] * inv, col[:, None] * inv], axis=-1)
    return jnp.cos(ang), jnp.sin(ang)


_HEAD_PERM = np.concatenate([np.arange(0, HEAD_DIM, 2), np.arange(1, HEAD_DIM, 2)])


def _qk_tables(cos, sin, q_gain, k_gain):
    half = HEAD_DIM // 2
    sign = jnp.concatenate([-jnp.ones((half,), F32), jnp.ones((half,), F32)])
    c64 = jnp.concatenate([cos, cos], axis=1)
    s64 = jnp.concatenate([sin, sin], axis=1) * sign
    swap = np.concatenate([np.arange(half, HEAD_DIM), np.arange(0, half)])

    def tables(gain, scale):
        gp = gain[_HEAD_PERM] * scale
        return c64 * gp[None, :], s64 * gp[swap][None, :]

    aq, bq = tables(q_gain, HEAD_DIM ** -0.5 * LOG2_E)
    ak, bk = tables(k_gain, 1.0)
    return aq.T, bq.T, jnp.tile(ak, (1, ATTN_KV_HEADS)), jnp.tile(bk, (1, ATTN_KV_HEADS))


def kernel(x, c, ctx, c_ctx, ada_w, ada_b, norm_mix, norm_ffn, ffn_w_gate, ffn_w_up, ffn_w_down, hy_w_in,
           hy_q_norm, hy_k_norm, hy_conv_w, hy_conv_b, hy_dt_bias, hy_a_log, hy_d_skip, hy_ssm_norm, hy_w_out,
           sc_w_in, sc_conv_w, sc_w_out, final_norm):
    b, t, d = x.shape
    nctx = ctx.shape[1]
    tm_in = min(512, t)
    tm_mix = min(512, t)

    n_rows = -(-(b + 1) // SUBLANES) * SUBLANES
    cc = jnp.zeros((n_rows, d), F32).at[:b].set(c).at[b].set(c_ctx)
    mods = _modulation(cc, ada_w, ada_b).reshape(ada_w.shape[0], n_rows, N_MOD, d)
    mod0, mod0_ctx, mod1 = mods[0, :b], mods[0, b:b + 1], mods[1, :b]

    w_in = hy_w_in[0]
    idx_k, idx_v, idx_z = Q_DIM, Q_DIM + KV_DIM, Q_DIM + 2 * KV_DIM
    idx_xbc = idx_z + SSM_D_INNER
    idx_dt = idx_xbc + SSM_CONV_DIM
    q_cols = (np.arange(ATTN_HEADS)[:, None] * HEAD_DIM + _HEAD_PERM[None, :]).reshape(-1)
    k_cols = idx_k + (np.arange(ATTN_KV_HEADS)[:, None] * HEAD_DIM + _HEAD_PERM[None, :]).reshape(-1)
    dt_pad = jnp.zeros((d, LANES - 2 * SSM_HEADS), F32)
    wkz = jnp.concatenate([w_in[:, k_cols], w_in[:, idx_z:idx_xbc]], axis=1).astype(BF16)
    wx = jnp.concatenate([w_in[:, idx_xbc:], dt_pad], axis=1).astype(BF16)
    wt = jnp.concatenate([w_in[:, q_cols], w_in[:, idx_v:idx_z]], axis=1).T.astype(BF16)
    nw0 = norm_mix[0][None, :]

    cos, sin = _rope_tables(t)
    aq, bq, ck, sk = _qk_tables(cos, sin, hy_q_norm[0], hy_k_norm[0])
    ones_c, zeros_c = jnp.ones((nctx, AXIS_DIM), F32), jnp.zeros((nctx, AXIS_DIM), F32)
    aq_c, bq_c, ck_c, sk_c = _qk_tables(ones_c, zeros_c, hy_q_norm[0], hy_k_norm[0])
    seg = np.arange(KV_DIM) // HEAD_DIM
    bd = jnp.asarray(seg[:, None] == seg[None, :], BF16)

    conv_w = jnp.concatenate([hy_conv_w[0], jnp.zeros((3, LANES), F32)], axis=1)
    conv_b = jnp.concatenate([hy_conv_b[0], jnp.zeros((LANES,), F32)])[None, :]
    dtb = jnp.concatenate([hy_dt_bias[0].reshape(-1), jnp.zeros((LANES - 2 * SSM_HEADS,), F32)])[None, :]
    shared = (nw0, wx, wkz, wt)
    qt, vt, k, z, *ssd_lat = _inproj(x, mod0, True, *shared, aq, bq, ck, sk, bd, conv_w, conv_b, dtb, tm_in)
    _, vct, kc, _, *ssd_ctx = _inproj(ctx, mod0_ctx, False, *shared, aq_c, bq_c, ck_c, sk_c, bd, conv_w, conv_b, dtb,
                                      nctx)

    k_all = jnp.concatenate([k, kc], axis=1)
    vt_all = jnp.concatenate([vt, vct], axis=2)
    kmax = 1.01 * HEAD_DIM ** 0.5 * jnp.max(jnp.abs(hy_k_norm[0]))
    qmax = 1.01 * LOG2_E * jnp.max(jnp.abs(hy_q_norm[0]))
    score_bound = qmax * kmax
    tq = min(128, t)
    a_lat = lax.cond(score_bound <= SAFE_SCORE_BOUND,
                     functools.partial(_attention_bounded, tq=tq), functools.partial(_attention_exact, tq=tq),
                     qt, k_all, vt_all, kmax)

    alog_col = jnp.broadcast_to(hy_a_log[0].reshape(-1, 1), (2 * SSM_HEADS, SSM_CHUNK))
    dsk_col = jnp.broadcast_to(jnp.repeat(hy_d_skip[0], SSM_HEAD_DIM)[:, None], (SSM_D_INNER, SSM_CHUNK))
    yf, yb = _ssd(ssd_lat, ssd_ctx, alog_col, dsk_col, toks=min(4 * SSM_CHUNK, t))

    w_out = hy_w_out[0].astype(BF16)
    x = _mix0(x, a_lat, yf, yb, z, mod0, hy_ssm_norm[0][None, :], norm_ffn[0][None, :],
              w_out[:Q_DIM], w_out[Q_DIM:], ffn_w_gate[0].astype(BF16), ffn_w_up[0].astype(BF16),
              ffn_w_down[0].astype(BF16), tm_mix)

    return _layer1(x, mod1, norm_mix[1][None, :], sc_w_in[0].astype(BF16), sc_conv_w[0], sc_w_out[0].astype(BF16),
                   norm_ffn[1][None, :], ffn_w_gate[1].astype(BF16), ffn_w_up[1].astype(BF16),
                   ffn_w_down[1].astype(BF16), final_norm[None, :], tm_mix)
```

```python
import functools

import jax
import jax.numpy as jnp
import numpy as np
from jax import lax
from jax.experimental import pallas as pl
from jax.experimental.pallas import tpu as pltpu

F32 = jnp.float32
BF16 = jnp.bfloat16

EPS = 1e-6
N_MOD = 6
GRID_W = 64
ROPE_THETA = 10000.0

ATTN_HEADS = 8
ATTN_KV_HEADS = 2
HEAD_DIM = 64
AXIS_DIM = HEAD_DIM // 2
KV_GROUP = ATTN_HEADS // ATTN_KV_HEADS
Q_DIM = ATTN_HEADS * HEAD_DIM
KV_DIM = ATTN_KV_HEADS * HEAD_DIM

SSM_HEADS = 8
SSM_HEAD_DIM = 64
SSM_D_INNER = SSM_HEADS * SSM_HEAD_DIM
SSM_GROUPS = 2
SSM_STATE = 64
SSM_CHUNK = 128
SSM_CONV_DIM = SSM_D_INNER + 2 * SSM_GROUPS * SSM_STATE
SSM_HEADS_PER_GROUP = SSM_HEADS // SSM_GROUPS

LANES = 128
SUBLANES = 8
XBCDT_W = SSM_CONV_DIM + LANES
NEG_BIG = -1e30
LOG2_E = 1.4426950408889634
ATTN_KEY_TILES = (768, 512, 256, 128)
ROW_GROUPS = 2
FFN_CHUNK = 768
ATTN_TQ_EXACT = 128
ATTN_TQ_BOUNDED = 128
ATTN_Q_BLOCKS = 8
ATTN_SUB_TILE = 256
ATTN_PV_LAG = 3
SAFE_SCORE_BOUND = 40.0
SCORE_BOUND_SLACK = 1.001

VMEM_LIMIT = 56 * 1024 * 1024


def _cparams(n_axes):
    return pltpu.CompilerParams(dimension_semantics=("parallel",) * n_axes, vmem_limit_bytes=VMEM_LIMIT)


def _const_spec(shape):
    nd = len(shape)
    return pl.BlockSpec(shape, lambda *_: (0,) * nd, pipeline_mode=pl.Buffered(1))


def _rms(x):
    return x * lax.rsqrt(jnp.mean(x * x, axis=-1, keepdims=True) + EPS)


def _silu(x):
    return x * (1.0 / (1.0 + jnp.exp(-x)))


def _dot(a, b):
    return jnp.dot(a, b, preferred_element_type=F32)


def _dot_exact(a, b):
    return jnp.dot(a, b, preferred_element_type=F32, precision=lax.Precision.HIGHEST)


def _mod_kernel(c_ref, w_ref, b_ref, o_ref):
    c = c_ref[...]
    o_ref[0] = _dot_exact(_silu(c), w_ref[0]) + b_ref[0]


def _modulation(cc, ada_w, ada_b):
    depth, d, nd = ada_w.shape
    r = cc.shape[0]
    tn = d
    return pl.pallas_call(
        _mod_kernel,
        grid=(depth, nd // tn),
        in_specs=[
            pl.BlockSpec((r, d), lambda l, j: (0, 0)),
            pl.BlockSpec((1, d, tn), lambda l, j: (l, 0, j)),
            pl.BlockSpec((1, 1, tn), lambda l, j: (l, 0, j)),
        ],
        out_specs=pl.BlockSpec((1, r, tn), lambda l, j: (l, 0, j)),
        out_shape=jax.ShapeDtypeStruct((depth, r, nd), F32),
        compiler_params=_cparams(2),
        name="mod",
    )(cc, ada_w, ada_b.reshape(depth, 1, nd))


def _inproj_kernel(x_ref, xp_ref, xn_ref, mod_ref, nw_ref, wx_ref, wkz_ref, wt_ref, aq_ref, bq_ref, ck_ref, sk_ref,
                   bd_ref, cw_ref, cb_ref, dtb_ref,
                   qt_ref, vt_ref, k_ref, z_ref, xst_ref, bn_ref, ct_ref, dtt_ref, sc_ref):
    i = pl.program_id(1)
    last = pl.num_programs(1) - 1
    m = mod_ref[0]
    tm = x_ref.shape[1]
    half = HEAD_DIM // 2
    xa = jnp.concatenate([xp_ref[0], x_ref[0], xn_ref[0]], axis=0)
    h = _rms(xa) * (nw_ref[...] * (1.0 + m[1:2])) + m[0:1]
    px = _dot(h.astype(BF16), wx_ref[...])
    groups = [slice(0, tm)]
    proj = []
    for rows in groups:
        hc = h[SUBLANES + rows.start:SUBLANES + rows.stop].astype(BF16)
        pkz = _dot(hc, wkz_ref[...])
        pt = lax.dot_general(wt_ref[...], hc, (((1,), (1,)), ((), ())), preferred_element_type=F32)
        sq = pkz[:, :KV_DIM] * pkz[:, :KV_DIM]
        hi = sq.astype(BF16)
        lo = (sq - hi.astype(F32)).astype(BF16)
        proj.append((pkz, pt, _dot(hi, bd_ref[...]) + _dot(lo, bd_ref[...])))

    sc_ref[0:SUBLANES] = jnp.where(i > 0, px[:SUBLANES], 0.0)
    sc_ref[SUBLANES:SUBLANES + tm] = px[SUBLANES:SUBLANES + tm]
    sc_ref[SUBLANES + tm:] = jnp.where(i < last, px[SUBLANES + tm:], 0.0)
    cur = sc_ref[SUBLANES:SUBLANES + tm]
    w = cw_ref[...]
    y = (sc_ref[SUBLANES - 1:SUBLANES - 1 + tm] * w[0:1] + cur * w[1:2]
         + sc_ref[SUBLANES + 1:SUBLANES + 1 + tm] * w[2:3] + cb_ref[...])
    u = _silu(y[:, :SSM_CONV_DIM])
    bc = SSM_D_INNER + SSM_GROUPS * SSM_STATE
    xst_ref[0] = u[:, :SSM_D_INNER].T
    bn_ref[0] = u[:, SSM_D_INNER:bc].astype(BF16)
    ct_ref[0] = u[:, bc:].T.astype(BF16)
    d = cur[:, SSM_CONV_DIM:] + dtb_ref[...]
    dt = jnp.maximum(d, 0.0) + jnp.log1p(jnp.exp(-jnp.abs(d)))
    dtt_ref[0] = dt.T[:2 * SSM_HEADS]

    for rows, (pkz, pt, ss) in zip(groups, proj):
        n = rows.stop - rows.start
        kx = pkz[:, :KV_DIM]
        rs = lax.rsqrt(ss * (1.0 / HEAD_DIM) + EPS)
        lane = lax.broadcasted_iota(jnp.int32, kx.shape, 1)
        swap = jnp.where((lane & half) == 0, pltpu.roll(kx, KV_DIM - half, 1), pltpu.roll(kx, half, 1))
        k_ref[0, rows, :] = (rs * (kx * ck_ref[rows, :] + swap * sk_ref[rows, :])).astype(BF16)

        q3 = pt[:Q_DIM].reshape(ATTN_HEADS, HEAD_DIM, n)
        qrs = lax.rsqrt(jnp.mean(q3 * q3, axis=1, keepdims=True) + EPS)
        qsw = jnp.concatenate([q3[:, half:], q3[:, :half]], axis=1)
        qo = qrs * (q3 * aq_ref[:, rows][None] + qsw * bq_ref[:, rows][None])
        qt_ref[0, :, rows] = qo.reshape(Q_DIM, n).astype(BF16)

        vt_ref[0, :, rows] = pt[Q_DIM:].astype(BF16)
        z_ref[0, rows, :] = pkz[:, KV_DIM:]


def _inproj(x, mod, per_batch_mod, nw, wx, wkz, wt, aq, bq, ck, sk, bd, cw, cb, dtb, tm):
    b, t, d = x.shape
    r8 = tm // SUBLANES
    nblk8 = t // SUBLANES
    gn = SSM_GROUPS * SSM_STATE
    mod_idx = (lambda bi, i: (bi, 0, 0)) if per_batch_mod else (lambda bi, i: (0, 0, 0))
    row = lambda bi, i: (bi, i, 0)
    col = lambda bi, i: (bi, 0, i)
    return pl.pallas_call(
        _inproj_kernel,
        grid=(b, t // tm),
        in_specs=[
            pl.BlockSpec((1, tm, d), row),
            pl.BlockSpec((1, SUBLANES, d), lambda bi, i: (bi, jnp.maximum(i * r8 - 1, 0), 0)),
            pl.BlockSpec((1, SUBLANES, d), lambda bi, i: (bi, jnp.minimum((i + 1) * r8, nblk8 - 1), 0)),
            pl.BlockSpec((1, N_MOD, d), mod_idx),
            _const_spec((1, d)),
            _const_spec((d, XBCDT_W)),
            _const_spec((d, KV_DIM + SSM_D_INNER)),
            _const_spec((Q_DIM + KV_DIM, d)),
            pl.BlockSpec((HEAD_DIM, tm), lambda bi, i: (0, i)),
            pl.BlockSpec((HEAD_DIM, tm), lambda bi, i: (0, i)),
            pl.BlockSpec((tm, KV_DIM), lambda bi, i: (i, 0)),
            pl.BlockSpec((tm, KV_DIM), lambda bi, i: (i, 0)),
            _const_spec((KV_DIM, KV_DIM)),
            _const_spec((3, XBCDT_W)),
            _const_spec((1, XBCDT_W)),
            _const_spec((1, LANES)),
        ],
        out_specs=[
            pl.BlockSpec((1, Q_DIM, tm), col),
            pl.BlockSpec((1, KV_DIM, tm), col),
            pl.BlockSpec((1, tm, KV_DIM), row),
            pl.BlockSpec((1, tm, SSM_D_INNER), row),
            pl.BlockSpec((1, SSM_D_INNER, tm), col),
            pl.BlockSpec((1, tm, gn), row),
            pl.BlockSpec((1, gn, tm), col),
            pl.BlockSpec((1, 2 * SSM_HEADS, tm), col),
        ],
        out_shape=[
            jax.ShapeDtypeStruct((b, Q_DIM, t), BF16),
            jax.ShapeDtypeStruct((b, KV_DIM, t), BF16),
            jax.ShapeDtypeStruct((b, t, KV_DIM), BF16),
            jax.ShapeDtypeStruct((b, t, SSM_D_INNER), F32),
            jax.ShapeDtypeStruct((b, SSM_D_INNER, t), F32),
            jax.ShapeDtypeStruct((b, t, gn), BF16),
            jax.ShapeDtypeStruct((b, gn, t), BF16),
            jax.ShapeDtypeStruct((b, 2 * SSM_HEADS, t), F32),
        ],
        scratch_shapes=[pltpu.VMEM((tm + 2 * SUBLANES, XBCDT_W), F32)],
        compiler_params=_cparams(2),
        name="inproj",
    )(x, x, x, mod, nw, wx, wkz, wt, aq, bq, ck, sk, bd, cw, cb, dtb)


def _attn_kernel(qt_ref, k_ref, vt_ref, o_ref, s0_ref, s1_ref, p0_ref, p1_ref, m0_ref, m1_ref, *, nq):
    g = pl.program_id(1)
    i = pl.program_id(2)
    tq = qt_ref.shape[2]
    cols = KV_GROUP * tq
    nk, kt = s0_ref.shape[0], s0_ref.shape[1]
    s_refs, m_refs, p_refs = (s0_ref, s1_ref), (m0_ref, m1_ref), (p0_ref, p1_ref)

    def run(a_buf, b_buf):
        do_a, do_bc = a_buf is not None, b_buf is not None
        if do_a:
            q4 = qt_ref[0]
            qcat = jnp.concatenate([q4[h * HEAD_DIM:(h + 1) * HEAD_DIM] for h in range(KV_GROUP)], axis=1)
            qp = jnp.concatenate([qcat] * ATTN_KV_HEADS, axis=0)
            rowgrp = lax.broadcasted_iota(jnp.int32, qp.shape, 0) // HEAD_DIM
            qp = jnp.where(rowgrp == g, qp, jnp.zeros_like(qp))
        if do_bc:
            m8 = jnp.broadcast_to(jnp.max(m_refs[b_buf][...], axis=0, keepdims=True), (SUBLANES, cols))

        def stage_a(j, mx):
            ktile = k_ref[0, pl.ds(pl.multiple_of(j * kt, kt), kt), :]
            s = _dot(ktile, qp)
            s_refs[a_buf][j] = s
            return jnp.maximum(mx, jnp.max(s.reshape(kt // SUBLANES, SUBLANES, cols), axis=0))

        def stage_b(j, par, ls):
            s = s_refs[b_buf][j].reshape(kt // SUBLANES, SUBLANES, cols)
            p = jnp.exp2(s - m8[None])
            p_refs[par][...] = p.reshape(kt, cols).astype(BF16)
            return ls + jnp.sum(p, axis=0)

        def stage_c(j, par, acc):
            return acc + _dot(vt_ref[0, 0, j], p_refs[par][...])

        def tile(j, par, carry):
            mx, ls, acc = carry
            if do_a:
                mx = stage_a(j, mx)
            if do_bc:
                acc = stage_c(j - 1, 1 - par, acc)
                ls = stage_b(j, par, ls)
            return mx, ls, acc

        mx = jnp.full((SUBLANES, cols), NEG_BIG, F32)
        ls = jnp.zeros((SUBLANES, cols), F32)
        acc = jnp.zeros((HEAD_DIM, cols), F32)
        if do_a:
            mx = stage_a(0, mx)
        if do_bc:
            ls = stage_b(0, 0, ls)

        def body(t, carry):
            j = 1 + 2 * t
            return tile(j + 1, 0, tile(j, 1, carry))

        carry = lax.fori_loop(0, (nk - 1) // 2, body, (mx, ls, acc), unroll=True)
        if (nk - 1) % 2:
            carry = tile(nk - 1, (nk - 1) % 2, carry)
        mx, ls, acc = carry
        if do_a:
            m_refs[a_buf][...] = mx
        if do_bc:
            acc = stage_c(nk - 1, (nk - 1) % 2, acc)
            o = acc * (1.0 / jnp.sum(ls, axis=0, keepdims=True))
            o4 = jnp.concatenate([o[:, h * tq:(h + 1) * tq] for h in range(KV_GROUP)], axis=0)
            o_ref[0] = o4.T.astype(BF16)

    @pl.when(i == 0)
    def _():
        run(0, None)

    inner = jnp.logical_and(i > 0, i < nq)

    @pl.when(jnp.logical_and(inner, i % 2 == 1))
    def _():
        run(1, 0)

    @pl.when(jnp.logical_and(inner, i % 2 == 0))
    def _():
        run(0, 1)

    @pl.when(i == nq)
    def _():
        run(None, (nq - 1) % 2)


def _attention(qt, k_all, vt_tiles, tq):
    b, _, t = qt.shape
    tk = k_all.shape[1]
    nk, kt = vt_tiles.shape[2], vt_tiles.shape[4]
    nq = t // tq
    gw = KV_GROUP * HEAD_DIM
    cols = KV_GROUP * tq
    return pl.pallas_call(
        functools.partial(_attn_kernel, nq=nq),
        grid=(b, ATTN_KV_HEADS, nq + 1),
        in_specs=[
            pl.BlockSpec((1, gw, tq), lambda bi, g, i: (bi, g, jnp.minimum(i, nq - 1))),
            pl.BlockSpec((1, tk, KV_DIM), lambda bi, g, i: (bi, 0, 0)),
            pl.BlockSpec((1, 1, nk, HEAD_DIM, kt), lambda bi, g, i: (bi, g, 0, 0, 0)),
        ],
        out_specs=pl.BlockSpec((1, tq, gw), lambda bi, g, i: (bi, jnp.maximum(i - 1, 0), g)),
        out_shape=jax.ShapeDtypeStruct((b, t, Q_DIM), BF16),
        scratch_shapes=(
            [pltpu.VMEM((nk, kt, cols), F32)] * 2
            + [pltpu.VMEM((kt, cols), BF16)] * 2
            + [pltpu.VMEM((SUBLANES, cols), F32)] * 2
        ),
        compiler_params=pltpu.CompilerParams(dimension_semantics=("parallel", "parallel", "arbitrary"),
                                             vmem_limit_bytes=VMEM_LIMIT),
        name="attn",
    )(qt, k_all, vt_tiles)


def _attn_bounded_kernel(qt_ref, k_ref, vt_ref, km_ref, o_ref, *p_refs):
    g = pl.program_id(1)
    cols = p_refs[0].shape[1]
    tq = cols // KV_GROUP
    n_blk = qt_ref.shape[2] // tq
    sub = p_refs[0].shape[0]
    n_all = k_ref.shape[1] // sub

    kmax = jnp.concatenate([km_ref[0:1, :]] * (cols // LANES), axis=1)
    qps, mb8s = [], []
    for blk in range(n_blk):
        q4 = qt_ref[0, :, blk * tq:(blk + 1) * tq]
        qf = q4.astype(F32)
        qn = jnp.sqrt(jnp.sum((qf * qf).reshape(KV_GROUP, HEAD_DIM, tq), axis=1))
        qn = jnp.concatenate([qn[h:h + 1] for h in range(KV_GROUP)], axis=1)
        mb8s.append(jnp.broadcast_to(qn * kmax * SCORE_BOUND_SLACK, (SUBLANES, cols)))
        qcat = jnp.concatenate([q4[h * HEAD_DIM:(h + 1) * HEAD_DIM] for h in range(KV_GROUP)], axis=1)
        qp = jnp.concatenate([qcat] * ATTN_KV_HEADS, axis=0)
        rowgrp = lax.broadcasted_iota(jnp.int32, qp.shape, 0) // HEAD_DIM
        qps.append(jnp.where(rowgrp == g, qp, jnp.zeros_like(qp)))

    ring = len(p_refs)
    lag = ring - 1
    ls = [jnp.zeros((SUBLANES, cols), F32) for _ in range(n_blk)]
    acc = [jnp.zeros((HEAD_DIM, cols), F32) for _ in range(n_blk)]

    def pv(w):
        blk, u = divmod(w, n_all)
        acc[blk] = acc[blk] + _dot(vt_ref[0, :, u * sub:(u + 1) * sub], p_refs[w % ring][...])
        if u == n_all - 1:
            o = acc[blk] * (1.0 / jnp.sum(ls[blk], axis=0, keepdims=True))
            o4 = jnp.concatenate([o[:, h * tq:(h + 1) * tq] for h in range(KV_GROUP)], axis=0)
            o_ref[0, blk * tq:(blk + 1) * tq, :] = o4.T.astype(BF16)

    n_items = n_blk * n_all
    for w in range(n_items):
        blk, u = divmod(w, n_all)
        s = _dot(k_ref[0, u * sub:(u + 1) * sub, :], qps[blk])
        if w >= lag:
            pv(w - lag)
        p = jnp.exp2(s.reshape(sub // SUBLANES, SUBLANES, cols) - mb8s[blk][None])
        p_refs[w % ring][...] = p.reshape(sub, cols).astype(BF16)
        ls[blk] = ls[blk] + jnp.sum(p, axis=0)
    for w in range(max(n_items - lag, 0), n_items):
        pv(w)


def _attention_bounded(qt, k_all, vt_all, kmax, tq):
    b, _, t = qt.shape
    tk = k_all.shape[1]
    sub = next(c for c in (ATTN_SUB_TILE, LANES) if tk % c == 0)
    gw = KV_GROUP * HEAD_DIM
    cols = KV_GROUP * tq
    tstep = tq * (ATTN_Q_BLOCKS if t % (tq * ATTN_Q_BLOCKS) == 0 else 1)
    return pl.pallas_call(
        _attn_bounded_kernel,
        grid=(b, ATTN_KV_HEADS, t // tstep),
        in_specs=[
            pl.BlockSpec((1, gw, tstep), lambda bi, g, i: (bi, g, i)),
            pl.BlockSpec((1, tk, KV_DIM), lambda bi, g, i: (bi, 0, 0)),
            pl.BlockSpec((1, HEAD_DIM, tk), lambda bi, g, i: (bi, g, 0)),
            _const_spec((SUBLANES, LANES)),
        ],
        out_specs=pl.BlockSpec((1, tstep, gw), lambda bi, g, i: (bi, i, g)),
        out_shape=jax.ShapeDtypeStruct((b, t, Q_DIM), BF16),
        scratch_shapes=[pltpu.VMEM((sub, cols), BF16)] * (ATTN_PV_LAG + 1),
        compiler_params=_cparams(3),
        name="attn_bounded",
    )(qt, k_all, vt_all, jnp.full((SUBLANES, LANES), kmax, F32))


def _attention_exact(qt, k_all, vt_all, kmax, tq):
    del kmax
    b, _, tk = vt_all.shape
    kt = next(c for c in ATTN_KEY_TILES if tk % c == 0)
    vt_tiles = vt_all.reshape(b, ATTN_KV_HEADS, HEAD_DIM, tk // kt, kt).transpose(0, 1, 3, 2, 4)
    return _attention(qt, k_all, vt_tiles, tq)


class _SsdChunk:
    def __init__(self, xst, bn, ct, dtt, a_col, direction):
        self.xst, self.bn, self.ct, self.dtt, self.a_col, self.direction = xst, bn, ct, dtt, a_col, direction

    def decay(self):
        ln = self.xst.shape[1]
        da = self.dtt * self.a_col
        ri = lax.broadcasted_iota(jnp.int32, (ln, ln), 0)
        ci = lax.broadcasted_iota(jnp.int32, (ln, ln), 1)
        self.reach = (ci >= ri) if self.direction == 0 else (ci <= ri)
        self.cs_row = _dot_exact(da, self.reach.astype(F32))
        self.tot = _dot_exact(da, jnp.ones((ln, ln), F32))

    def local(self):
        ln = self.xst.shape[1]
        gn = SSM_GROUPS * SSM_STATE
        cs_col = jnp.concatenate([self.cs_row, jnp.zeros((ln - 2 * SSM_HEADS, ln), F32)], axis=0).T
        rowgrp = lax.broadcasted_iota(jnp.int32, (gn, ln), 0) // SSM_STATE
        self.ctz, self.y_diag, self.xw, self.e_row, self.e_tot = [], [], [], [], []
        for grp in range(SSM_GROUPS):
            ctz = jnp.where(rowgrp == grp, self.ct, jnp.zeros_like(self.ct))
            cbt = _dot(self.bn, ctz)
            y_parts, xw_parts = [], []
            for e in range(SSM_HEADS_PER_GROUP):
                hd = grp * SSM_HEADS_PER_GROUP + e
                c = self.direction * SSM_HEADS + hd
                row = self.cs_row[c:c + 1, :]
                lmt = jnp.exp(jnp.where(self.reach, row - cs_col[:, c:c + 1], NEG_BIG))
                mt = (cbt * lmt).astype(BF16)
                xdt = self.xst[hd * SSM_HEAD_DIM:(hd + 1) * SSM_HEAD_DIM, :] * self.dtt[c:c + 1, :]
                y_parts.append(_dot(xdt.astype(BF16), mt))
                xw_parts.append((xdt * jnp.exp(self.tot[c:c + 1, :] - row)).astype(BF16))
                self.e_row.append(jnp.exp(row))
                self.e_tot.append(jnp.exp(self.tot[c:c + 1, :gn]))
            self.ctz.append(ctz)
            self.y_diag.append(y_parts)
            self.xw.append(jnp.concatenate(xw_parts, axis=0))

    def carry(self, h_ref):
        gp = SSM_HEADS_PER_GROUP * SSM_HEAD_DIM
        ys = []
        for grp in range(SSM_GROUPS):
            hg = h_ref[grp * gp:(grp + 1) * gp, :]
            y_off = _dot(hg.astype(BF16), self.ctz[grp])
            st = _dot(self.xw[grp], self.bn)
            h_parts = []
            for e in range(SSM_HEADS_PER_GROUP):
                hd = grp * SSM_HEADS_PER_GROUP + e
                rows = slice(e * SSM_HEAD_DIM, (e + 1) * SSM_HEAD_DIM)
                ys.append(self.y_diag[grp][e] + y_off[rows] * self.e_row[hd])
                h_parts.append(hg[rows] * self.e_tot[hd])
            h_ref[grp * gp:(grp + 1) * gp, :] = jnp.concatenate(h_parts, axis=0) + st
        return jnp.concatenate(ys, axis=0)


def _ssd_run(chunks, h_refs):
    for ch in chunks:
        ch.decay()
    for ch in chunks:
        ch.local()
    return [ch.carry(h_refs[ch.direction]) for ch in chunks]


def _ssd_kernel(xf_ref, xb_ref, bf_ref, bb_ref, cf_ref, cb_ref, df_ref, db_ref, xc_ref, bc_ref, cc_ref, dc_ref,
                alog_ref, dsk_ref, yf_ref, yb_ref, hf_ref, hb_ref):
    c = pl.program_id(1)
    a_col = -jnp.exp(alog_ref[...])
    ln = SSM_CHUNK
    h_refs = (hf_ref, hb_ref)

    def chunk(x_ref, b_ref, c_ref, d_ref, s, direction):
        tok = slice(s * ln, (s + 1) * ln)
        return _SsdChunk(x_ref[0, :, tok], b_ref[0, tok, :], c_ref[0, :, tok], d_ref[0, :, tok], a_col, direction)

    @pl.when(c == 0)
    def _():
        n_ctx = xc_ref.shape[2] // ln
        for direction in range(2):
            h_refs[direction][...] = jnp.zeros_like(h_refs[direction])
        chunks = []
        for k in range(n_ctx):
            chunks.append(chunk(xc_ref, bc_ref, cc_ref, dc_ref, k, 0))
            chunks.append(chunk(xc_ref, bc_ref, cc_ref, dc_ref, n_ctx - 1 - k, 1))
        _ssd_run(chunks, h_refs)

    n_sub = xf_ref.shape[2] // ln
    chunks = []
    for k in range(n_sub):
        chunks.append(chunk(xf_ref, bf_ref, cf_ref, df_ref, k, 0))
        chunks.append(chunk(xb_ref, bb_ref, cb_ref, db_ref, n_sub - 1 - k, 1))
    ys = _ssd_run(chunks, h_refs)
    for k in range(n_sub):
        tok = slice(k * ln, (k + 1) * ln)
        yf_ref[0, :, tok] = ys[2 * k] + xf_ref[0, :, tok] * dsk_ref[...]
        tok = slice((n_sub - 1 - k) * ln, (n_sub - k) * ln)
        yb_ref[0, :, tok] = ys[2 * k + 1]


def _ssd(lat, ctx, alog_col, dsk_col, toks):
    xst, bn, ct, dtt = lat
    b, di, t = xst.shape
    gn = bn.shape[2]
    nh2 = dtt.shape[1]
    nctx = ctx[0].shape[2]
    ns = t // toks
    fcol = lambda bi, c: (bi, 0, c)
    bcol = lambda bi, c: (bi, 0, ns - 1 - c)
    frow = lambda bi, c: (bi, c, 0)
    brow = lambda bi, c: (bi, ns - 1 - c, 0)
    whole = lambda bi, c: (bi, 0, 0)
    return pl.pallas_call(
        _ssd_kernel,
        grid=(b, ns),
        in_specs=[
            pl.BlockSpec((1, di, toks), fcol), pl.BlockSpec((1, di, toks), bcol),
            pl.BlockSpec((1, toks, gn), frow), pl.BlockSpec((1, toks, gn), brow),
            pl.BlockSpec((1, gn, toks), fcol), pl.BlockSpec((1, gn, toks), bcol),
            pl.BlockSpec((1, nh2, toks), fcol), pl.BlockSpec((1, nh2, toks), bcol),
            pl.BlockSpec((1, di, nctx), whole), pl.BlockSpec((1, nctx, gn), whole),
            pl.BlockSpec((1, gn, nctx), whole), pl.BlockSpec((1, nh2, nctx), whole),
            _const_spec((nh2, SSM_CHUNK)),
            _const_spec((di, SSM_CHUNK)),
        ],
        out_specs=[pl.BlockSpec((1, di, toks), fcol), pl.BlockSpec((1, di, toks), bcol)],
        out_shape=[jax.ShapeDtypeStruct((b, di, t), F32), jax.ShapeDtypeStruct((b, di, t), F32)],
        scratch_shapes=[pltpu.VMEM((di, gn), F32)] * 2,
        compiler_params=pltpu.CompilerParams(dimension_semantics=("parallel", "arbitrary"),
                                             vmem_limit_bytes=VMEM_LIMIT),
        name="ssd",
    )(xst, xst, bn, bn, ct, ct, dtt, dtt, *ctx, alog_col, dsk_col)


def _row_groups(tm):
    n = ROW_GROUPS if tm % (ROW_GROUPS * SUBLANES * 2) == 0 else 1
    return [slice(i * (tm // n), (i + 1) * (tm // n)) for i in range(n)]


def _ffn_tail(x1s, m, nw, wg_ref, wu_ref, wd_ref):
    scale = nw * (1.0 + m[4:5])
    hs = [(_rms(x1) * scale + m[3:4]).astype(BF16) for x1 in x1s]
    ffn = wg_ref.shape[1]
    fs = [None] * len(x1s)
    for c0 in range(0, ffn, FFN_CHUNK):
        c1 = min(c0 + FFN_CHUNK, ffn)
        hids = [(_silu(_dot(h, wg_ref[:, c0:c1])) * _dot(h, wu_ref[:, c0:c1])).astype(BF16) for h in hs]
        for i, hid in enumerate(hids):
            part = _dot(hid, wd_ref[c0:c1, :])
            fs[i] = part if fs[i] is None else fs[i] + part
    return [x1 + m[5:6] * f for x1, f in zip(x1s, fs)]


def _mix0_kernel(x_ref, a_ref, yf_ref, yb_ref, z_ref, mod_ref, snw_ref, nw_ref,
                 wa_ref, wy_ref, wg_ref, wu_ref, wd_ref, o_ref):
    m = mod_ref[0]
    gw = SSM_D_INNER // SSM_GROUPS
    groups = _row_groups(x_ref.shape[1])
    yns = []
    for rows in groups:
        y = (yf_ref[0, :, rows] + yb_ref[0, :, rows]).T
        gy = y * _silu(z_ref[0, rows, :])
        gn = jnp.concatenate([_rms(gy[:, i * gw:(i + 1) * gw]) for i in range(SSM_GROUPS)], axis=1)
        yns.append((gn * snw_ref[...]).astype(BF16))
    mix_a = [_dot(a_ref[0, rows, :], wa_ref[...]) for rows in groups]
    x1s = []
    for rows, yn, ma in zip(groups, yns, mix_a):
        x1s.append(x_ref[0, rows, :] + m[2:3] * (ma + _dot(yn, wy_ref[...])))
    for rows, out in zip(groups, _ffn_tail(x1s, m, nw_ref[...], wg_ref, wu_ref, wd_ref)):
        o_ref[0, rows, :] = out


def _mix0(x, a, yf, yb, z, mod, snw, nw, wa, wy, wg, wu, wd, tm):
    b, t, d = x.shape
    ffn = wg.shape[1]
    row = lambda bi, i: (bi, i, 0)
    col = lambda bi, i: (bi, 0, i)
    return pl.pallas_call(
        _mix0_kernel,
        grid=(b, t // tm),
        in_specs=[
            pl.BlockSpec((1, tm, d), row),
            pl.BlockSpec((1, tm, Q_DIM), row),
            pl.BlockSpec((1, SSM_D_INNER, tm), col),
            pl.BlockSpec((1, SSM_D_INNER, tm), col),
            pl.BlockSpec((1, tm, SSM_D_INNER), row),
            pl.BlockSpec((1, N_MOD, d), lambda bi, i: (bi, 0, 0)),
            _const_spec((1, SSM_D_INNER)),
            _const_spec((1, d)),
            _const_spec((Q_DIM, d)),
            _const_spec((SSM_D_INNER, d)),
            _const_spec((d, ffn)),
            _const_spec((d, ffn)),
            _const_spec((ffn, d)),
        ],
        out_specs=pl.BlockSpec((1, tm, d), row),
        out_shape=jax.ShapeDtypeStruct((b, t, d), F32),
        compiler_params=_cparams(2),
        name="mix0",
    )(x, a, yf, yb, z, mod, snw, nw, wa, wy, wg, wu, wd)


def _layer1_kernel(x_ref, xp_ref, xn_ref, mod_ref, nmix_ref, win_ref, cw_ref, wout_ref, nffn_ref,
                   wg_ref, wu_ref, wd_ref, fn_ref, o_ref, v_ref):
    i = pl.program_id(1)
    last = pl.num_programs(1) - 1
    m = mod_ref[0]
    tm, d = x_ref.shape[1], x_ref.shape[2]
    x = x_ref[0]
    xa = jnp.concatenate([xp_ref[0], x, xn_ref[0]], axis=0)
    h = _rms(xa) * (nmix_ref[...] * (1.0 + m[1:2])) + m[0:1]
    pcu = _dot(h.astype(BF16), win_ref[:, d:])
    pgb = _dot(h[SUBLANES:SUBLANES + tm].astype(BF16), win_ref[:, :d])
    v = pcu[:, :d] * pcu[:, d:]
    v_ref[0:SUBLANES] = jnp.where(i > 0, v[:SUBLANES], 0.0)
    v_ref[SUBLANES:SUBLANES + tm] = v[SUBLANES:SUBLANES + tm]
    v_ref[SUBLANES + tm:] = jnp.where(i < last, v[SUBLANES + tm:], 0.0)
    cw = cw_ref[...]
    groups = _row_groups(tm)
    x1s = []
    for rows in groups:
        r0 = rows.start + SUBLANES
        n = rows.stop - rows.start
        conv = (v_ref[r0 - 1:r0 - 1 + n] * cw[0:1] + v_ref[r0:r0 + n] * cw[1:2] + v_ref[r0 + 1:r0 + 1 + n] * cw[2:3])
        gated = (pgb[rows] * conv).astype(BF16)
        x1s.append(x[rows] + m[2:3] * _dot(gated, wout_ref[...]))
    for rows, x2 in zip(groups, _ffn_tail(x1s, m, nffn_ref[...], wg_ref, wu_ref, wd_ref)):
        o_ref[0, rows, :] = _rms(x2) * fn_ref[...]


def _layer1(x, mod, nmix, win, cw, wout, nffn, wg, wu, wd, fn, tm):
    b, t, d = x.shape
    ffn = wg.shape[1]
    r8 = tm // SUBLANES
    nblk8 = t // SUBLANES
    return pl.pallas_call(
        _layer1_kernel,
        grid=(b, t // tm),
        in_specs=[
            pl.BlockSpec((1, tm, d), lambda bi, i: (bi, i, 0)),
            pl.BlockSpec((1, SUBLANES, d), lambda bi, i: (bi, jnp.maximum(i * r8 - 1, 0), 0)),
            pl.BlockSpec((1, SUBLANES, d), lambda bi, i: (bi, jnp.minimum((i + 1) * r8, nblk8 - 1), 0)),
            pl.BlockSpec((1, N_MOD, d), lambda bi, i: (bi, 0, 0)),
            _const_spec((1, d)),
            _const_spec((d, 3 * d)),
            _const_spec((3, d)),
            _const_spec((d, d)),
            _const_spec((1, d)),
            _const_spec((d, ffn)),
            _const_spec((d, ffn)),
            _const_spec((ffn, d)),
            _const_spec((1, d)),
        ],
        out_specs=pl.BlockSpec((1, tm, d), lambda bi, i: (bi, i, 0)),
        out_shape=jax.ShapeDtypeStruct((b, t, d), F32),
        scratch_shapes=[pltpu.VMEM((tm + 2 * SUBLANES, d), F32)],
        compiler_params=_cparams(2),
        name="layer1",
    )(x, x, x, mod, nmix, win, cw, wout, nffn, wg, wu, wd, fn)


def _rope_tables(n_tokens):
    rows = n_tokens // GRID_W
    row = jnp.repeat(jnp.arange(rows), GRID_W).astype(F32)
    col = jnp.tile(jnp.arange(GRID_W), rows).astype(F32)
    inv = 1.0 / (ROPE_THETA ** (jnp.arange(0, AXIS_DIM, 2, dtype=F32) / AXIS_DIM))
    ang = jnp.concatenate([row[:, None] * inv, col[:, None] * inv], axis=-1)
    return jnp.cos(ang), jnp.sin(ang)


_HEAD_PERM = np.concatenate([np.arange(0, HEAD_DIM, 2), np.arange(1, HEAD_DIM, 2)])


def _qk_tables(cos, sin, q_gain, k_gain):
    half = HEAD_DIM // 2
    sign = jnp.concatenate([-jnp.ones((half,), F32), jnp.ones((half,), F32)])
    c64 = jnp.concatenate([cos, cos], axis=1)
    s64 = jnp.concatenate([sin, sin], axis=1) * sign
    swap = np.concatenate([np.arange(half, HEAD_DIM), np.arange(0, half)])

    def tables(gain, scale):
        gp = gain[_HEAD_PERM] * scale
        return c64 * gp[None, :], s64 * gp[swap][None, :]

    aq, bq = tables(q_gain, HEAD_DIM ** -0.5 * LOG2_E)
    ak, bk = tables(k_gain, 1.0)
    return aq.T, bq.T, jnp.tile(ak, (1, ATTN_KV_HEADS)), jnp.tile(bk, (1, ATTN_KV_HEADS))


def kernel(x, c, ctx, c_ctx, ada_w, ada_b, norm_mix, norm_ffn, ffn_w_gate, ffn_w_up, ffn_w_down, hy_w_in,
           hy_q_norm, hy_k_norm, hy_conv_w, hy_conv_b, hy_dt_bias, hy_a_log, hy_d_skip, hy_ssm_norm, hy_w_out,
           sc_w_in, sc_conv_w, sc_w_out, final_norm):
    b, t, d = x.shape
    nctx = ctx.shape[1]
    tm_in = min(1024, t)
    tm_mix = min(512, t)

    n_rows = -(-(b + 1) // SUBLANES) * SUBLANES
    cc = jnp.zeros((n_rows, d), F32).at[:b].set(c).at[b].set(c_ctx)
    mods = _modulation(cc, ada_w, ada_b).reshape(ada_w.shape[0], n_rows, N_MOD, d)
    mod0, mod0_ctx, mod1 = mods[0, :b], mods[0, b:b + 1], mods[1, :b]

    w_in = hy_w_in[0]
    idx_k, idx_v, idx_z = Q_DIM, Q_DIM + KV_DIM, Q_DIM + 2 * KV_DIM
    idx_xbc = idx_z + SSM_D_INNER
    q_cols = (np.arange(ATTN_HEADS)[:, None] * HEAD_DIM + _HEAD_PERM[None, :]).reshape(-1)
    k_cols = idx_k + (np.arange(ATTN_KV_HEADS)[:, None] * HEAD_DIM + _HEAD_PERM[None, :]).reshape(-1)
    dt_pad = jnp.zeros((d, LANES - 2 * SSM_HEADS), F32)
    wkz = jnp.concatenate([w_in[:, k_cols], w_in[:, idx_z:idx_xbc]], axis=1).astype(BF16)
    wx = jnp.concatenate([w_in[:, idx_xbc:], dt_pad], axis=1).astype(BF16)
    wt = jnp.concatenate([w_in[:, q_cols], w_in[:, idx_v:idx_z]], axis=1).T.astype(BF16)
    nw0 = norm_mix[0][None, :]

    cos, sin = _rope_tables(t)
    aq, bq, ck, sk = _qk_tables(cos, sin, hy_q_norm[0], hy_k_norm[0])
    ones_c, zeros_c = jnp.ones((nctx, AXIS_DIM), F32), jnp.zeros((nctx, AXIS_DIM), F32)
    aq_c, bq_c, ck_c, sk_c = _qk_tables(ones_c, zeros_c, hy_q_norm[0], hy_k_norm[0])
    seg = np.arange(KV_DIM) // HEAD_DIM
    bd = jnp.asarray(seg[:, None] == seg[None, :], BF16)

    conv_w = jnp.concatenate([hy_conv_w[0], jnp.zeros((3, LANES), F32)], axis=1)
    conv_b = jnp.concatenate([hy_conv_b[0], jnp.zeros((LANES,), F32)])[None, :]
    dtb = jnp.concatenate([hy_dt_bias[0].reshape(-1), jnp.zeros((LANES - 2 * SSM_HEADS,), F32)])[None, :]
    shared = (nw0, wx, wkz, wt)
    qt, vt, k, z, *ssd_lat = _inproj(x, mod0, True, *shared, aq, bq, ck, sk, bd, conv_w, conv_b, dtb, tm_in)
    _, vct, kc, _, *ssd_ctx = _inproj(ctx, mod0_ctx, False, *shared, aq_c, bq_c, ck_c, sk_c, bd, conv_w, conv_b, dtb,
                                      nctx)

    k_all = jnp.concatenate([k, kc], axis=1)
    vt_all = jnp.concatenate([vt, vct], axis=2)
    kmax = 1.01 * HEAD_DIM ** 0.5 * jnp.max(jnp.abs(hy_k_norm[0]))
    qmax = 1.01 * LOG2_E * jnp.max(jnp.abs(hy_q_norm[0]))
    score_bound = qmax * kmax
    a_lat = lax.cond(score_bound <= SAFE_SCORE_BOUND,
                     functools.partial(_attention_bounded, tq=min(ATTN_TQ_BOUNDED, t)),
                     functools.partial(_attention_exact, tq=min(ATTN_TQ_EXACT, t)),
                     qt, k_all, vt_all, kmax)

    alog_col = jnp.broadcast_to(hy_a_log[0].reshape(-1, 1), (2 * SSM_HEADS, SSM_CHUNK))
    dsk_col = jnp.broadcast_to(jnp.repeat(hy_d_skip[0], SSM_HEAD_DIM)[:, None], (SSM_D_INNER, SSM_CHUNK))
    yf, yb = _ssd(ssd_lat, ssd_ctx, alog_col, dsk_col, toks=min(8 * SSM_CHUNK, t))

    w_out = hy_w_out[0].astype(BF16)
    x = _mix0(x, a_lat, yf, yb, z, mod0, hy_ssm_norm[0][None, :], norm_ffn[0][None, :],
              w_out[:Q_DIM], w_out[Q_DIM:], ffn_w_gate[0].astype(BF16), ffn_w_up[0].astype(BF16),
              ffn_w_down[0].astype(BF16), tm_mix)

    return _layer1(x, mod1, norm_mix[1][None, :], sc_w_in[0].astype(BF16), sc_conv_w[0], sc_w_out[0].astype(BF16),
                   norm_ffn[1][None, :], ffn_w_gate[1].astype(BF16), ffn_w_up[1].astype(BF16),
                   ffn_w_down[1].astype(BF16), final_norm[None, :], tm_mix)
```

```python
import functools

import jax
import jax.numpy as jnp
import numpy as np
from jax import lax
from jax.experimental import pallas as pl
from jax.experimental.pallas import tpu as pltpu

F32 = jnp.float32
BF16 = jnp.bfloat16

EPS = 1e-6
N_MOD = 6
GRID_W = 64
ROPE_THETA = 10000.0

ATTN_HEADS = 8
ATTN_KV_HEADS = 2
HEAD_DIM = 64
AXIS_DIM = HEAD_DIM // 2
KV_GROUP = ATTN_HEADS // ATTN_KV_HEADS
Q_DIM = ATTN_HEADS * HEAD_DIM
KV_DIM = ATTN_KV_HEADS * HEAD_DIM

SSM_HEADS = 8
SSM_HEAD_DIM = 64
SSM_D_INNER = SSM_HEADS * SSM_HEAD_DIM
SSM_GROUPS = 2
SSM_STATE = 64
SSM_CHUNK = 128
SSM_CONV_DIM = SSM_D_INNER + 2 * SSM_GROUPS * SSM_STATE
SSM_HEADS_PER_GROUP = SSM_HEADS // SSM_GROUPS

LANES = 128
SUBLANES = 8
XBCDT_W = SSM_CONV_DIM + LANES
NEG_BIG = -1e30
LOG2_E = 1.4426950408889634
ATTN_KEY_TILES = (768, 512, 256, 128)
ROW_GROUPS = 2
FFN_CHUNK = 768
ATTN_TQ_EXACT = 128
ATTN_TQ_BOUNDED = 128
ATTN_Q_BLOCKS = 8
ATTN_SUB_TILE = 256
ATTN_PV_LAG = 3
SAFE_SCORE_BOUND = 40.0
SCORE_BOUND_SLACK = 1.001

VMEM_LIMIT = 56 * 1024 * 1024


def _cparams(n_axes):
    return pltpu.CompilerParams(dimension_semantics=("parallel",) * n_axes, vmem_limit_bytes=VMEM_LIMIT)


def _const_spec(shape):
    nd = len(shape)
    return pl.BlockSpec(shape, lambda *_: (0,) * nd, pipeline_mode=pl.Buffered(1))


def _layer_spec(shape, layer):
    nd = len(shape)
    return pl.BlockSpec((None,) + tuple(shape), lambda *_: (layer,) + (0,) * nd, pipeline_mode=pl.Buffered(1))


def _rms(x):
    return x * lax.rsqrt(jnp.mean(x * x, axis=-1, keepdims=True) + EPS)


def _silu(x):
    return x * (1.0 / (1.0 + jnp.exp(-x)))


def _dot(a, b):
    return jnp.dot(a, b, preferred_element_type=F32)


def _dot_exact(a, b):
    return jnp.dot(a, b, preferred_element_type=F32, precision=lax.Precision.HIGHEST)


def _mod_kernel(c_ref, w_ref, b_ref, o_ref):
    c = c_ref[...]
    o_ref[0] = _dot_exact(_silu(c), w_ref[0]) + b_ref[0]


def _modulation(cc, ada_w, ada_b):
    depth, d, nd = ada_w.shape
    r = cc.shape[0]
    tn = d
    return pl.pallas_call(
        _mod_kernel,
        grid=(depth, nd // tn),
        in_specs=[
            pl.BlockSpec((r, d), lambda l, j: (0, 0)),
            pl.BlockSpec((1, d, tn), lambda l, j: (l, 0, j)),
            pl.BlockSpec((1, 1, tn), lambda l, j: (l, 0, j)),
        ],
        out_specs=pl.BlockSpec((1, r, tn), lambda l, j: (l, 0, j)),
        out_shape=jax.ShapeDtypeStruct((depth, r, nd), F32),
        compiler_params=_cparams(2),
        name="mod",
    )(cc, ada_w, ada_b.reshape(depth, 1, nd))


def _inproj_kernel(x_ref, xp_ref, xn_ref, mod_ref, nw_ref, wx_ref, wkz_ref, wt_ref, aq_ref, bq_ref, ck_ref, sk_ref,
                   bd_ref, cw_ref, cb_ref, dtb_ref,
                   qt_ref, vt_ref, k_ref, z_ref, xst_ref, bn_ref, ct_ref, dtt_ref, sc_ref):
    i = pl.program_id(1)
    last = pl.num_programs(1) - 1
    m = mod_ref[0]
    tm = x_ref.shape[1]
    half = HEAD_DIM // 2
    xa = jnp.concatenate([xp_ref[0], x_ref[0], xn_ref[0]], axis=0)
    h = _rms(xa) * (nw_ref[...] * (1.0 + m[1:2])) + m[0:1]
    px = _dot(h.astype(BF16), wx_ref[...])
    groups = [slice(0, tm)]
    proj = []
    for rows in groups:
        hc = h[SUBLANES + rows.start:SUBLANES + rows.stop].astype(BF16)
        pkz = _dot(hc, wkz_ref[...])
        pt = lax.dot_general(wt_ref[...], hc, (((1,), (1,)), ((), ())), preferred_element_type=F32)
        sq = pkz[:, :KV_DIM] * pkz[:, :KV_DIM]
        hi = sq.astype(BF16)
        lo = (sq - hi.astype(F32)).astype(BF16)
        proj.append((pkz, pt, _dot(hi, bd_ref[...]) + _dot(lo, bd_ref[...])))

    sc_ref[0:SUBLANES] = jnp.where(i > 0, px[:SUBLANES], 0.0)
    sc_ref[SUBLANES:SUBLANES + tm] = px[SUBLANES:SUBLANES + tm]
    sc_ref[SUBLANES + tm:] = jnp.where(i < last, px[SUBLANES + tm:], 0.0)
    cur = sc_ref[SUBLANES:SUBLANES + tm]
    w = cw_ref[...]
    y = (sc_ref[SUBLANES - 1:SUBLANES - 1 + tm] * w[0:1] + cur * w[1:2]
         + sc_ref[SUBLANES + 1:SUBLANES + 1 + tm] * w[2:3] + cb_ref[...])
    u = _silu(y[:, :SSM_CONV_DIM])
    bc = SSM_D_INNER + SSM_GROUPS * SSM_STATE
    xst_ref[0] = u[:, :SSM_D_INNER].T
    bn_ref[0] = u[:, SSM_D_INNER:bc].astype(BF16)
    ct_ref[0] = u[:, bc:].T.astype(BF16)
    d = cur[:, SSM_CONV_DIM:] + dtb_ref[...]
    dt = jnp.maximum(d, 0.0) + jnp.log1p(jnp.exp(-jnp.abs(d)))
    dtt_ref[0] = dt.T[:2 * SSM_HEADS]

    for rows, (pkz, pt, ss) in zip(groups, proj):
        n = rows.stop - rows.start
        kx = pkz[:, :KV_DIM]
        rs = lax.rsqrt(ss * (1.0 / HEAD_DIM) + EPS)
        lane = lax.broadcasted_iota(jnp.int32, kx.shape, 1)
        swap = jnp.where((lane & half) == 0, pltpu.roll(kx, KV_DIM - half, 1), pltpu.roll(kx, half, 1))
        k_ref[0, rows, :] = (rs * (kx * ck_ref[rows, :] + swap * sk_ref[rows, :])).astype(BF16)

        q3 = pt[:Q_DIM].reshape(ATTN_HEADS, HEAD_DIM, n)
        qrs = lax.rsqrt(jnp.mean(q3 * q3, axis=1, keepdims=True) + EPS)
        qsw = jnp.concatenate([q3[:, half:], q3[:, :half]], axis=1)
        qo = qrs * (q3 * aq_ref[:, rows][None] + qsw * bq_ref[:, rows][None])
        qt_ref[0, :, rows] = qo.reshape(Q_DIM, n).astype(BF16)

        vt_ref[0, :, rows] = pt[Q_DIM:].astype(BF16)
        z_ref[0, rows, :] = pkz[:, KV_DIM:]


def _inproj(x, mod, per_batch_mod, nw, wx, wkz, wt, aq, bq, ck, sk, bd, cw, cb, dtb, tm):
    b, t, d = x.shape
    r8 = tm // SUBLANES
    nblk8 = t // SUBLANES
    gn = SSM_GROUPS * SSM_STATE
    mod_idx = (lambda bi, i: (bi, 0, 0)) if per_batch_mod else (lambda bi, i: (0, 0, 0))
    row = lambda bi, i: (bi, i, 0)
    col = lambda bi, i: (bi, 0, i)
    return pl.pallas_call(
        _inproj_kernel,
        grid=(b, t // tm),
        in_specs=[
            pl.BlockSpec((1, tm, d), row),
            pl.BlockSpec((1, SUBLANES, d), lambda bi, i: (bi, jnp.maximum(i * r8 - 1, 0), 0)),
            pl.BlockSpec((1, SUBLANES, d), lambda bi, i: (bi, jnp.minimum((i + 1) * r8, nblk8 - 1), 0)),
            pl.BlockSpec((1, N_MOD, d), mod_idx),
            _const_spec((1, d)),
            _const_spec((d, XBCDT_W)),
            _const_spec((d, KV_DIM + SSM_D_INNER)),
            _const_spec((Q_DIM + KV_DIM, d)),
            pl.BlockSpec((HEAD_DIM, tm), lambda bi, i: (0, i)),
            pl.BlockSpec((HEAD_DIM, tm), lambda bi, i: (0, i)),
            pl.BlockSpec((tm, KV_DIM), lambda bi, i: (i, 0)),
            pl.BlockSpec((tm, KV_DIM), lambda bi, i: (i, 0)),
            _const_spec((KV_DIM, KV_DIM)),
            _const_spec((3, XBCDT_W)),
            _const_spec((1, XBCDT_W)),
            _const_spec((1, LANES)),
        ],
        out_specs=[
            pl.BlockSpec((1, Q_DIM, tm), col),
            pl.BlockSpec((1, KV_DIM, tm), col),
            pl.BlockSpec((1, tm, KV_DIM), row),
            pl.BlockSpec((1, tm, SSM_D_INNER), row),
            pl.BlockSpec((1, SSM_D_INNER, tm), col),
            pl.BlockSpec((1, tm, gn), row),
            pl.BlockSpec((1, gn, tm), col),
            pl.BlockSpec((1, 2 * SSM_HEADS, tm), col),
        ],
        out_shape=[
            jax.ShapeDtypeStruct((b, Q_DIM, t), BF16),
            jax.ShapeDtypeStruct((b, KV_DIM, t), BF16),
            jax.ShapeDtypeStruct((b, t, KV_DIM), BF16),
            jax.ShapeDtypeStruct((b, t, SSM_D_INNER), F32),
            jax.ShapeDtypeStruct((b, SSM_D_INNER, t), F32),
            jax.ShapeDtypeStruct((b, t, gn), BF16),
            jax.ShapeDtypeStruct((b, gn, t), BF16),
            jax.ShapeDtypeStruct((b, 2 * SSM_HEADS, t), F32),
        ],
        scratch_shapes=[pltpu.VMEM((tm + 2 * SUBLANES, XBCDT_W), F32)],
        compiler_params=_cparams(2),
        name="inproj",
    )(x, x, x, mod, nw, wx, wkz, wt, aq, bq, ck, sk, bd, cw, cb, dtb)


def _attn_kernel(qt_ref, k_ref, vt_ref, o_ref, s0_ref, s1_ref, p0_ref, p1_ref, m0_ref, m1_ref, *, nq):
    g = pl.program_id(1)
    i = pl.program_id(2)
    tq = qt_ref.shape[2]
    cols = KV_GROUP * tq
    nk, kt = s0_ref.shape[0], s0_ref.shape[1]
    s_refs, m_refs, p_refs = (s0_ref, s1_ref), (m0_ref, m1_ref), (p0_ref, p1_ref)

    def run(a_buf, b_buf):
        do_a, do_bc = a_buf is not None, b_buf is not None
        if do_a:
            q4 = qt_ref[0]
            qcat = jnp.concatenate([q4[h * HEAD_DIM:(h + 1) * HEAD_DIM] for h in range(KV_GROUP)], axis=1)
            qp = jnp.concatenate([qcat] * ATTN_KV_HEADS, axis=0)
            rowgrp = lax.broadcasted_iota(jnp.int32, qp.shape, 0) // HEAD_DIM
            qp = jnp.where(rowgrp == g, qp, jnp.zeros_like(qp))
        if do_bc:
            m8 = jnp.broadcast_to(jnp.max(m_refs[b_buf][...], axis=0, keepdims=True), (SUBLANES, cols))

        def stage_a(j, mx):
            ktile = k_ref[0, pl.ds(pl.multiple_of(j * kt, kt), kt), :]
            s = _dot(ktile, qp)
            s_refs[a_buf][j] = s
            return jnp.maximum(mx, jnp.max(s.reshape(kt // SUBLANES, SUBLANES, cols), axis=0))

        def stage_b(j, par, ls):
            s = s_refs[b_buf][j].reshape(kt // SUBLANES, SUBLANES, cols)
            p = jnp.exp2(s - m8[None])
            p_refs[par][...] = p.reshape(kt, cols).astype(BF16)
            return ls + jnp.sum(p, axis=0)

        def stage_c(j, par, acc):
            return acc + _dot(vt_ref[0, 0, j], p_refs[par][...])

        def tile(j, par, carry):
            mx, ls, acc = carry
            if do_a:
                mx = stage_a(j, mx)
            if do_bc:
                acc = stage_c(j - 1, 1 - par, acc)
                ls = stage_b(j, par, ls)
            return mx, ls, acc

        mx = jnp.full((SUBLANES, cols), NEG_BIG, F32)
        ls = jnp.zeros((SUBLANES, cols), F32)
        acc = jnp.zeros((HEAD_DIM, cols), F32)
        if do_a:
            mx = stage_a(0, mx)
        if do_bc:
            ls = stage_b(0, 0, ls)

        def body(t, carry):
            j = 1 + 2 * t
            return tile(j + 1, 0, tile(j, 1, carry))

        carry = lax.fori_loop(0, (nk - 1) // 2, body, (mx, ls, acc), unroll=True)
        if (nk - 1) % 2:
            carry = tile(nk - 1, (nk - 1) % 2, carry)
        mx, ls, acc = carry
        if do_a:
            m_refs[a_buf][...] = mx
        if do_bc:
            acc = stage_c(nk - 1, (nk - 1) % 2, acc)
            o = acc * (1.0 / jnp.sum(ls, axis=0, keepdims=True))
            o4 = jnp.concatenate([o[:, h * tq:(h + 1) * tq] for h in range(KV_GROUP)], axis=0)
            o_ref[0] = o4.T.astype(BF16)

    @pl.when(i == 0)
    def _():
        run(0, None)

    inner = jnp.logical_and(i > 0, i < nq)

    @pl.when(jnp.logical_and(inner, i % 2 == 1))
    def _():
        run(1, 0)

    @pl.when(jnp.logical_and(inner, i % 2 == 0))
    def _():
        run(0, 1)

    @pl.when(i == nq)
    def _():
        run(None, (nq - 1) % 2)


def _attention(qt, k_all, vt_tiles, tq):
    b, _, t = qt.shape
    tk = k_all.shape[1]
    nk, kt = vt_tiles.shape[2], vt_tiles.shape[4]
    nq = t // tq
    gw = KV_GROUP * HEAD_DIM
    cols = KV_GROUP * tq
    return pl.pallas_call(
        functools.partial(_attn_kernel, nq=nq),
        grid=(b, ATTN_KV_HEADS, nq + 1),
        in_specs=[
            pl.BlockSpec((1, gw, tq), lambda bi, g, i: (bi, g, jnp.minimum(i, nq - 1))),
            pl.BlockSpec((1, tk, KV_DIM), lambda bi, g, i: (bi, 0, 0)),
            pl.BlockSpec((1, 1, nk, HEAD_DIM, kt), lambda bi, g, i: (bi, g, 0, 0, 0)),
        ],
        out_specs=pl.BlockSpec((1, tq, gw), lambda bi, g, i: (bi, jnp.maximum(i - 1, 0), g)),
        out_shape=jax.ShapeDtypeStruct((b, t, Q_DIM), BF16),
        scratch_shapes=(
            [pltpu.VMEM((nk, kt, cols), F32)] * 2
            + [pltpu.VMEM((kt, cols), BF16)] * 2
            + [pltpu.VMEM((SUBLANES, cols), F32)] * 2
        ),
        compiler_params=pltpu.CompilerParams(dimension_semantics=("parallel", "parallel", "arbitrary"),
                                             vmem_limit_bytes=VMEM_LIMIT),
        name="attn",
    )(qt, k_all, vt_tiles)


def _attn_bounded_kernel(qt_ref, k_ref, vt_ref, km_ref, o_ref, *p_refs):
    g = pl.program_id(1)
    cols = p_refs[0].shape[1]
    tq = cols // KV_GROUP
    n_blk = qt_ref.shape[2] // tq
    sub = p_refs[0].shape[0]
    n_all = k_ref.shape[1] // sub

    kmax = jnp.concatenate([km_ref[0:1, :]] * (cols // LANES), axis=1)
    qps, mb8s = [], []
    for blk in range(n_blk):
        q4 = qt_ref[0, :, blk * tq:(blk + 1) * tq]
        qf = q4.astype(F32)
        qn = jnp.sqrt(jnp.sum((qf * qf).reshape(KV_GROUP, HEAD_DIM, tq), axis=1))
        qn = jnp.concatenate([qn[h:h + 1] for h in range(KV_GROUP)], axis=1)
        mb8s.append(jnp.broadcast_to(qn * kmax * SCORE_BOUND_SLACK, (SUBLANES, cols)))
        qcat = jnp.concatenate([q4[h * HEAD_DIM:(h + 1) * HEAD_DIM] for h in range(KV_GROUP)], axis=1)
        qp = jnp.concatenate([qcat] * ATTN_KV_HEADS, axis=0)
        rowgrp = lax.broadcasted_iota(jnp.int32, qp.shape, 0) // HEAD_DIM
        qps.append(jnp.where(rowgrp == g, qp, jnp.zeros_like(qp)))

    ring = len(p_refs)
    lag = ring - 1
    ls = [jnp.zeros((SUBLANES, cols), F32) for _ in range(n_blk)]
    acc = [jnp.zeros((HEAD_DIM, cols), F32) for _ in range(n_blk)]

    def pv(w):
        blk, u = divmod(w, n_all)
        acc[blk] = acc[blk] + _dot(vt_ref[0, :, u * sub:(u + 1) * sub], p_refs[w % ring][...])
        if u == n_all - 1:
            o = acc[blk] * (1.0 / jnp.sum(ls[blk], axis=0, keepdims=True))
            o4 = jnp.concatenate([o[:, h * tq:(h + 1) * tq] for h in range(KV_GROUP)], axis=0)
            o_ref[0, blk * tq:(blk + 1) * tq, :] = o4.T.astype(BF16)

    n_items = n_blk * n_all
    for w in range(n_items):
        blk, u = divmod(w, n_all)
        s = _dot(k_ref[0, u * sub:(u + 1) * sub, :], qps[blk])
        if w >= lag:
            pv(w - lag)
        p = jnp.exp2(s.reshape(sub // SUBLANES, SUBLANES, cols) - mb8s[blk][None])
        p_refs[w % ring][...] = p.reshape(sub, cols).astype(BF16)
        ls[blk] = ls[blk] + jnp.sum(p, axis=0)
    for w in range(max(n_items - lag, 0), n_items):
        pv(w)


def _attention_bounded(qt, k_all, vt_all, kmax, tq):
    b, _, t = qt.shape
    tk = k_all.shape[1]
    sub = next(c for c in (ATTN_SUB_TILE, LANES) if tk % c == 0)
    gw = KV_GROUP * HEAD_DIM
    cols = KV_GROUP * tq
    tstep = tq * (ATTN_Q_BLOCKS if t % (tq * ATTN_Q_BLOCKS) == 0 else 1)
    return pl.pallas_call(
        _attn_bounded_kernel,
        grid=(b, ATTN_KV_HEADS, t // tstep),
        in_specs=[
            pl.BlockSpec((1, gw, tstep), lambda bi, g, i: (bi, g, i)),
            pl.BlockSpec((1, tk, KV_DIM), lambda bi, g, i: (bi, 0, 0)),
            pl.BlockSpec((1, HEAD_DIM, tk), lambda bi, g, i: (bi, g, 0)),
            _const_spec((SUBLANES, LANES)),
        ],
        out_specs=pl.BlockSpec((1, tstep, gw), lambda bi, g, i: (bi, i, g)),
        out_shape=jax.ShapeDtypeStruct((b, t, Q_DIM), BF16),
        scratch_shapes=[pltpu.VMEM((sub, cols), BF16)] * (ATTN_PV_LAG + 1),
        compiler_params=_cparams(3),
        name="attn_bounded",
    )(qt, k_all, vt_all, jnp.full((SUBLANES, LANES), kmax, F32))


def _attention_exact(qt, k_all, vt_all, kmax, tq):
    del kmax
    b, _, tk = vt_all.shape
    kt = next(c for c in ATTN_KEY_TILES if tk % c == 0)
    vt_tiles = vt_all.reshape(b, ATTN_KV_HEADS, HEAD_DIM, tk // kt, kt).transpose(0, 1, 3, 2, 4)
    return _attention(qt, k_all, vt_tiles, tq)


class _SsdChunk:
    def __init__(self, xst, bn, ct, dtt, a_col, direction):
        self.xst, self.bn, self.ct, self.dtt, self.a_col, self.direction = xst, bn, ct, dtt, a_col, direction

    def decay(self):
        ln = self.xst.shape[1]
        da = self.dtt * self.a_col
        ri = lax.broadcasted_iota(jnp.int32, (ln, ln), 0)
        ci = lax.broadcasted_iota(jnp.int32, (ln, ln), 1)
        self.reach = (ci >= ri) if self.direction == 0 else (ci <= ri)
        self.cs_row = _dot_exact(da, self.reach.astype(F32))
        self.tot = _dot_exact(da, jnp.ones((ln, ln), F32))

    def local(self):
        ln = self.xst.shape[1]
        gn = SSM_GROUPS * SSM_STATE
        cs_col = jnp.concatenate([self.cs_row, jnp.zeros((ln - 2 * SSM_HEADS, ln), F32)], axis=0).T
        rowgrp = lax.broadcasted_iota(jnp.int32, (gn, ln), 0) // SSM_STATE
        self.ctz, self.y_diag, self.xw, self.e_row, self.e_tot = [], [], [], [], []
        for grp in range(SSM_GROUPS):
            ctz = jnp.where(rowgrp == grp, self.ct, jnp.zeros_like(self.ct))
            cbt = _dot(self.bn, ctz)
            y_parts, xw_parts = [], []
            for e in range(SSM_HEADS_PER_GROUP):
                hd = grp * SSM_HEADS_PER_GROUP + e
                c = self.direction * SSM_HEADS + hd
                row = self.cs_row[c:c + 1, :]
                lmt = jnp.exp(jnp.where(self.reach, row - cs_col[:, c:c + 1], NEG_BIG))
                mt = (cbt * lmt).astype(BF16)
                xdt = self.xst[hd * SSM_HEAD_DIM:(hd + 1) * SSM_HEAD_DIM, :] * self.dtt[c:c + 1, :]
                y_parts.append(_dot(xdt.astype(BF16), mt))
                xw_parts.append((xdt * jnp.exp(self.tot[c:c + 1, :] - row)).astype(BF16))
                self.e_row.append(jnp.exp(row))
                self.e_tot.append(jnp.exp(self.tot[c:c + 1, :gn]))
            self.ctz.append(ctz)
            self.y_diag.append(y_parts)
            self.xw.append(jnp.concatenate(xw_parts, axis=0))

    def carry(self, h_ref):
        gp = SSM_HEADS_PER_GROUP * SSM_HEAD_DIM
        ys = []
        for grp in range(SSM_GROUPS):
            hg = h_ref[grp * gp:(grp + 1) * gp, :]
            y_off = _dot(hg.astype(BF16), self.ctz[grp])
            st = _dot(self.xw[grp], self.bn)
            h_parts = []
            for e in range(SSM_HEADS_PER_GROUP):
                hd = grp * SSM_HEADS_PER_GROUP + e
                rows = slice(e * SSM_HEAD_DIM, (e + 1) * SSM_HEAD_DIM)
                ys.append(self.y_diag[grp][e] + y_off[rows] * self.e_row[hd])
                h_parts.append(hg[rows] * self.e_tot[hd])
            h_ref[grp * gp:(grp + 1) * gp, :] = jnp.concatenate(h_parts, axis=0) + st
        return jnp.concatenate(ys, axis=0)


def _ssd_run(chunks, h_refs):
    for ch in chunks:
        ch.decay()
    for ch in chunks:
        ch.local()
    return [ch.carry(h_refs[ch.direction]) for ch in chunks]


def _ssd_kernel(xf_ref, xb_ref, bf_ref, bb_ref, cf_ref, cb_ref, df_ref, db_ref, xc_ref, bc_ref, cc_ref, dc_ref,
                alog_ref, dsk_ref, yf_ref, yb_ref, hf_ref, hb_ref):
    c = pl.program_id(1)
    a_col = -jnp.exp(alog_ref[...])
    ln = SSM_CHUNK
    h_refs = (hf_ref, hb_ref)

    def chunk(x_ref, b_ref, c_ref, d_ref, s, direction):
        tok = slice(s * ln, (s + 1) * ln)
        return _SsdChunk(x_ref[0, :, tok], b_ref[0, tok, :], c_ref[0, :, tok], d_ref[0, :, tok], a_col, direction)

    @pl.when(c == 0)
    def _():
        n_ctx = xc_ref.shape[2] // ln
        for direction in range(2):
            h_refs[direction][...] = jnp.zeros_like(h_refs[direction])
        chunks = []
        for k in range(n_ctx):
            chunks.append(chunk(xc_ref, bc_ref, cc_ref, dc_ref, k, 0))
            chunks.append(chunk(xc_ref, bc_ref, cc_ref, dc_ref, n_ctx - 1 - k, 1))
        _ssd_run(chunks, h_refs)

    n_sub = xf_ref.shape[2] // ln
    chunks = []
    for k in range(n_sub):
        chunks.append(chunk(xf_ref, bf_ref, cf_ref, df_ref, k, 0))
        chunks.append(chunk(xb_ref, bb_ref, cb_ref, db_ref, n_sub - 1 - k, 1))
    ys = _ssd_run(chunks, h_refs)
    for k in range(n_sub):
        tok = slice(k * ln, (k + 1) * ln)
        yf_ref[0, :, tok] = ys[2 * k] + xf_ref[0, :, tok] * dsk_ref[...]
        tok = slice((n_sub - 1 - k) * ln, (n_sub - k) * ln)
        yb_ref[0, :, tok] = ys[2 * k + 1]


def _ssd(lat, ctx, alog_col, dsk_col, toks):
    xst, bn, ct, dtt = lat
    b, di, t = xst.shape
    gn = bn.shape[2]
    nh2 = dtt.shape[1]
    nctx = ctx[0].shape[2]
    ns = t // toks
    fcol = lambda bi, c: (bi, 0, c)
    bcol = lambda bi, c: (bi, 0, ns - 1 - c)
    frow = lambda bi, c: (bi, c, 0)
    brow = lambda bi, c: (bi, ns - 1 - c, 0)
    whole = lambda bi, c: (bi, 0, 0)
    return pl.pallas_call(
        _ssd_kernel,
        grid=(b, ns),
        in_specs=[
            pl.BlockSpec((1, di, toks), fcol), pl.BlockSpec((1, di, toks), bcol),
            pl.BlockSpec((1, toks, gn), frow), pl.BlockSpec((1, toks, gn), brow),
            pl.BlockSpec((1, gn, toks), fcol), pl.BlockSpec((1, gn, toks), bcol),
            pl.BlockSpec((1, nh2, toks), fcol), pl.BlockSpec((1, nh2, toks), bcol),
            pl.BlockSpec((1, di, nctx), whole), pl.BlockSpec((1, nctx, gn), whole),
            pl.BlockSpec((1, gn, nctx), whole), pl.BlockSpec((1, nh2, nctx), whole),
            _const_spec((nh2, SSM_CHUNK)),
            _const_spec((di, SSM_CHUNK)),
        ],
        out_specs=[pl.BlockSpec((1, di, toks), fcol), pl.BlockSpec((1, di, toks), bcol)],
        out_shape=[jax.ShapeDtypeStruct((b, di, t), F32), jax.ShapeDtypeStruct((b, di, t), F32)],
        scratch_shapes=[pltpu.VMEM((di, gn), F32)] * 2,
        compiler_params=pltpu.CompilerParams(dimension_semantics=("parallel", "arbitrary"),
                                             vmem_limit_bytes=VMEM_LIMIT),
        name="ssd",
    )(xst, xst, bn, bn, ct, ct, dtt, dtt, *ctx, alog_col, dsk_col)


def _row_groups(tm):
    n = ROW_GROUPS if tm % (ROW_GROUPS * SUBLANES * 2) == 0 else 1
    return [slice(i * (tm // n), (i + 1) * (tm // n)) for i in range(n)]


def _ffn_tail(x1s, m, nw, wg_ref, wu_ref, wd_ref):
    scale = nw * (1.0 + m[4:5])
    hs = [(_rms(x1) * scale + m[3:4]).astype(BF16) for x1 in x1s]
    ffn = wg_ref.shape[1]
    fs = [None] * len(x1s)
    for c0 in range(0, ffn, FFN_CHUNK):
        c1 = min(c0 + FFN_CHUNK, ffn)
        hids = [(_silu(_dot(h, wg_ref[:, c0:c1])) * _dot(h, wu_ref[:, c0:c1])).astype(BF16) for h in hs]
        for i, hid in enumerate(hids):
            part = _dot(hid, wd_ref[c0:c1, :])
            fs[i] = part if fs[i] is None else fs[i] + part
    return [x1 + m[5:6] * f for x1, f in zip(x1s, fs)]


def _mix0_kernel(x_ref, a_ref, yf_ref, yb_ref, z_ref, mod_ref, snw_ref, nw_ref,
                 wo_ref, wg_ref, wu_ref, wd_ref, o_ref):
    m = mod_ref[0]
    gw = SSM_D_INNER // SSM_GROUPS
    groups = _row_groups(x_ref.shape[1])
    yns = []
    for rows in groups:
        y = (yf_ref[0, :, rows] + yb_ref[0, :, rows]).T
        gy = y * _silu(z_ref[0, rows, :])
        gn = jnp.concatenate([_rms(gy[:, i * gw:(i + 1) * gw]) for i in range(SSM_GROUPS)], axis=1)
        yns.append((gn * snw_ref[...]).astype(BF16))
    mix_a = [_dot(a_ref[0, rows, :], wo_ref[:Q_DIM, :]) for rows in groups]
    x1s = []
    for rows, yn, ma in zip(groups, yns, mix_a):
        x1s.append(x_ref[0, rows, :] + m[2:3] * (ma + _dot(yn, wo_ref[Q_DIM:, :])))
    for rows, out in zip(groups, _ffn_tail(x1s, m, nw_ref[...], wg_ref, wu_ref, wd_ref)):
        o_ref[0, rows, :] = out


def _mix0(x, a, yf, yb, z, mod, snw, nw, wo, wg, wu, wd, tm):
    b, t, d = x.shape
    ffn = wg.shape[2]
    row = lambda bi, i: (bi, i, 0)
    col = lambda bi, i: (bi, 0, i)
    return pl.pallas_call(
        _mix0_kernel,
        grid=(b, t // tm),
        in_specs=[
            pl.BlockSpec((1, tm, d), row),
            pl.BlockSpec((1, tm, Q_DIM), row),
            pl.BlockSpec((1, SSM_D_INNER, tm), col),
            pl.BlockSpec((1, SSM_D_INNER, tm), col),
            pl.BlockSpec((1, tm, SSM_D_INNER), row),
            pl.BlockSpec((1, N_MOD, d), lambda bi, i: (bi, 0, 0)),
            _const_spec((1, SSM_D_INNER)),
            _const_spec((1, d)),
            _const_spec((Q_DIM + SSM_D_INNER, d)),
            _layer_spec((d, ffn), 0),
            _layer_spec((d, ffn), 0),
            _layer_spec((ffn, d), 0),
        ],
        out_specs=pl.BlockSpec((1, tm, d), row),
        out_shape=jax.ShapeDtypeStruct((b, t, d), F32),
        compiler_params=_cparams(2),
        name="mix0",
    )(x, a, yf, yb, z, mod, snw, nw, wo, wg, wu, wd)


def _layer1_kernel(x_ref, xp_ref, xn_ref, mod_ref, nmix_ref, win_ref, cw_ref, wout_ref, nffn_ref,
                   wg_ref, wu_ref, wd_ref, fn_ref, o_ref, v_ref):
    i = pl.program_id(1)
    last = pl.num_programs(1) - 1
    m = mod_ref[0]
    tm, d = x_ref.shape[1], x_ref.shape[2]
    x = x_ref[0]
    xa = jnp.concatenate([xp_ref[0], x, xn_ref[0]], axis=0)
    h = _rms(xa) * (nmix_ref[...] * (1.0 + m[1:2])) + m[0:1]
    pcu = _dot(h.astype(BF16), win_ref[:, d:])
    pgb = _dot(h[SUBLANES:SUBLANES + tm].astype(BF16), win_ref[:, :d])
    v = pcu[:, :d] * pcu[:, d:]
    v_ref[0:SUBLANES] = jnp.where(i > 0, v[:SUBLANES], 0.0)
    v_ref[SUBLANES:SUBLANES + tm] = v[SUBLANES:SUBLANES + tm]
    v_ref[SUBLANES + tm:] = jnp.where(i < last, v[SUBLANES + tm:], 0.0)
    cw = cw_ref[...]
    groups = _row_groups(tm)
    x1s = []
    for rows in groups:
        r0 = rows.start + SUBLANES
        n = rows.stop - rows.start
        conv = (v_ref[r0 - 1:r0 - 1 + n] * cw[0:1] + v_ref[r0:r0 + n] * cw[1:2] + v_ref[r0 + 1:r0 + 1 + n] * cw[2:3])
        gated = (pgb[rows] * conv).astype(BF16)
        x1s.append(x[rows] + m[2:3] * _dot(gated, wout_ref[...]))
    for rows, x2 in zip(groups, _ffn_tail(x1s, m, nffn_ref[...], wg_ref, wu_ref, wd_ref)):
        o_ref[0, rows, :] = _rms(x2) * fn_ref[...]


def _layer1(x, mod, nmix, win, cw, wout, nffn, wg, wu, wd, fn, tm):
    b, t, d = x.shape
    ffn = wg.shape[2]
    r8 = tm // SUBLANES
    nblk8 = t // SUBLANES
    return pl.pallas_call(
        _layer1_kernel,
        grid=(b, t // tm),
        in_specs=[
            pl.BlockSpec((1, tm, d), lambda bi, i: (bi, i, 0)),
            pl.BlockSpec((1, SUBLANES, d), lambda bi, i: (bi, jnp.maximum(i * r8 - 1, 0), 0)),
            pl.BlockSpec((1, SUBLANES, d), lambda bi, i: (bi, jnp.minimum((i + 1) * r8, nblk8 - 1), 0)),
            pl.BlockSpec((1, N_MOD, d), lambda bi, i: (bi, 0, 0)),
            _const_spec((1, d)),
            _const_spec((d, 3 * d)),
            _const_spec((3, d)),
            _const_spec((d, d)),
            _const_spec((1, d)),
            _layer_spec((d, ffn), 1),
            _layer_spec((d, ffn), 1),
            _layer_spec((ffn, d), 1),
            _const_spec((1, d)),
        ],
        out_specs=pl.BlockSpec((1, tm, d), lambda bi, i: (bi, i, 0)),
        out_shape=jax.ShapeDtypeStruct((b, t, d), F32),
        scratch_shapes=[pltpu.VMEM((tm + 2 * SUBLANES, d), F32)],
        compiler_params=_cparams(2),
        name="layer1",
    )(x, x, x, mod, nmix, win, cw, wout, nffn, wg, wu, wd, fn)


def _rope_tables(n_tokens):
    rows = n_tokens // GRID_W
    row = jnp.repeat(jnp.arange(rows), GRID_W).astype(F32)
    col = jnp.tile(jnp.arange(GRID_W), rows).astype(F32)
    inv = 1.0 / (ROPE_THETA ** (jnp.arange(0, AXIS_DIM, 2, dtype=F32) / AXIS_DIM))
    ang = jnp.concatenate([row[:, None] * inv, col[:, None] * inv], axis=-1)
    return jnp.cos(ang), jnp.sin(ang)


_HEAD_PERM = np.concatenate([np.arange(0, HEAD_DIM, 2), np.arange(1, HEAD_DIM, 2)])


def _qk_tables(cos, sin, q_gain, k_gain):
    half = HEAD_DIM // 2
    sign = jnp.concatenate([-jnp.ones((half,), F32), jnp.ones((half,), F32)])
    c64 = jnp.concatenate([cos, cos], axis=1)
    s64 = jnp.concatenate([sin, sin], axis=1) * sign
    swap = np.concatenate([np.arange(half, HEAD_DIM), np.arange(0, half)])

    def tables(gain, scale):
        gp = gain[_HEAD_PERM] * scale
        return c64 * gp[None, :], s64 * gp[swap][None, :]

    aq, bq = tables(q_gain, HEAD_DIM ** -0.5 * LOG2_E)
    ak, bk = tables(k_gain, 1.0)
    return aq.T, bq.T, jnp.tile(ak, (1, ATTN_KV_HEADS)), jnp.tile(bk, (1, ATTN_KV_HEADS))


def kernel(x, c, ctx, c_ctx, ada_w, ada_b, norm_mix, norm_ffn, ffn_w_gate, ffn_w_up, ffn_w_down, hy_w_in,
           hy_q_norm, hy_k_norm, hy_conv_w, hy_conv_b, hy_dt_bias, hy_a_log, hy_d_skip, hy_ssm_norm, hy_w_out,
           sc_w_in, sc_conv_w, sc_w_out, final_norm):
    b, t, d = x.shape
    nctx = ctx.shape[1]
    tm_in = min(1024, t)
    tm_mix = min(512, t)

    n_rows = -(-(b + 1) // SUBLANES) * SUBLANES
    cc = jnp.zeros((n_rows, d), F32).at[:b].set(c).at[b].set(c_ctx)
    mods = _modulation(cc, ada_w, ada_b).reshape(ada_w.shape[0], n_rows, N_MOD, d)
    mod0, mod0_ctx, mod1 = mods[0, :b], mods[0, b:b + 1], mods[1, :b]

    w_in = hy_w_in[0]
    idx_k, idx_v, idx_z = Q_DIM, Q_DIM + KV_DIM, Q_DIM + 2 * KV_DIM
    idx_xbc = idx_z + SSM_D_INNER
    q_cols = (np.arange(ATTN_HEADS)[:, None] * HEAD_DIM + _HEAD_PERM[None, :]).reshape(-1)
    k_cols = idx_k + (np.arange(ATTN_KV_HEADS)[:, None] * HEAD_DIM + _HEAD_PERM[None, :]).reshape(-1)
    dt_pad = jnp.zeros((d, LANES - 2 * SSM_HEADS), F32)
    wkz = jnp.concatenate([w_in[:, k_cols], w_in[:, idx_z:idx_xbc]], axis=1).astype(BF16)
    wx = jnp.concatenate([w_in[:, idx_xbc:], dt_pad], axis=1).astype(BF16)
    wt = jnp.concatenate([w_in[:, q_cols], w_in[:, idx_v:idx_z]], axis=1).T.astype(BF16)
    nw0 = norm_mix[0][None, :]

    cos, sin = _rope_tables(t)
    aq, bq, ck, sk = _qk_tables(cos, sin, hy_q_norm[0], hy_k_norm[0])
    ones_c, zeros_c = jnp.ones((nctx, AXIS_DIM), F32), jnp.zeros((nctx, AXIS_DIM), F32)
    aq_c, bq_c, ck_c, sk_c = _qk_tables(ones_c, zeros_c, hy_q_norm[0], hy_k_norm[0])
    seg = np.arange(KV_DIM) // HEAD_DIM
    bd = jnp.asarray(seg[:, None] == seg[None, :], BF16)

    conv_w = jnp.concatenate([hy_conv_w[0], jnp.zeros((3, LANES), F32)], axis=1)
    conv_b = jnp.concatenate([hy_conv_b[0], jnp.zeros((LANES,), F32)])[None, :]
    dtb = jnp.concatenate([hy_dt_bias[0].reshape(-1), jnp.zeros((LANES - 2 * SSM_HEADS,), F32)])[None, :]
    shared = (nw0, wx, wkz, wt)
    qt, vt, k, z, *ssd_lat = _inproj(x, mod0, True, *shared, aq, bq, ck, sk, bd, conv_w, conv_b, dtb, tm_in)
    _, vct, kc, _, *ssd_ctx = _inproj(ctx, mod0_ctx, False, *shared, aq_c, bq_c, ck_c, sk_c, bd, conv_w, conv_b, dtb,
                                      nctx)

    k_all = jnp.concatenate([k, kc], axis=1)
    vt_all = jnp.concatenate([vt, vct], axis=2)
    kmax = 1.01 * HEAD_DIM ** 0.5 * jnp.max(jnp.abs(hy_k_norm[0]))
    qmax = 1.01 * LOG2_E * jnp.max(jnp.abs(hy_q_norm[0]))
    score_bound = qmax * kmax
    a_lat = lax.cond(score_bound <= SAFE_SCORE_BOUND,
                     functools.partial(_attention_bounded, tq=min(ATTN_TQ_BOUNDED, t)),
                     functools.partial(_attention_exact, tq=min(ATTN_TQ_EXACT, t)),
                     qt, k_all, vt_all, kmax)

    alog_col = jnp.broadcast_to(hy_a_log[0].reshape(-1, 1), (2 * SSM_HEADS, SSM_CHUNK))
    dsk_col = jnp.broadcast_to(jnp.repeat(hy_d_skip[0], SSM_HEAD_DIM)[:, None], (SSM_D_INNER, SSM_CHUNK))
    yf, yb = _ssd(ssd_lat, ssd_ctx, alog_col, dsk_col, toks=min(8 * SSM_CHUNK, t))

    wg, wu, wd = ffn_w_gate.astype(BF16), ffn_w_up.astype(BF16), ffn_w_down.astype(BF16)
    x = _mix0(x, a_lat, yf, yb, z, mod0, hy_ssm_norm[0][None, :], norm_ffn[0][None, :], hy_w_out[0].astype(BF16),
              wg, wu, wd, tm_mix)

    return _layer1(x, mod1, norm_mix[1][None, :], sc_w_in[0].astype(BF16), sc_conv_w[0], sc_w_out[0].astype(BF16),
                   norm_ffn[1][None, :], wg, wu, wd, final_norm[None, :], tm_mix)
```

```python
import functools

import jax
import jax.numpy as jnp
import numpy as np
from jax import lax
from jax.experimental import pallas as pl
from jax.experimental.pallas import tpu as pltpu

F32 = jnp.float32
BF16 = jnp.bfloat16

EPS = 1e-6
N_MOD = 6
GRID_W = 64
ROPE_THETA = 10000.0

ATTN_HEADS = 8
ATTN_KV_HEADS = 2
HEAD_DIM = 64
AXIS_DIM = HEAD_DIM // 2
KV_GROUP = ATTN_HEADS // ATTN_KV_HEADS
Q_DIM = ATTN_HEADS * HEAD_DIM
KV_DIM = ATTN_KV_HEADS * HEAD_DIM

SSM_HEADS = 8
SSM_HEAD_DIM = 64
SSM_D_INNER = SSM_HEADS * SSM_HEAD_DIM
SSM_GROUPS = 2
SSM_STATE = 64
SSM_CHUNK = 128
SSM_CONV_DIM = SSM_D_INNER + 2 * SSM_GROUPS * SSM_STATE
SSM_HEADS_PER_GROUP = SSM_HEADS // SSM_GROUPS

LANES = 128
SUBLANES = 8
XBCDT_W = SSM_CONV_DIM + LANES
NEG_BIG = -1e30
LOG2_E = 1.4426950408889634
ATTN_KEY_TILES = (768, 512, 256, 128)
ROW_GROUPS = 2
FFN_CHUNK = 768
ATTN_TQ_EXACT = 128
ATTN_TQ_BOUNDED = 128
ATTN_Q_BLOCKS = 8
ATTN_SUB_TILE = 256
ATTN_PV_LAG = 3
SAFE_SCORE_BOUND = 40.0
SCORE_BOUND_SLACK = 1.001

VMEM_LIMIT = 56 * 1024 * 1024


def _cparams(n_axes):
    return pltpu.CompilerParams(dimension_semantics=("parallel",) * n_axes, vmem_limit_bytes=VMEM_LIMIT)


def _const_spec(shape):
    nd = len(shape)
    return pl.BlockSpec(shape, lambda *_: (0,) * nd, pipeline_mode=pl.Buffered(1))


def _layer_spec(shape, layer):
    nd = len(shape)
    return pl.BlockSpec((None,) + tuple(shape), lambda *_: (layer,) + (0,) * nd, pipeline_mode=pl.Buffered(1))


def _rms(x):
    return x * lax.rsqrt(jnp.mean(x * x, axis=-1, keepdims=True) + EPS)


def _silu(x):
    return x * (1.0 / (1.0 + jnp.exp(-x)))


def _dot(a, b):
    return jnp.dot(a, b, preferred_element_type=F32)


def _dot_exact(a, b):
    return jnp.dot(a, b, preferred_element_type=F32, precision=lax.Precision.HIGHEST)


def _mod_kernel(c_ref, w_ref, b_ref, o_ref):
    c = c_ref[...]
    o_ref[0] = _dot_exact(_silu(c), w_ref[0]) + b_ref[0]


def _modulation(cc, ada_w, ada_b):
    depth, d, nd = ada_w.shape
    r = cc.shape[0]
    tn = d
    return pl.pallas_call(
        _mod_kernel,
        grid=(depth, nd // tn),
        in_specs=[
            pl.BlockSpec((r, d), lambda l, j: (0, 0)),
            pl.BlockSpec((1, d, tn), lambda l, j: (l, 0, j)),
            pl.BlockSpec((1, 1, tn), lambda l, j: (l, 0, j)),
        ],
        out_specs=pl.BlockSpec((1, r, tn), lambda l, j: (l, 0, j)),
        out_shape=jax.ShapeDtypeStruct((depth, r, nd), F32),
        compiler_params=_cparams(2),
        name="mod",
    )(cc, ada_w, ada_b.reshape(depth, 1, nd))


def _inproj_kernel(x_ref, xp_ref, xn_ref, mod_ref, nw_ref, wx_ref, wkz_ref, wt_ref, aq_ref, bq_ref, ck_ref, sk_ref,
                   bd_ref, cw_ref, cb_ref, dtb_ref,
                   qt_ref, vt_ref, k_ref, z_ref, xst_ref, bn_ref, ct_ref, dtt_ref, sc_ref):
    i = pl.program_id(1)
    last = pl.num_programs(1) - 1
    m = mod_ref[0]
    tm = x_ref.shape[1]
    half = HEAD_DIM // 2
    xa = jnp.concatenate([xp_ref[0], x_ref[0], xn_ref[0]], axis=0)
    h = _rms(xa) * (nw_ref[...] * (1.0 + m[1:2])) + m[0:1]
    px = _dot(h.astype(BF16), wx_ref[...])
    hc = h[SUBLANES:SUBLANES + tm].astype(BF16)
    pkz = _dot(hc, wkz_ref[...])
    pt = lax.dot_general(wt_ref[...], hc, (((1,), (1,)), ((), ())), preferred_element_type=F32)
    kx = pkz[:, :KV_DIM]
    sq = kx * kx
    hi = sq.astype(BF16)
    lo = (sq - hi.astype(F32)).astype(BF16)
    ss = _dot(hi, bd_ref[...]) + _dot(lo, bd_ref[...])

    xbc = px[:, :SSM_CONV_DIM]
    sc_ref[0:SUBLANES] = jnp.where(i > 0, xbc[:SUBLANES], 0.0)
    sc_ref[SUBLANES:SUBLANES + tm] = xbc[SUBLANES:SUBLANES + tm]
    sc_ref[SUBLANES + tm:] = jnp.where(i < last, xbc[SUBLANES + tm:], 0.0)
    w = cw_ref[...]
    y = (sc_ref[SUBLANES - 1:SUBLANES - 1 + tm] * w[0:1] + sc_ref[SUBLANES:SUBLANES + tm] * w[1:2]
         + sc_ref[SUBLANES + 1:SUBLANES + 1 + tm] * w[2:3] + cb_ref[...])
    u = _silu(y)
    bc = SSM_D_INNER + SSM_GROUPS * SSM_STATE
    xst_ref[0] = u[:, :SSM_D_INNER].T
    bn_ref[0] = u[:, SSM_D_INNER:bc].astype(BF16)
    ct_ref[0] = u[:, bc:].T.astype(BF16)
    d = px[SUBLANES:SUBLANES + tm, SSM_CONV_DIM:].T[:2 * SSM_HEADS]
    d = d + jnp.concatenate([dtb_ref[...]] * (tm // LANES), axis=1)
    dtt_ref[0] = jnp.maximum(d, 0.0) + jnp.log1p(jnp.exp(-jnp.abs(d)))

    rs = lax.rsqrt(ss * (1.0 / HEAD_DIM) + EPS)
    lane = lax.broadcasted_iota(jnp.int32, kx.shape, 1)
    swap = jnp.where((lane & half) == 0, pltpu.roll(kx, KV_DIM - half, 1), pltpu.roll(kx, half, 1))
    k_ref[0] = (rs * (kx * ck_ref[...] + swap * sk_ref[...])).astype(BF16)

    q3 = pt[:Q_DIM].reshape(ATTN_HEADS, HEAD_DIM, tm)
    qrs = lax.rsqrt(jnp.mean(q3 * q3, axis=1, keepdims=True) + EPS)
    qsw = jnp.concatenate([q3[:, half:], q3[:, :half]], axis=1)
    qo = qrs * (q3 * aq_ref[...][None] + qsw * bq_ref[...][None])
    qt_ref[0] = qo.reshape(Q_DIM, tm).astype(BF16)

    vt_ref[0] = pt[Q_DIM:].astype(BF16)
    z_ref[0] = pkz[:, KV_DIM:]


def _inproj(x, mod, per_batch_mod, nw, wx, wkz, wt, aq, bq, ck, sk, bd, cw, cb, dtb, tm):
    b, t, d = x.shape
    r8 = tm // SUBLANES
    nblk8 = t // SUBLANES
    gn = SSM_GROUPS * SSM_STATE
    mod_idx = (lambda bi, i: (bi, 0, 0)) if per_batch_mod else (lambda bi, i: (0, 0, 0))
    row = lambda bi, i: (bi, i, 0)
    col = lambda bi, i: (bi, 0, i)
    return pl.pallas_call(
        _inproj_kernel,
        grid=(b, t // tm),
        in_specs=[
            pl.BlockSpec((1, tm, d), row),
            pl.BlockSpec((1, SUBLANES, d), lambda bi, i: (bi, jnp.maximum(i * r8 - 1, 0), 0)),
            pl.BlockSpec((1, SUBLANES, d), lambda bi, i: (bi, jnp.minimum((i + 1) * r8, nblk8 - 1), 0)),
            pl.BlockSpec((1, N_MOD, d), mod_idx),
            _const_spec((1, d)),
            _const_spec((d, XBCDT_W)),
            _const_spec((d, KV_DIM + SSM_D_INNER)),
            _const_spec((Q_DIM + KV_DIM, d)),
            pl.BlockSpec((HEAD_DIM, tm), lambda bi, i: (0, i)),
            pl.BlockSpec((HEAD_DIM, tm), lambda bi, i: (0, i)),
            pl.BlockSpec((tm, KV_DIM), lambda bi, i: (i, 0)),
            pl.BlockSpec((tm, KV_DIM), lambda bi, i: (i, 0)),
            _const_spec((KV_DIM, KV_DIM)),
            _const_spec((3, SSM_CONV_DIM)),
            _const_spec((1, SSM_CONV_DIM)),
            _const_spec((2 * SSM_HEADS, LANES)),
        ],
        out_specs=[
            pl.BlockSpec((1, Q_DIM, tm), col),
            pl.BlockSpec((1, KV_DIM, tm), col),
            pl.BlockSpec((1, tm, KV_DIM), row),
            pl.BlockSpec((1, tm, SSM_D_INNER), row),
            pl.BlockSpec((1, SSM_D_INNER, tm), col),
            pl.BlockSpec((1, tm, gn), row),
            pl.BlockSpec((1, gn, tm), col),
            pl.BlockSpec((1, 2 * SSM_HEADS, tm), col),
        ],
        out_shape=[
            jax.ShapeDtypeStruct((b, Q_DIM, t), BF16),
            jax.ShapeDtypeStruct((b, KV_DIM, t), BF16),
            jax.ShapeDtypeStruct((b, t, KV_DIM), BF16),
            jax.ShapeDtypeStruct((b, t, SSM_D_INNER), F32),
            jax.ShapeDtypeStruct((b, SSM_D_INNER, t), F32),
            jax.ShapeDtypeStruct((b, t, gn), BF16),
            jax.ShapeDtypeStruct((b, gn, t), BF16),
            jax.ShapeDtypeStruct((b, 2 * SSM_HEADS, t), F32),
        ],
        scratch_shapes=[pltpu.VMEM((tm + 2 * SUBLANES, SSM_CONV_DIM), F32)],
        compiler_params=_cparams(2),
        name="inproj",
    )(x, x, x, mod, nw, wx, wkz, wt, aq, bq, ck, sk, bd, cw, cb, dtb)


def _attn_kernel(qt_ref, k_ref, vt_ref, o_ref, s0_ref, s1_ref, p0_ref, p1_ref, m0_ref, m1_ref, *, nq):
    g = pl.program_id(1)
    i = pl.program_id(2)
    tq = qt_ref.shape[2]
    cols = KV_GROUP * tq
    nk, kt = s0_ref.shape[0], s0_ref.shape[1]
    s_refs, m_refs, p_refs = (s0_ref, s1_ref), (m0_ref, m1_ref), (p0_ref, p1_ref)

    def run(a_buf, b_buf):
        do_a, do_bc = a_buf is not None, b_buf is not None
        if do_a:
            q4 = qt_ref[0]
            qcat = jnp.concatenate([q4[h * HEAD_DIM:(h + 1) * HEAD_DIM] for h in range(KV_GROUP)], axis=1)
            qp = jnp.concatenate([qcat] * ATTN_KV_HEADS, axis=0)
            rowgrp = lax.broadcasted_iota(jnp.int32, qp.shape, 0) // HEAD_DIM
            qp = jnp.where(rowgrp == g, qp, jnp.zeros_like(qp))
        if do_bc:
            m8 = jnp.broadcast_to(jnp.max(m_refs[b_buf][...], axis=0, keepdims=True), (SUBLANES, cols))

        def stage_a(j, mx):
            ktile = k_ref[0, pl.ds(pl.multiple_of(j * kt, kt), kt), :]
            s = _dot(ktile, qp)
            s_refs[a_buf][j] = s
            return jnp.maximum(mx, jnp.max(s.reshape(kt // SUBLANES, SUBLANES, cols), axis=0))

        def stage_b(j, par, ls):
            s = s_refs[b_buf][j].reshape(kt // SUBLANES, SUBLANES, cols)
            p = jnp.exp2(s - m8[None])
            p_refs[par][...] = p.reshape(kt, cols).astype(BF16)
            return ls + jnp.sum(p, axis=0)

        def stage_c(j, par, acc):
            return acc + _dot(vt_ref[0, 0, j], p_refs[par][...])

        def tile(j, par, carry):
            mx, ls, acc = carry
            if do_a:
                mx = stage_a(j, mx)
            if do_bc:
                acc = stage_c(j - 1, 1 - par, acc)
                ls = stage_b(j, par, ls)
            return mx, ls, acc

        mx = jnp.full((SUBLANES, cols), NEG_BIG, F32)
        ls = jnp.zeros((SUBLANES, cols), F32)
        acc = jnp.zeros((HEAD_DIM, cols), F32)
        if do_a:
            mx = stage_a(0, mx)
        if do_bc:
            ls = stage_b(0, 0, ls)

        def body(t, carry):
            j = 1 + 2 * t
            return tile(j + 1, 0, tile(j, 1, carry))

        carry = lax.fori_loop(0, (nk - 1) // 2, body, (mx, ls, acc), unroll=True)
        if (nk - 1) % 2:
            carry = tile(nk - 1, (nk - 1) % 2, carry)
        mx, ls, acc = carry
        if do_a:
            m_refs[a_buf][...] = mx
        if do_bc:
            acc = stage_c(nk - 1, (nk - 1) % 2, acc)
            o = acc * (1.0 / jnp.sum(ls, axis=0, keepdims=True))
            o4 = jnp.concatenate([o[:, h * tq:(h + 1) * tq] for h in range(KV_GROUP)], axis=0)
            o_ref[0] = o4.T.astype(BF16)

    @pl.when(i == 0)
    def _():
        run(0, None)

    inner = jnp.logical_and(i > 0, i < nq)

    @pl.when(jnp.logical_and(inner, i % 2 == 1))
    def _():
        run(1, 0)

    @pl.when(jnp.logical_and(inner, i % 2 == 0))
    def _():
        run(0, 1)

    @pl.when(i == nq)
    def _():
        run(None, (nq - 1) % 2)


def _attention(qt, k_all, vt_tiles, tq):
    b, _, t = qt.shape
    tk = k_all.shape[1]
    nk, kt = vt_tiles.shape[2], vt_tiles.shape[4]
    nq = t // tq
    gw = KV_GROUP * HEAD_DIM
    cols = KV_GROUP * tq
    return pl.pallas_call(
        functools.partial(_attn_kernel, nq=nq),
        grid=(b, ATTN_KV_HEADS, nq + 1),
        in_specs=[
            pl.BlockSpec((1, gw, tq), lambda bi, g, i: (bi, g, jnp.minimum(i, nq - 1))),
            pl.BlockSpec((1, tk, KV_DIM), lambda bi, g, i: (bi, 0, 0)),
            pl.BlockSpec((1, 1, nk, HEAD_DIM, kt), lambda bi, g, i: (bi, g, 0, 0, 0)),
        ],
        out_specs=pl.BlockSpec((1, tq, gw), lambda bi, g, i: (bi, jnp.maximum(i - 1, 0), g)),
        out_shape=jax.ShapeDtypeStruct((b, t, Q_DIM), BF16),
        scratch_shapes=(
            [pltpu.VMEM((nk, kt, cols), F32)] * 2
            + [pltpu.VMEM((kt, cols), BF16)] * 2
            + [pltpu.VMEM((SUBLANES, cols), F32)] * 2
        ),
        compiler_params=pltpu.CompilerParams(dimension_semantics=("parallel", "parallel", "arbitrary"),
                                             vmem_limit_bytes=VMEM_LIMIT),
        name="attn",
    )(qt, k_all, vt_tiles)


def _attn_bounded_kernel(qt_ref, k_ref, vt_ref, km_ref, o_ref, *p_refs):
    g = pl.program_id(1)
    cols = p_refs[0].shape[1]
    tq = cols // KV_GROUP
    n_blk = qt_ref.shape[2] // tq
    sub = p_refs[0].shape[0]
    n_all = k_ref.shape[1] // sub

    kmax = jnp.concatenate([km_ref[0:1, :]] * (cols // LANES), axis=1)
    qps, mb8s = [], []
    for blk in range(n_blk):
        q4 = qt_ref[0, :, blk * tq:(blk + 1) * tq]
        qf = q4.astype(F32)
        qn = jnp.sqrt(jnp.sum((qf * qf).reshape(KV_GROUP, HEAD_DIM, tq), axis=1))
        qn = jnp.concatenate([qn[h:h + 1] for h in range(KV_GROUP)], axis=1)
        mb8s.append(jnp.broadcast_to(qn * kmax * SCORE_BOUND_SLACK, (SUBLANES, cols)))
        qcat = jnp.concatenate([q4[h * HEAD_DIM:(h + 1) * HEAD_DIM] for h in range(KV_GROUP)], axis=1)
        qp = jnp.concatenate([qcat] * ATTN_KV_HEADS, axis=0)
        rowgrp = lax.broadcasted_iota(jnp.int32, qp.shape, 0) // HEAD_DIM
        qps.append(jnp.where(rowgrp == g, qp, jnp.zeros_like(qp)))

    ring = len(p_refs)
    lag = ring - 1
    ls = [jnp.zeros((SUBLANES, cols), F32) for _ in range(n_blk)]
    acc = [jnp.zeros((HEAD_DIM, cols), F32) for _ in range(n_blk)]

    def pv(w):
        blk, u = divmod(w, n_all)
        acc[blk] = acc[blk] + _dot(vt_ref[0, :, u * sub:(u + 1) * sub], p_refs[w % ring][...])
        if u == n_all - 1:
            o = acc[blk] * (1.0 / jnp.sum(ls[blk], axis=0, keepdims=True))
            o4 = jnp.concatenate([o[:, h * tq:(h + 1) * tq] for h in range(KV_GROUP)], axis=0)
            o_ref[0, blk * tq:(blk + 1) * tq, :] = o4.T.astype(BF16)

    n_items = n_blk * n_all
    for w in range(n_items):
        blk, u = divmod(w, n_all)
        s = _dot(k_ref[0, u * sub:(u + 1) * sub, :], qps[blk])
        if w >= lag:
            pv(w - lag)
        p = jnp.exp2(s.reshape(sub // SUBLANES, SUBLANES, cols) - mb8s[blk][None])
        p_refs[w % ring][...] = p.reshape(sub, cols).astype(BF16)
        ls[blk] = ls[blk] + jnp.sum(p, axis=0)
    for w in range(max(n_items - lag, 0), n_items):
        pv(w)


def _attention_bounded(qt, k_all, vt_all, kmax, tq):
    b, _, t = qt.shape
    tk = k_all.shape[1]
    sub = next(c for c in (ATTN_SUB_TILE, LANES) if tk % c == 0)
    gw = KV_GROUP * HEAD_DIM
    cols = KV_GROUP * tq
    tstep = tq * (ATTN_Q_BLOCKS if t % (tq * ATTN_Q_BLOCKS) == 0 else 1)
    return pl.pallas_call(
        _attn_bounded_kernel,
        grid=(b, ATTN_KV_HEADS, t // tstep),
        in_specs=[
            pl.BlockSpec((1, gw, tstep), lambda bi, g, i: (bi, g, i)),
            pl.BlockSpec((1, tk, KV_DIM), lambda bi, g, i: (bi, 0, 0)),
            pl.BlockSpec((1, HEAD_DIM, tk), lambda bi, g, i: (bi, g, 0)),
            _const_spec((SUBLANES, LANES)),
        ],
        out_specs=pl.BlockSpec((1, tstep, gw), lambda bi, g, i: (bi, i, g)),
        out_shape=jax.ShapeDtypeStruct((b, t, Q_DIM), BF16),
        scratch_shapes=[pltpu.VMEM((sub, cols), BF16)] * (ATTN_PV_LAG + 1),
        compiler_params=_cparams(3),
        name="attn_bounded",
    )(qt, k_all, vt_all, jnp.full((SUBLANES, LANES), kmax, F32))


def _attention_exact(qt, k_all, vt_all, kmax, tq):
    del kmax
    b, _, tk = vt_all.shape
    kt = next(c for c in ATTN_KEY_TILES if tk % c == 0)
    vt_tiles = vt_all.reshape(b, ATTN_KV_HEADS, HEAD_DIM, tk // kt, kt).transpose(0, 1, 3, 2, 4)
    return _attention(qt, k_all, vt_tiles, tq)


class _SsdChunk:
    def __init__(self, xst, bn, ct, dtt, a_col, direction):
        self.xst, self.bn, self.ct, self.dtt, self.a_col, self.direction = xst, bn, ct, dtt, a_col, direction

    def decay(self):
        ln = self.xst.shape[1]
        da = self.dtt * self.a_col
        ri = lax.broadcasted_iota(jnp.int32, (ln, ln), 0)
        ci = lax.broadcasted_iota(jnp.int32, (ln, ln), 1)
        self.reach = (ci >= ri) if self.direction == 0 else (ci <= ri)
        self.cs_row = _dot_exact(da, self.reach.astype(F32))
        self.tot = _dot_exact(da, jnp.ones((ln, ln), F32))

    def local(self):
        ln = self.xst.shape[1]
        gn = SSM_GROUPS * SSM_STATE
        cs_col = jnp.concatenate([self.cs_row, jnp.zeros((ln - 2 * SSM_HEADS, ln), F32)], axis=0).T
        rowgrp = lax.broadcasted_iota(jnp.int32, (gn, ln), 0) // SSM_STATE
        self.ctz, self.y_diag, self.xw, self.e_row, self.e_tot = [], [], [], [], []
        for grp in range(SSM_GROUPS):
            ctz = jnp.where(rowgrp == grp, self.ct, jnp.zeros_like(self.ct))
            cbt = _dot(self.bn, ctz)
            y_parts, xw_parts = [], []
            for e in range(SSM_HEADS_PER_GROUP):
                hd = grp * SSM_HEADS_PER_GROUP + e
                c = self.direction * SSM_HEADS + hd
                row = self.cs_row[c:c + 1, :]
                lmt = jnp.exp2(jnp.where(self.reach, row - cs_col[:, c:c + 1], NEG_BIG))
                mt = (cbt * lmt).astype(BF16)
                xdt = self.xst[hd * SSM_HEAD_DIM:(hd + 1) * SSM_HEAD_DIM, :] * self.dtt[c:c + 1, :]
                y_parts.append(_dot(xdt.astype(BF16), mt))
                xw_parts.append((xdt * jnp.exp2(self.tot[c:c + 1, :] - row)).astype(BF16))
                self.e_row.append(jnp.exp2(row))
                self.e_tot.append(jnp.exp2(self.tot[c:c + 1, :gn]))
            self.ctz.append(ctz)
            self.y_diag.append(y_parts)
            self.xw.append(jnp.concatenate(xw_parts, axis=0))

    def carry(self, h_ref):
        gp = SSM_HEADS_PER_GROUP * SSM_HEAD_DIM
        ys = []
        for grp in range(SSM_GROUPS):
            hg = h_ref[grp * gp:(grp + 1) * gp, :]
            y_off = _dot(hg.astype(BF16), self.ctz[grp])
            st = _dot(self.xw[grp], self.bn)
            h_parts = []
            for e in range(SSM_HEADS_PER_GROUP):
                hd = grp * SSM_HEADS_PER_GROUP + e
                rows = slice(e * SSM_HEAD_DIM, (e + 1) * SSM_HEAD_DIM)
                ys.append(self.y_diag[grp][e] + y_off[rows] * self.e_row[hd])
                h_parts.append(hg[rows] * self.e_tot[hd])
            h_ref[grp * gp:(grp + 1) * gp, :] = jnp.concatenate(h_parts, axis=0) + st
        return jnp.concatenate(ys, axis=0)


def _ssd_run(chunks, h_refs):
    for ch in chunks:
        ch.decay()
    for ch in chunks:
        ch.local()
    return [ch.carry(h_refs[ch.direction]) for ch in chunks]


def _ssd_kernel(xf_ref, xb_ref, bf_ref, bb_ref, cf_ref, cb_ref, df_ref, db_ref, xc_ref, bc_ref, cc_ref, dc_ref,
                alog_ref, dsk_ref, yf_ref, yb_ref, hf_ref, hb_ref):
    c = pl.program_id(1)
    a_col = -jnp.exp(alog_ref[...]) * LOG2_E
    ln = SSM_CHUNK
    h_refs = (hf_ref, hb_ref)

    def chunk(x_ref, b_ref, c_ref, d_ref, s, direction):
        tok = slice(s * ln, (s + 1) * ln)
        return _SsdChunk(x_ref[0, :, tok], b_ref[0, tok, :], c_ref[0, :, tok], d_ref[0, :, tok], a_col, direction)

    @pl.when(c == 0)
    def _():
        n_ctx = xc_ref.shape[2] // ln
        for direction in range(2):
            h_refs[direction][...] = jnp.zeros_like(h_refs[direction])
        chunks = []
        for k in range(n_ctx):
            chunks.append(chunk(xc_ref, bc_ref, cc_ref, dc_ref, k, 0))
            chunks.append(chunk(xc_ref, bc_ref, cc_ref, dc_ref, n_ctx - 1 - k, 1))
        _ssd_run(chunks, h_refs)

    n_sub = xf_ref.shape[2] // ln
    chunks = []
    for k in range(n_sub):
        chunks.append(chunk(xf_ref, bf_ref, cf_ref, df_ref, k, 0))
        chunks.append(chunk(xb_ref, bb_ref, cb_ref, db_ref, n_sub - 1 - k, 1))
    ys = _ssd_run(chunks, h_refs)
    for k in range(n_sub):
        tok = slice(k * ln, (k + 1) * ln)
        yf_ref[0, :, tok] = ys[2 * k] + xf_ref[0, :, tok] * dsk_ref[...]
        tok = slice((n_sub - 1 - k) * ln, (n_sub - k) * ln)
        yb_ref[0, :, tok] = ys[2 * k + 1]


def _ssd(lat, ctx, alog_col, dsk_col, toks):
    xst, bn, ct, dtt = lat
    b, di, t = xst.shape
    gn = bn.shape[2]
    nh2 = dtt.shape[1]
    nctx = ctx[0].shape[2]
    ns = t // toks
    fcol = lambda bi, c: (bi, 0, c)
    bcol = lambda bi, c: (bi, 0, ns - 1 - c)
    frow = lambda bi, c: (bi, c, 0)
    brow = lambda bi, c: (bi, ns - 1 - c, 0)
    whole = lambda bi, c: (bi, 0, 0)
    return pl.pallas_call(
        _ssd_kernel,
        grid=(b, ns),
        in_specs=[
            pl.BlockSpec((1, di, toks), fcol), pl.BlockSpec((1, di, toks), bcol),
            pl.BlockSpec((1, toks, gn), frow), pl.BlockSpec((1, toks, gn), brow),
            pl.BlockSpec((1, gn, toks), fcol), pl.BlockSpec((1, gn, toks), bcol),
            pl.BlockSpec((1, nh2, toks), fcol), pl.BlockSpec((1, nh2, toks), bcol),
            pl.BlockSpec((1, di, nctx), whole), pl.BlockSpec((1, nctx, gn), whole),
            pl.BlockSpec((1, gn, nctx), whole), pl.BlockSpec((1, nh2, nctx), whole),
            _const_spec((nh2, SSM_CHUNK)),
            _const_spec((di, SSM_CHUNK)),
        ],
        out_specs=[pl.BlockSpec((1, di, toks), fcol), pl.BlockSpec((1, di, toks), bcol)],
        out_shape=[jax.ShapeDtypeStruct((b, di, t), F32), jax.ShapeDtypeStruct((b, di, t), F32)],
        scratch_shapes=[pltpu.VMEM((di, gn), F32)] * 2,
        compiler_params=pltpu.CompilerParams(dimension_semantics=("parallel", "arbitrary"),
                                             vmem_limit_bytes=VMEM_LIMIT),
        name="ssd",
    )(xst, xst, bn, bn, ct, ct, dtt, dtt, *ctx, alog_col, dsk_col)


def _row_groups(tm):
    n = ROW_GROUPS if tm % (ROW_GROUPS * SUBLANES * 2) == 0 else 1
    return [slice(i * (tm // n), (i + 1) * (tm // n)) for i in range(n)]


def _ffn_tail(x1s, m, nw, wg_ref, wu_ref, wd_ref):
    scale = nw * (1.0 + m[4:5])
    hs = [(_rms(x1) * scale + m[3:4]).astype(BF16) for x1 in x1s]
    ffn = wg_ref.shape[1]
    fs = [None] * len(x1s)
    for c0 in range(0, ffn, FFN_CHUNK):
        c1 = min(c0 + FFN_CHUNK, ffn)
        hids = [(_silu(_dot(h, wg_ref[:, c0:c1])) * _dot(h, wu_ref[:, c0:c1])).astype(BF16) for h in hs]
        for i, hid in enumerate(hids):
            part = _dot(hid, wd_ref[c0:c1, :])
            fs[i] = part if fs[i] is None else fs[i] + part
    return [x1 + m[5:6] * f for x1, f in zip(x1s, fs)]


def _mix0_kernel(x_ref, a_ref, yf_ref, yb_ref, z_ref, mod_ref, snw_ref, nw_ref,
                 wo_ref, wg_ref, wu_ref, wd_ref, o_ref):
    m = mod_ref[0]
    gw = SSM_D_INNER // SSM_GROUPS
    groups = _row_groups(x_ref.shape[1])
    yns = []
    for rows in groups:
        y = (yf_ref[0, :, rows] + yb_ref[0, :, rows]).T
        gy = y * _silu(z_ref[0, rows, :])
        gn = jnp.concatenate([_rms(gy[:, i * gw:(i + 1) * gw]) for i in range(SSM_GROUPS)], axis=1)
        yns.append((gn * snw_ref[...]).astype(BF16))
    mix_a = [_dot(a_ref[0, rows, :], wo_ref[:Q_DIM, :]) for rows in groups]
    x1s = []
    for rows, yn, ma in zip(groups, yns, mix_a):
        x1s.append(x_ref[0, rows, :] + m[2:3] * (ma + _dot(yn, wo_ref[Q_DIM:, :])))
    for rows, out in zip(groups, _ffn_tail(x1s, m, nw_ref[...], wg_ref, wu_ref, wd_ref)):
        o_ref[0, rows, :] = out


def _mix0(x, a, yf, yb, z, mod, snw, nw, wo, wg, wu, wd, tm):
    b, t, d = x.shape
    ffn = wg.shape[2]
    row = lambda bi, i: (bi, i, 0)
    col = lambda bi, i: (bi, 0, i)
    return pl.pallas_call(
        _mix0_kernel,
        grid=(b, t // tm),
        in_specs=[
            pl.BlockSpec((1, tm, d), row),
            pl.BlockSpec((1, tm, Q_DIM), row),
            pl.BlockSpec((1, SSM_D_INNER, tm), col),
            pl.BlockSpec((1, SSM_D_INNER, tm), col),
            pl.BlockSpec((1, tm, SSM_D_INNER), row),
            pl.BlockSpec((1, N_MOD, d), lambda bi, i: (bi, 0, 0)),
            _const_spec((1, SSM_D_INNER)),
            _const_spec((1, d)),
            _const_spec((Q_DIM + SSM_D_INNER, d)),
            _layer_spec((d, ffn), 0),
            _layer_spec((d, ffn), 0),
            _layer_spec((ffn, d), 0),
        ],
        out_specs=pl.BlockSpec((1, tm, d), row),
        out_shape=jax.ShapeDtypeStruct((b, t, d), F32),
        compiler_params=_cparams(2),
        name="mix0",
    )(x, a, yf, yb, z, mod, snw, nw, wo, wg, wu, wd)


def _layer1_kernel(x_ref, xp_ref, xn_ref, mod_ref, nmix_ref, win_ref, cw_ref, wout_ref, nffn_ref,
                   wg_ref, wu_ref, wd_ref, fn_ref, o_ref, v_ref):
    i = pl.program_id(1)
    last = pl.num_programs(1) - 1
    m = mod_ref[0]
    tm, d = x_ref.shape[1], x_ref.shape[2]
    x = x_ref[0]
    xa = jnp.concatenate([xp_ref[0], x, xn_ref[0]], axis=0)
    h = _rms(xa) * (nmix_ref[...] * (1.0 + m[1:2])) + m[0:1]
    pcu = _dot(h.astype(BF16), win_ref[:, d:])
    pgb = _dot(h[SUBLANES:SUBLANES + tm].astype(BF16), win_ref[:, :d])
    v = pcu[:, :d] * pcu[:, d:]
    v_ref[0:SUBLANES] = jnp.where(i > 0, v[:SUBLANES], 0.0)
    v_ref[SUBLANES:SUBLANES + tm] = v[SUBLANES:SUBLANES + tm]
    v_ref[SUBLANES + tm:] = jnp.where(i < last, v[SUBLANES + tm:], 0.0)
    cw = cw_ref[...]
    groups = _row_groups(tm)
    x1s = []
    for rows in groups:
        r0 = rows.start + SUBLANES
        n = rows.stop - rows.start
        conv = (v_ref[r0 - 1:r0 - 1 + n] * cw[0:1] + v_ref[r0:r0 + n] * cw[1:2] + v_ref[r0 + 1:r0 + 1 + n] * cw[2:3])
        gated = (pgb[rows] * conv).astype(BF16)
        x1s.append(x[rows] + m[2:3] * _dot(gated, wout_ref[...]))
    for rows, x2 in zip(groups, _ffn_tail(x1s, m, nffn_ref[...], wg_ref, wu_ref, wd_ref)):
        o_ref[0, rows, :] = _rms(x2) * fn_ref[...]


def _layer1(x, mod, nmix, win, cw, wout, nffn, wg, wu, wd, fn, tm):
    b, t, d = x.shape
    ffn = wg.shape[2]
    r8 = tm // SUBLANES
    nblk8 = t // SUBLANES
    return pl.pallas_call(
        _layer1_kernel,
        grid=(b, t // tm),
        in_specs=[
            pl.BlockSpec((1, tm, d), lambda bi, i: (bi, i, 0)),
            pl.BlockSpec((1, SUBLANES, d), lambda bi, i: (bi, jnp.maximum(i * r8 - 1, 0), 0)),
            pl.BlockSpec((1, SUBLANES, d), lambda bi, i: (bi, jnp.minimum((i + 1) * r8, nblk8 - 1), 0)),
            pl.BlockSpec((1, N_MOD, d), lambda bi, i: (bi, 0, 0)),
            _const_spec((1, d)),
            _const_spec((d, 3 * d)),
            _const_spec((3, d)),
            _const_spec((d, d)),
            _const_spec((1, d)),
            _layer_spec((d, ffn), 1),
            _layer_spec((d, ffn), 1),
            _layer_spec((ffn, d), 1),
            _const_spec((1, d)),
        ],
        out_specs=pl.BlockSpec((1, tm, d), lambda bi, i: (bi, i, 0)),
        out_shape=jax.ShapeDtypeStruct((b, t, d), F32),
        scratch_shapes=[pltpu.VMEM((tm + 2 * SUBLANES, d), F32)],
        compiler_params=_cparams(2),
        name="layer1",
    )(x, x, x, mod, nmix, win, cw, wout, nffn, wg, wu, wd, fn)


def _rope_tables(n_tokens):
    rows = n_tokens // GRID_W
    row = jnp.repeat(jnp.arange(rows), GRID_W).astype(F32)
    col = jnp.tile(jnp.arange(GRID_W), rows).astype(F32)
    inv = 1.0 / (ROPE_THETA ** (jnp.arange(0, AXIS_DIM, 2, dtype=F32) / AXIS_DIM))
    ang = jnp.concatenate([row[:, None] * inv, col[:, None] * inv], axis=-1)
    return jnp.cos(ang), jnp.sin(ang)


_HEAD_PERM = np.concatenate([np.arange(0, HEAD_DIM, 2), np.arange(1, HEAD_DIM, 2)])


def _qk_tables(cos, sin, q_gain, k_gain):
    half = HEAD_DIM // 2
    sign = jnp.concatenate([-jnp.ones((half,), F32), jnp.ones((half,), F32)])
    c64 = jnp.concatenate([cos, cos], axis=1)
    s64 = jnp.concatenate([sin, sin], axis=1) * sign
    swap = np.concatenate([np.arange(half, HEAD_DIM), np.arange(0, half)])

    def tables(gain, scale):
        gp = gain[_HEAD_PERM] * scale
        return c64 * gp[None, :], s64 * gp[swap][None, :]

    aq, bq = tables(q_gain, HEAD_DIM ** -0.5 * LOG2_E)
    ak, bk = tables(k_gain, 1.0)
    return aq.T, bq.T, jnp.tile(ak, (1, ATTN_KV_HEADS)), jnp.tile(bk, (1, ATTN_KV_HEADS))


def kernel(x, c, ctx, c_ctx, ada_w, ada_b, norm_mix, norm_ffn, ffn_w_gate, ffn_w_up, ffn_w_down, hy_w_in,
           hy_q_norm, hy_k_norm, hy_conv_w, hy_conv_b, hy_dt_bias, hy_a_log, hy_d_skip, hy_ssm_norm, hy_w_out,
           sc_w_in, sc_conv_w, sc_w_out, final_norm):
    b, t, d = x.shape
    nctx = ctx.shape[1]
    tm_in = min(1024, t)
    tm_mix = min(512, t)

    n_rows = -(-(b + 1) // SUBLANES) * SUBLANES
    cc = jnp.zeros((n_rows, d), F32).at[:b].set(c).at[b].set(c_ctx)
    mods = _modulation(cc, ada_w, ada_b).reshape(ada_w.shape[0], n_rows, N_MOD, d)
    mod0, mod0_ctx, mod1 = mods[0, :b], mods[0, b:b + 1], mods[1, :b]

    w_in = hy_w_in[0]
    idx_k, idx_v, idx_z = Q_DIM, Q_DIM + KV_DIM, Q_DIM + 2 * KV_DIM
    idx_xbc = idx_z + SSM_D_INNER
    q_cols = (np.arange(ATTN_HEADS)[:, None] * HEAD_DIM + _HEAD_PERM[None, :]).reshape(-1)
    k_cols = idx_k + (np.arange(ATTN_KV_HEADS)[:, None] * HEAD_DIM + _HEAD_PERM[None, :]).reshape(-1)
    dt_pad = jnp.zeros((d, LANES - 2 * SSM_HEADS), F32)
    wkz = jnp.concatenate([w_in[:, k_cols], w_in[:, idx_z:idx_xbc]], axis=1).astype(BF16)
    wx = jnp.concatenate([w_in[:, idx_xbc:], dt_pad], axis=1).astype(BF16)
    wt = jnp.concatenate([w_in[:, q_cols], w_in[:, idx_v:idx_z]], axis=1).T.astype(BF16)
    nw0 = norm_mix[0][None, :]

    cos, sin = _rope_tables(t)
    aq, bq, ck, sk = _qk_tables(cos, sin, hy_q_norm[0], hy_k_norm[0])
    ones_c, zeros_c = jnp.ones((nctx, AXIS_DIM), F32), jnp.zeros((nctx, AXIS_DIM), F32)
    aq_c, bq_c, ck_c, sk_c = _qk_tables(ones_c, zeros_c, hy_q_norm[0], hy_k_norm[0])
    seg = np.arange(KV_DIM) // HEAD_DIM
    bd = jnp.asarray(seg[:, None] == seg[None, :], BF16)

    conv_w, conv_b = hy_conv_w[0], hy_conv_b[0][None, :]
    dtb = jnp.broadcast_to(hy_dt_bias[0].reshape(-1, 1), (2 * SSM_HEADS, LANES))
    shared = (nw0, wx, wkz, wt)
    qt, vt, k, z, *ssd_lat = _inproj(x, mod0, True, *shared, aq, bq, ck, sk, bd, conv_w, conv_b, dtb, tm_in)
    _, vct, kc, _, *ssd_ctx = _inproj(ctx, mod0_ctx, False, *shared, aq_c, bq_c, ck_c, sk_c, bd, conv_w, conv_b, dtb,
                                      nctx)

    k_all = jnp.concatenate([k, kc], axis=1)
    vt_all = jnp.concatenate([vt, vct], axis=2)
    kmax = 1.01 * HEAD_DIM ** 0.5 * jnp.max(jnp.abs(hy_k_norm[0]))
    qmax = 1.01 * LOG2_E * jnp.max(jnp.abs(hy_q_norm[0]))
    score_bound = qmax * kmax
    a_lat = lax.cond(score_bound <= SAFE_SCORE_BOUND,
                     functools.partial(_attention_bounded, tq=min(ATTN_TQ_BOUNDED, t)),
                     functools.partial(_attention_exact, tq=min(ATTN_TQ_EXACT, t)),
                     qt, k_all, vt_all, kmax)

    alog_col = jnp.broadcast_to(hy_a_log[0].reshape(-1, 1), (2 * SSM_HEADS, SSM_CHUNK))
    dsk_col = jnp.broadcast_to(jnp.repeat(hy_d_skip[0], SSM_HEAD_DIM)[:, None], (SSM_D_INNER, SSM_CHUNK))
    yf, yb = _ssd(ssd_lat, ssd_ctx, alog_col, dsk_col, toks=min(8 * SSM_CHUNK, t))

    wg, wu, wd = ffn_w_gate.astype(BF16), ffn_w_up.astype(BF16), ffn_w_down.astype(BF16)
    x = _mix0(x, a_lat, yf, yb, z, mod0, hy_ssm_norm[0][None, :], norm_ffn[0][None, :], hy_w_out[0].astype(BF16),
              wg, wu, wd, tm_mix)

    return _layer1(x, mod1, norm_mix[1][None, :], sc_w_in[0].astype(BF16), sc_conv_w[0], sc_w_out[0].astype(BF16),
                   norm_ffn[1][None, :], wg, wu, wd, final_norm[None, :], tm_mix)
```

```python
import functools

import jax
import jax.numpy as jnp
import numpy as np
from jax import lax
from jax.experimental import pallas as pl
from jax.experimental.pallas import tpu as pltpu

F32 = jnp.float32
BF16 = jnp.bfloat16

EPS = 1e-6
N_MOD = 6
GRID_W = 64
ROPE_THETA = 10000.0

ATTN_HEADS = 8
ATTN_KV_HEADS = 2
HEAD_DIM = 64
AXIS_DIM = HEAD_DIM // 2
KV_GROUP = ATTN_HEADS // ATTN_KV_HEADS
Q_DIM = ATTN_HEADS * HEAD_DIM
KV_DIM = ATTN_KV_HEADS * HEAD_DIM

SSM_HEADS = 8
SSM_HEAD_DIM = 64
SSM_D_INNER = SSM_HEADS * SSM_HEAD_DIM
SSM_GROUPS = 2
SSM_STATE = 64
SSM_CHUNK = 128
SSM_CONV_DIM = SSM_D_INNER + 2 * SSM_GROUPS * SSM_STATE
SSM_HEADS_PER_GROUP = SSM_HEADS // SSM_GROUPS

LANES = 128
SUBLANES = 8
MXU_TILE = 256
VMEM_LIMIT = 56 * 1024 * 1024

XBCDT_W = SSM_CONV_DIM + LANES
NEG_BIG = -1e30
LOG2_E = 1.4426950408889634

ROW_GROUPS = 2
FFN_CHUNK = 3 * MXU_TILE
ATTN_KEY_TILES = (768, 512, 256, 128)
ATTN_TQ_EXACT = 128
ATTN_TQ_BOUNDED = 128
ATTN_Q_BLOCKS = 8
ATTN_SUB_TILE = MXU_TILE
ATTN_PV_LAG = 3
SAFE_SCORE_BOUND = 40.0
SCORE_BOUND_SLACK = 1.001


def _cparams(n_axes):
    return pltpu.CompilerParams(dimension_semantics=("parallel",) * n_axes, vmem_limit_bytes=VMEM_LIMIT)


def _const_spec(shape):
    nd = len(shape)
    return pl.BlockSpec(shape, lambda *_: (0,) * nd, pipeline_mode=pl.Buffered(1))


def _layer_spec(shape, layer):
    nd = len(shape)
    return pl.BlockSpec((None,) + tuple(shape), lambda *_: (layer,) + (0,) * nd, pipeline_mode=pl.Buffered(1))


def _rms(x):
    return x * lax.rsqrt(jnp.mean(x * x, axis=-1, keepdims=True) + EPS)


def _silu(x):
    return x * (1.0 / (1.0 + jnp.exp(-x)))


def _dot(a, b):
    return jnp.dot(a, b, preferred_element_type=F32)


def _dot_exact(a, b):
    return jnp.dot(a, b, preferred_element_type=F32, precision=lax.Precision.HIGHEST)


def _mod_kernel(c_ref, w_ref, b_ref, o_ref):
    c = c_ref[...]
    o_ref[0] = _dot_exact(_silu(c), w_ref[0]) + b_ref[0]


def _modulation(cc, ada_w, ada_b):
    depth, d, nd = ada_w.shape
    r = cc.shape[0]
    tn = d
    return pl.pallas_call(
        _mod_kernel,
        grid=(depth, nd // tn),
        in_specs=[
            pl.BlockSpec((r, d), lambda l, j: (0, 0)),
            pl.BlockSpec((1, d, tn), lambda l, j: (l, 0, j)),
            pl.BlockSpec((1, 1, tn), lambda l, j: (l, 0, j)),
        ],
        out_specs=pl.BlockSpec((1, r, tn), lambda l, j: (l, 0, j)),
        out_shape=jax.ShapeDtypeStruct((depth, r, nd), F32),
        compiler_params=_cparams(2),
        name="mod",
    )(cc, ada_w, ada_b.reshape(depth, 1, nd))


def _inproj_kernel(x_ref, xp_ref, xn_ref, mod_ref, nw_ref, wx_ref, wkz_ref, wt_ref, aq_ref, bq_ref, ck_ref, sk_ref,
                   bd_ref, cw_ref, cb_ref, dtb_ref,
                   qt_ref, vt_ref, k_ref, z_ref, xst_ref, bn_ref, ct_ref, dtt_ref, sc_ref):
    i = pl.program_id(1)
    last = pl.num_programs(1) - 1
    m = mod_ref[0]
    tm = x_ref.shape[1]
    half = HEAD_DIM // 2
    xa = jnp.concatenate([xp_ref[0], x_ref[0], xn_ref[0]], axis=0)
    h = _rms(xa) * (nw_ref[...] * (1.0 + m[1:2])) + m[0:1]
    px = _dot(h.astype(BF16), wx_ref[...])
    hc = h[SUBLANES:SUBLANES + tm].astype(BF16)
    pkz = _dot(hc, wkz_ref[...])
    pt = lax.dot_general(wt_ref[...], hc, (((1,), (1,)), ((), ())), preferred_element_type=F32)
    kx = pkz[:, :KV_DIM]
    sq = kx * kx
    hi = sq.astype(BF16)
    lo = (sq - hi.astype(F32)).astype(BF16)
    ss = _dot(hi, bd_ref[...]) + _dot(lo, bd_ref[...])

    xbc = px[:, :SSM_CONV_DIM]
    sc_ref[0:SUBLANES] = jnp.where(i > 0, xbc[:SUBLANES], 0.0)
    sc_ref[SUBLANES:SUBLANES + tm] = xbc[SUBLANES:SUBLANES + tm]
    sc_ref[SUBLANES + tm:] = jnp.where(i < last, xbc[SUBLANES + tm:], 0.0)
    w = cw_ref[...]
    y = (sc_ref[SUBLANES - 1:SUBLANES - 1 + tm] * w[0:1] + sc_ref[SUBLANES:SUBLANES + tm] * w[1:2]
         + sc_ref[SUBLANES + 1:SUBLANES + 1 + tm] * w[2:3] + cb_ref[...])
    u = _silu(y)
    bc = SSM_D_INNER + SSM_GROUPS * SSM_STATE
    xst_ref[0] = u[:, :SSM_D_INNER].T
    bn_ref[0] = u[:, SSM_D_INNER:bc].astype(BF16)
    ct_ref[0] = u[:, bc:].T.astype(BF16)
    d = px[SUBLANES:SUBLANES + tm, SSM_CONV_DIM:].T[:2 * SSM_HEADS]
    d = d + jnp.concatenate([dtb_ref[...]] * (tm // LANES), axis=1)
    dtt_ref[0] = jnp.maximum(d, 0.0) + jnp.log1p(jnp.exp(-jnp.abs(d)))

    rs = lax.rsqrt(ss * (1.0 / HEAD_DIM) + EPS)
    lane = lax.broadcasted_iota(jnp.int32, kx.shape, 1)
    swap = jnp.where((lane & half) == 0, pltpu.roll(kx, KV_DIM - half, 1), pltpu.roll(kx, half, 1))
    k_ref[0] = (rs * (kx * ck_ref[...] + swap * sk_ref[...])).astype(BF16)

    q3 = pt[:Q_DIM].reshape(ATTN_HEADS, HEAD_DIM, tm)
    qrs = lax.rsqrt(jnp.mean(q3 * q3, axis=1, keepdims=True) + EPS)
    qsw = jnp.concatenate([q3[:, half:], q3[:, :half]], axis=1)
    qo = qrs * (q3 * aq_ref[...][None] + qsw * bq_ref[...][None])
    qt_ref[0] = qo.reshape(Q_DIM, tm).astype(BF16)

    vt_ref[0] = pt[Q_DIM:].astype(BF16)
    z_ref[0] = pkz[:, KV_DIM:]


def _inproj(x, mod, per_batch_mod, nw, wx, wkz, wt, aq, bq, ck, sk, bd, cw, cb, dtb, tm):
    b, t, d = x.shape
    r8 = tm // SUBLANES
    nblk8 = t // SUBLANES
    gn = SSM_GROUPS * SSM_STATE
    mod_idx = (lambda bi, i: (bi, 0, 0)) if per_batch_mod else (lambda bi, i: (0, 0, 0))
    row = lambda bi, i: (bi, i, 0)
    col = lambda bi, i: (bi, 0, i)
    return pl.pallas_call(
        _inproj_kernel,
        grid=(b, t // tm),
        in_specs=[
            pl.BlockSpec((1, tm, d), row),
            pl.BlockSpec((1, SUBLANES, d), lambda bi, i: (bi, jnp.maximum(i * r8 - 1, 0), 0)),
            pl.BlockSpec((1, SUBLANES, d), lambda bi, i: (bi, jnp.minimum((i + 1) * r8, nblk8 - 1), 0)),
            pl.BlockSpec((1, N_MOD, d), mod_idx),
            _const_spec((1, d)),
            _const_spec((d, XBCDT_W)),
            _const_spec((d, KV_DIM + SSM_D_INNER)),
            _const_spec((Q_DIM + KV_DIM, d)),
            pl.BlockSpec((HEAD_DIM, tm), lambda bi, i: (0, i)),
            pl.BlockSpec((HEAD_DIM, tm), lambda bi, i: (0, i)),
            pl.BlockSpec((tm, KV_DIM), lambda bi, i: (i, 0)),
            pl.BlockSpec((tm, KV_DIM), lambda bi, i: (i, 0)),
            _const_spec((KV_DIM, KV_DIM)),
            _const_spec((3, SSM_CONV_DIM)),
            _const_spec((1, SSM_CONV_DIM)),
            _const_spec((2 * SSM_HEADS, LANES)),
        ],
        out_specs=[
            pl.BlockSpec((1, Q_DIM, tm), col),
            pl.BlockSpec((1, KV_DIM, tm), col),
            pl.BlockSpec((1, tm, KV_DIM), row),
            pl.BlockSpec((1, tm, SSM_D_INNER), row),
            pl.BlockSpec((1, SSM_D_INNER, tm), col),
            pl.BlockSpec((1, tm, gn), row),
            pl.BlockSpec((1, gn, tm), col),
            pl.BlockSpec((1, 2 * SSM_HEADS, tm), col),
        ],
        out_shape=[
            jax.ShapeDtypeStruct((b, Q_DIM, t), BF16),
            jax.ShapeDtypeStruct((b, KV_DIM, t), BF16),
            jax.ShapeDtypeStruct((b, t, KV_DIM), BF16),
            jax.ShapeDtypeStruct((b, t, SSM_D_INNER), F32),
            jax.ShapeDtypeStruct((b, SSM_D_INNER, t), F32),
            jax.ShapeDtypeStruct((b, t, gn), BF16),
            jax.ShapeDtypeStruct((b, gn, t), BF16),
            jax.ShapeDtypeStruct((b, 2 * SSM_HEADS, t), F32),
        ],
        scratch_shapes=[pltpu.VMEM((tm + 2 * SUBLANES, SSM_CONV_DIM), F32)],
        compiler_params=_cparams(2),
        name="inproj",
    )(x, x, x, mod, nw, wx, wkz, wt, aq, bq, ck, sk, bd, cw, cb, dtb)


def _attn_kernel(qt_ref, k_ref, vt_ref, o_ref, s0_ref, s1_ref, p0_ref, p1_ref, m0_ref, m1_ref, *, nq):
    g = pl.program_id(1)
    i = pl.program_id(2)
    tq = qt_ref.shape[2]
    cols = KV_GROUP * tq
    nk, kt = s0_ref.shape[0], s0_ref.shape[1]
    s_refs, m_refs, p_refs = (s0_ref, s1_ref), (m0_ref, m1_ref), (p0_ref, p1_ref)

    def run(a_buf, b_buf):
        do_a, do_bc = a_buf is not None, b_buf is not None
        if do_a:
            q4 = qt_ref[0]
            qcat = jnp.concatenate([q4[h * HEAD_DIM:(h + 1) * HEAD_DIM] for h in range(KV_GROUP)], axis=1)
            qp = jnp.concatenate([qcat] * ATTN_KV_HEADS, axis=0)
            rowgrp = lax.broadcasted_iota(jnp.int32, qp.shape, 0) // HEAD_DIM
            qp = jnp.where(rowgrp == g, qp, jnp.zeros_like(qp))
        if do_bc:
            m8 = jnp.broadcast_to(jnp.max(m_refs[b_buf][...], axis=0, keepdims=True), (SUBLANES, cols))

        def stage_a(j, mx):
            ktile = k_ref[0, pl.ds(pl.multiple_of(j * kt, kt), kt), :]
            s = _dot(ktile, qp)
            s_refs[a_buf][j] = s
            return jnp.maximum(mx, jnp.max(s.reshape(kt // SUBLANES, SUBLANES, cols), axis=0))

        def stage_b(j, par, ls):
            s = s_refs[b_buf][j].reshape(kt // SUBLANES, SUBLANES, cols)
            p = jnp.exp2(s - m8[None])
            p_refs[par][...] = p.reshape(kt, cols).astype(BF16)
            return ls + jnp.sum(p, axis=0)

        def stage_c(j, par, acc):
            return acc + _dot(vt_ref[0, 0, j], p_refs[par][...])

        def tile(j, par, carry):
            mx, ls, acc = carry
            if do_a:
                mx = stage_a(j, mx)
            if do_bc:
                acc = stage_c(j - 1, 1 - par, acc)
                ls = stage_b(j, par, ls)
            return mx, ls, acc

        mx = jnp.full((SUBLANES, cols), NEG_BIG, F32)
        ls = jnp.zeros((SUBLANES, cols), F32)
        acc = jnp.zeros((HEAD_DIM, cols), F32)
        if do_a:
            mx = stage_a(0, mx)
        if do_bc:
            ls = stage_b(0, 0, ls)

        def body(t, carry):
            j = 1 + 2 * t
            return tile(j + 1, 0, tile(j, 1, carry))

        carry = lax.fori_loop(0, (nk - 1) // 2, body, (mx, ls, acc), unroll=True)
        if (nk - 1) % 2:
            carry = tile(nk - 1, (nk - 1) % 2, carry)
        mx, ls, acc = carry
        if do_a:
            m_refs[a_buf][...] = mx
        if do_bc:
            acc = stage_c(nk - 1, (nk - 1) % 2, acc)
            o = acc * (1.0 / jnp.sum(ls, axis=0, keepdims=True))
            o4 = jnp.concatenate([o[:, h * tq:(h + 1) * tq] for h in range(KV_GROUP)], axis=0)
            o_ref[0] = o4.T.astype(BF16)

    @pl.when(i == 0)
    def _():
        run(0, None)

    inner = jnp.logical_and(i > 0, i < nq)

    @pl.when(jnp.logical_and(inner, i % 2 == 1))
    def _():
        run(1, 0)

    @pl.when(jnp.logical_and(inner, i % 2 == 0))
    def _():
        run(0, 1)

    @pl.when(i == nq)
    def _():
        run(None, (nq - 1) % 2)


def _attention(qt, k_all, vt_tiles, tq):
    b, _, t = qt.shape
    tk = k_all.shape[1]
    nk, kt = vt_tiles.shape[2], vt_tiles.shape[4]
    nq = t // tq
    gw = KV_GROUP * HEAD_DIM
    cols = KV_GROUP * tq
    return pl.pallas_call(
        functools.partial(_attn_kernel, nq=nq),
        grid=(b, ATTN_KV_HEADS, nq + 1),
        in_specs=[
            pl.BlockSpec((1, gw, tq), lambda bi, g, i: (bi, g, jnp.minimum(i, nq - 1))),
            pl.BlockSpec((1, tk, KV_DIM), lambda bi, g, i: (bi, 0, 0)),
            pl.BlockSpec((1, 1, nk, HEAD_DIM, kt), lambda bi, g, i: (bi, g, 0, 0, 0)),
        ],
        out_specs=pl.BlockSpec((1, tq, gw), lambda bi, g, i: (bi, jnp.maximum(i - 1, 0), g)),
        out_shape=jax.ShapeDtypeStruct((b, t, Q_DIM), BF16),
        scratch_shapes=(
            [pltpu.VMEM((nk, kt, cols), F32)] * 2
            + [pltpu.VMEM((kt, cols), BF16)] * 2
            + [pltpu.VMEM((SUBLANES, cols), F32)] * 2
        ),
        compiler_params=pltpu.CompilerParams(dimension_semantics=("parallel", "parallel", "arbitrary"),
                                             vmem_limit_bytes=VMEM_LIMIT),
        name="attn",
    )(qt, k_all, vt_tiles)


def _attn_bounded_kernel(qt_ref, k_ref, vt_ref, km_ref, o_ref, *p_refs):
    g = pl.program_id(1)
    cols = p_refs[0].shape[1]
    tq = cols // KV_GROUP
    n_blk = qt_ref.shape[2] // tq
    sub = p_refs[0].shape[0]
    n_all = k_ref.shape[1] // sub

    kmax = jnp.concatenate([km_ref[0:1, :]] * (cols // LANES), axis=1)
    qps, mb8s = [], []
    for blk in range(n_blk):
        q4 = qt_ref[0, :, blk * tq:(blk + 1) * tq]
        qf = q4.astype(F32)
        qn = jnp.sqrt(jnp.sum((qf * qf).reshape(KV_GROUP, HEAD_DIM, tq), axis=1))
        qn = jnp.concatenate([qn[h:h + 1] for h in range(KV_GROUP)], axis=1)
        mb8s.append(jnp.broadcast_to(qn * kmax * SCORE_BOUND_SLACK, (SUBLANES, cols)))
        qcat = jnp.concatenate([q4[h * HEAD_DIM:(h + 1) * HEAD_DIM] for h in range(KV_GROUP)], axis=1)
        qp = jnp.concatenate([qcat] * ATTN_KV_HEADS, axis=0)
        rowgrp = lax.broadcasted_iota(jnp.int32, qp.shape, 0) // HEAD_DIM
        qps.append(jnp.where(rowgrp == g, qp, jnp.zeros_like(qp)))

    ring = len(p_refs)
    lag = ring - 1
    ls = [jnp.zeros((SUBLANES, cols), F32) for _ in range(n_blk)]
    acc = [jnp.zeros((HEAD_DIM, cols), F32) for _ in range(n_blk)]

    def pv(w):
        blk, u = divmod(w, n_all)
        acc[blk] = acc[blk] + _dot(vt_ref[0, :, u * sub:(u + 1) * sub], p_refs[w % ring][...])
        if u == n_all - 1:
            o = acc[blk] * (1.0 / jnp.sum(ls[blk], axis=0, keepdims=True))
            o4 = jnp.concatenate([o[:, h * tq:(h + 1) * tq] for h in range(KV_GROUP)], axis=0)
            o_ref[0, blk * tq:(blk + 1) * tq, :] = o4.T.astype(BF16)

    n_items = n_blk * n_all
    for w in range(n_items):
        blk, u = divmod(w, n_all)
        s = _dot(k_ref[0, u * sub:(u + 1) * sub, :], qps[blk])
        if w >= lag:
            pv(w - lag)
        p = jnp.exp2(s.reshape(sub // SUBLANES, SUBLANES, cols) - mb8s[blk][None])
        p_refs[w % ring][...] = p.reshape(sub, cols).astype(BF16)
        ls[blk] = ls[blk] + jnp.sum(p, axis=0)
    for w in range(max(n_items - lag, 0), n_items):
        pv(w)


def _attention_bounded(qt, k_all, vt_all, kmax, tq):
    b, _, t = qt.shape
    tk = k_all.shape[1]
    sub = next(c for c in (ATTN_SUB_TILE, LANES) if tk % c == 0)
    gw = KV_GROUP * HEAD_DIM
    cols = KV_GROUP * tq
    tstep = tq * (ATTN_Q_BLOCKS if t % (tq * ATTN_Q_BLOCKS) == 0 else 1)
    return pl.pallas_call(
        _attn_bounded_kernel,
        grid=(b, ATTN_KV_HEADS, t // tstep),
        in_specs=[
            pl.BlockSpec((1, gw, tstep), lambda bi, g, i: (bi, g, i)),
            pl.BlockSpec((1, tk, KV_DIM), lambda bi, g, i: (bi, 0, 0)),
            pl.BlockSpec((1, HEAD_DIM, tk), lambda bi, g, i: (bi, g, 0)),
            _const_spec((SUBLANES, LANES)),
        ],
        out_specs=pl.BlockSpec((1, tstep, gw), lambda bi, g, i: (bi, i, g)),
        out_shape=jax.ShapeDtypeStruct((b, t, Q_DIM), BF16),
        scratch_shapes=[pltpu.VMEM((sub, cols), BF16)] * (ATTN_PV_LAG + 1),
        compiler_params=_cparams(3),
        name="attn_bounded",
    )(qt, k_all, vt_all, jnp.full((SUBLANES, LANES), kmax, F32))


def _attention_exact(qt, k_all, vt_all, kmax, tq):
    del kmax
    b, _, tk = vt_all.shape
    kt = next(c for c in ATTN_KEY_TILES if tk % c == 0)
    vt_tiles = vt_all.reshape(b, ATTN_KV_HEADS, HEAD_DIM, tk // kt, kt).transpose(0, 1, 3, 2, 4)
    return _attention(qt, k_all, vt_tiles, tq)


class _SsdChunk:
    def __init__(self, xst, bn, ct, dtt, a_col, direction):
        self.xst, self.bn, self.ct, self.dtt, self.a_col, self.direction = xst, bn, ct, dtt, a_col, direction

    def decay(self):
        ln = self.xst.shape[1]
        da = self.dtt * self.a_col
        ri = lax.broadcasted_iota(jnp.int32, (ln, ln), 0)
        ci = lax.broadcasted_iota(jnp.int32, (ln, ln), 1)
        self.reach = (ci >= ri) if self.direction == 0 else (ci <= ri)
        self.cs_row = _dot_exact(da, self.reach.astype(F32))
        self.tot = _dot_exact(da, jnp.ones((ln, ln), F32))

    def local(self):
        ln = self.xst.shape[1]
        gn = SSM_GROUPS * SSM_STATE
        cs_col = jnp.concatenate([self.cs_row, jnp.zeros((ln - 2 * SSM_HEADS, ln), F32)], axis=0).T
        rowgrp = lax.broadcasted_iota(jnp.int32, (gn, ln), 0) // SSM_STATE
        self.ctz, self.y_diag, self.xw, self.e_row, self.e_tot = [], [], [], [], []
        for grp in range(SSM_GROUPS):
            ctz = jnp.where(rowgrp == grp, self.ct, jnp.zeros_like(self.ct))
            cbt = _dot(self.bn, ctz)
            y_parts, xw_parts = [], []
            for e in range(SSM_HEADS_PER_GROUP):
                hd = grp * SSM_HEADS_PER_GROUP + e
                c = self.direction * SSM_HEADS + hd
                row = self.cs_row[c:c + 1, :]
                lmt = jnp.exp2(jnp.where(self.reach, row - cs_col[:, c:c + 1], NEG_BIG))
                mt = (cbt * lmt).astype(BF16)
                xdt = self.xst[hd * SSM_HEAD_DIM:(hd + 1) * SSM_HEAD_DIM, :] * self.dtt[c:c + 1, :]
                y_parts.append(_dot(xdt.astype(BF16), mt))
                xw_parts.append((xdt * jnp.exp2(self.tot[c:c + 1, :] - row)).astype(BF16))
                self.e_row.append(jnp.exp2(row))
                self.e_tot.append(jnp.exp2(self.tot[c:c + 1, :gn]))
            self.ctz.append(ctz)
            self.y_diag.append(y_parts)
            self.xw.append(jnp.concatenate(xw_parts, axis=0))

    def carry(self, h_ref):
        gp = SSM_HEADS_PER_GROUP * SSM_HEAD_DIM
        ys = []
        for grp in range(SSM_GROUPS):
            hg = h_ref[grp * gp:(grp + 1) * gp, :]
            y_off = _dot(hg.astype(BF16), self.ctz[grp])
            st = _dot(self.xw[grp], self.bn)
            h_parts = []
            for e in range(SSM_HEADS_PER_GROUP):
                hd = grp * SSM_HEADS_PER_GROUP + e
                rows = slice(e * SSM_HEAD_DIM, (e + 1) * SSM_HEAD_DIM)
                ys.append(self.y_diag[grp][e] + y_off[rows] * self.e_row[hd])
                h_parts.append(hg[rows] * self.e_tot[hd])
            h_ref[grp * gp:(grp + 1) * gp, :] = jnp.concatenate(h_parts, axis=0) + st
        return jnp.concatenate(ys, axis=0)


def _ssd_run(chunks, h_refs):
    for ch in chunks:
        ch.decay()
    for ch in chunks:
        ch.local()
    return [ch.carry(h_refs[ch.direction]) for ch in chunks]


def _ssd_kernel(xf_ref, xb_ref, bf_ref, bb_ref, cf_ref, cb_ref, df_ref, db_ref, xc_ref, bc_ref, cc_ref, dc_ref,
                alog_ref, dsk_ref, yf_ref, yb_ref, hf_ref, hb_ref):
    c = pl.program_id(1)
    a_col = -jnp.exp(alog_ref[...]) * LOG2_E
    ln = SSM_CHUNK
    h_refs = (hf_ref, hb_ref)

    def chunk(x_ref, b_ref, c_ref, d_ref, s, direction):
        tok = slice(s * ln, (s + 1) * ln)
        return _SsdChunk(x_ref[0, :, tok], b_ref[0, tok, :], c_ref[0, :, tok], d_ref[0, :, tok], a_col, direction)

    @pl.when(c == 0)
    def _():
        n_ctx = xc_ref.shape[2] // ln
        for direction in range(2):
            h_refs[direction][...] = jnp.zeros_like(h_refs[direction])
        chunks = []
        for k in range(n_ctx):
            chunks.append(chunk(xc_ref, bc_ref, cc_ref, dc_ref, k, 0))
            chunks.append(chunk(xc_ref, bc_ref, cc_ref, dc_ref, n_ctx - 1 - k, 1))
        _ssd_run(chunks, h_refs)

    n_sub = xf_ref.shape[2] // ln
    chunks = []
    for k in range(n_sub):
        chunks.append(chunk(xf_ref, bf_ref, cf_ref, df_ref, k, 0))
        chunks.append(chunk(xb_ref, bb_ref, cb_ref, db_ref, n_sub - 1 - k, 1))
    ys = _ssd_run(chunks, h_refs)
    for k in range(n_sub):
        tok = slice(k * ln, (k + 1) * ln)
        yf_ref[0, :, tok] = ys[2 * k] + xf_ref[0, :, tok] * dsk_ref[...]
        tok = slice((n_sub - 1 - k) * ln, (n_sub - k) * ln)
        yb_ref[0, :, tok] = ys[2 * k + 1]


def _ssd(lat, ctx, alog_col, dsk_col, toks):
    xst, bn, ct, dtt = lat
    b, di, t = xst.shape
    gn = bn.shape[2]
    nh2 = dtt.shape[1]
    nctx = ctx[0].shape[2]
    ns = t // toks
    fcol = lambda bi, c: (bi, 0, c)
    bcol = lambda bi, c: (bi, 0, ns - 1 - c)
    frow = lambda bi, c: (bi, c, 0)
    brow = lambda bi, c: (bi, ns - 1 - c, 0)
    whole = lambda bi, c: (bi, 0, 0)
    return pl.pallas_call(
        _ssd_kernel,
        grid=(b, ns),
        in_specs=[
            pl.BlockSpec((1, di, toks), fcol), pl.BlockSpec((1, di, toks), bcol),
            pl.BlockSpec((1, toks, gn), frow), pl.BlockSpec((1, toks, gn), brow),
            pl.BlockSpec((1, gn, toks), fcol), pl.BlockSpec((1, gn, toks), bcol),
            pl.BlockSpec((1, nh2, toks), fcol), pl.BlockSpec((1, nh2, toks), bcol),
            pl.BlockSpec((1, di, nctx), whole), pl.BlockSpec((1, nctx, gn), whole),
            pl.BlockSpec((1, gn, nctx), whole), pl.BlockSpec((1, nh2, nctx), whole),
            _const_spec((nh2, SSM_CHUNK)),
            _const_spec((di, SSM_CHUNK)),
        ],
        out_specs=[pl.BlockSpec((1, di, toks), fcol), pl.BlockSpec((1, di, toks), bcol)],
        out_shape=[jax.ShapeDtypeStruct((b, di, t), F32), jax.ShapeDtypeStruct((b, di, t), F32)],
        scratch_shapes=[pltpu.VMEM((di, gn), F32)] * 2,
        compiler_params=pltpu.CompilerParams(dimension_semantics=("parallel", "arbitrary"),
                                             vmem_limit_bytes=VMEM_LIMIT),
        name="ssd",
    )(xst, xst, bn, bn, ct, ct, dtt, dtt, *ctx, alog_col, dsk_col)


def _row_groups(tm):
    n = ROW_GROUPS if tm % (ROW_GROUPS * SUBLANES * 2) == 0 else 1
    return [slice(i * (tm // n), (i + 1) * (tm // n)) for i in range(n)]


def _ffn_tail(x1s, m, nw, wg_ref, wu_ref, wd_ref):
    scale = nw * (1.0 + m[4:5])
    hs = [(_rms(x1) * scale + m[3:4]).astype(BF16) for x1 in x1s]
    ffn = wg_ref.shape[1]
    fs = [None] * len(x1s)
    for c0 in range(0, ffn, FFN_CHUNK):
        c1 = min(c0 + FFN_CHUNK, ffn)
        hids = [(_silu(_dot(h, wg_ref[:, c0:c1])) * _dot(h, wu_ref[:, c0:c1])).astype(BF16) for h in hs]
        for i, hid in enumerate(hids):
            part = _dot(hid, wd_ref[c0:c1, :])
            fs[i] = part if fs[i] is None else fs[i] + part
    return [x1 + m[5:6] * f for x1, f in zip(x1s, fs)]


def _mix0_kernel(x_ref, a_ref, yf_ref, yb_ref, z_ref, mod_ref, snw_ref, nw_ref,
                 wo_ref, wg_ref, wu_ref, wd_ref, o_ref):
    m = mod_ref[0]
    gw = SSM_D_INNER // SSM_GROUPS
    groups = _row_groups(x_ref.shape[1])
    yns = []
    for rows in groups:
        y = (yf_ref[0, :, rows] + yb_ref[0, :, rows]).T
        gy = y * _silu(z_ref[0, rows, :])
        gn = jnp.concatenate([_rms(gy[:, i * gw:(i + 1) * gw]) for i in range(SSM_GROUPS)], axis=1)
        yns.append((gn * snw_ref[...]).astype(BF16))
    mix_a = [_dot(a_ref[0, rows, :], wo_ref[:Q_DIM, :]) for rows in groups]
    x1s = []
    for rows, yn, ma in zip(groups, yns, mix_a):
        x1s.append(x_ref[0, rows, :] + m[2:3] * (ma + _dot(yn, wo_ref[Q_DIM:, :])))
    for rows, out in zip(groups, _ffn_tail(x1s, m, nw_ref[...], wg_ref, wu_ref, wd_ref)):
        o_ref[0, rows, :] = out


def _mix0(x, a, yf, yb, z, mod, snw, nw, wo, wg, wu, wd, tm):
    b, t, d = x.shape
    ffn = wg.shape[2]
    row = lambda bi, i: (bi, i, 0)
    col = lambda bi, i: (bi, 0, i)
    return pl.pallas_call(
        _mix0_kernel,
        grid=(b, t // tm),
        in_specs=[
            pl.BlockSpec((1, tm, d), row),
            pl.BlockSpec((1, tm, Q_DIM), row),
            pl.BlockSpec((1, SSM_D_INNER, tm), col),
            pl.BlockSpec((1, SSM_D_INNER, tm), col),
            pl.BlockSpec((1, tm, SSM_D_INNER), row),
            pl.BlockSpec((1, N_MOD, d), lambda bi, i: (bi, 0, 0)),
            _const_spec((1, SSM_D_INNER)),
            _const_spec((1, d)),
            _const_spec((Q_DIM + SSM_D_INNER, d)),
            _layer_spec((d, ffn), 0),
            _layer_spec((d, ffn), 0),
            _layer_spec((ffn, d), 0),
        ],
        out_specs=pl.BlockSpec((1, tm, d), row),
        out_shape=jax.ShapeDtypeStruct((b, t, d), F32),
        compiler_params=_cparams(2),
        name="mix0",
    )(x, a, yf, yb, z, mod, snw, nw, wo, wg, wu, wd)


def _layer1_kernel(x_ref, xp_ref, xn_ref, mod_ref, nmix_ref, win_ref, cw_ref, wout_ref, nffn_ref,
                   wg_ref, wu_ref, wd_ref, fn_ref, o_ref, v_ref):
    i = pl.program_id(1)
    last = pl.num_programs(1) - 1
    m = mod_ref[0]
    tm, d = x_ref.shape[1], x_ref.shape[2]
    x = x_ref[0]
    xa = jnp.concatenate([xp_ref[0], x, xn_ref[0]], axis=0)
    h = _rms(xa) * (nmix_ref[...] * (1.0 + m[1:2])) + m[0:1]
    pcu = _dot(h.astype(BF16), win_ref[:, d:])
    pgb = _dot(h[SUBLANES:SUBLANES + tm].astype(BF16), win_ref[:, :d])
    v = pcu[:, :d] * pcu[:, d:]
    v_ref[0:SUBLANES] = jnp.where(i > 0, v[:SUBLANES], 0.0)
    v_ref[SUBLANES:SUBLANES + tm] = v[SUBLANES:SUBLANES + tm]
    v_ref[SUBLANES + tm:] = jnp.where(i < last, v[SUBLANES + tm:], 0.0)
    cw = cw_ref[...]
    groups = _row_groups(tm)
    x1s = []
    for rows in groups:
        r0 = rows.start + SUBLANES
        n = rows.stop - rows.start
        conv = (v_ref[r0 - 1:r0 - 1 + n] * cw[0:1] + v_ref[r0:r0 + n] * cw[1:2] + v_ref[r0 + 1:r0 + 1 + n] * cw[2:3])
        gated = (pgb[rows] * conv).astype(BF16)
        x1s.append(x[rows] + m[2:3] * _dot(gated, wout_ref[...]))
    for rows, x2 in zip(groups, _ffn_tail(x1s, m, nffn_ref[...], wg_ref, wu_ref, wd_ref)):
        o_ref[0, rows, :] = _rms(x2) * fn_ref[...]


def _layer1(x, mod, nmix, win, cw, wout, nffn, wg, wu, wd, fn, tm):
    b, t, d = x.shape
    ffn = wg.shape[2]
    r8 = tm // SUBLANES
    nblk8 = t // SUBLANES
    return pl.pallas_call(
        _layer1_kernel,
        grid=(b, t // tm),
        in_specs=[
            pl.BlockSpec((1, tm, d), lambda bi, i: (bi, i, 0)),
            pl.BlockSpec((1, SUBLANES, d), lambda bi, i: (bi, jnp.maximum(i * r8 - 1, 0), 0)),
            pl.BlockSpec((1, SUBLANES, d), lambda bi, i: (bi, jnp.minimum((i + 1) * r8, nblk8 - 1), 0)),
            pl.BlockSpec((1, N_MOD, d), lambda bi, i: (bi, 0, 0)),
            _const_spec((1, d)),
            _const_spec((d, 3 * d)),
            _const_spec((3, d)),
            _const_spec((d, d)),
            _const_spec((1, d)),
            _layer_spec((d, ffn), 1),
            _layer_spec((d, ffn), 1),
            _layer_spec((ffn, d), 1),
            _const_spec((1, d)),
        ],
        out_specs=pl.BlockSpec((1, tm, d), lambda bi, i: (bi, i, 0)),
        out_shape=jax.ShapeDtypeStruct((b, t, d), F32),
        scratch_shapes=[pltpu.VMEM((tm + 2 * SUBLANES, d), F32)],
        compiler_params=_cparams(2),
        name="layer1",
    )(x, x, x, mod, nmix, win, cw, wout, nffn, wg, wu, wd, fn)


def _rope_tables(n_tokens):
    rows = n_tokens // GRID_W
    row = jnp.repeat(jnp.arange(rows), GRID_W).astype(F32)
    col = jnp.tile(jnp.arange(GRID_W), rows).astype(F32)
    inv = 1.0 / (ROPE_THETA ** (jnp.arange(0, AXIS_DIM, 2, dtype=F32) / AXIS_DIM))
    ang = jnp.concatenate([row[:, None] * inv, col[:, None] * inv], axis=-1)
    return jnp.cos(ang), jnp.sin(ang)


_HEAD_PERM = np.concatenate([np.arange(0, HEAD_DIM, 2), np.arange(1, HEAD_DIM, 2)])


def _qk_tables(cos, sin, q_gain, k_gain):
    half = HEAD_DIM // 2
    sign = jnp.concatenate([-jnp.ones((half,), F32), jnp.ones((half,), F32)])
    c64 = jnp.concatenate([cos, cos], axis=1)
    s64 = jnp.concatenate([sin, sin], axis=1) * sign
    swap = np.concatenate([np.arange(half, HEAD_DIM), np.arange(0, half)])

    def tables(gain, scale):
        gp = gain[_HEAD_PERM] * scale
        return c64 * gp[None, :], s64 * gp[swap][None, :]

    aq, bq = tables(q_gain, HEAD_DIM ** -0.5 * LOG2_E)
    ak, bk = tables(k_gain, 1.0)
    return aq.T, bq.T, jnp.tile(ak, (1, ATTN_KV_HEADS)), jnp.tile(bk, (1, ATTN_KV_HEADS))


def kernel(x, c, ctx, c_ctx, ada_w, ada_b, norm_mix, norm_ffn, ffn_w_gate, ffn_w_up, ffn_w_down, hy_w_in,
           hy_q_norm, hy_k_norm, hy_conv_w, hy_conv_b, hy_dt_bias, hy_a_log, hy_d_skip, hy_ssm_norm, hy_w_out,
           sc_w_in, sc_conv_w, sc_w_out, final_norm):
    b, t, d = x.shape
    nctx = ctx.shape[1]
    tm_in = min(1024, t)
    tm_mix = min(512, t)

    n_rows = -(-(b + 1) // SUBLANES) * SUBLANES
    cc = jnp.zeros((n_rows, d), F32).at[:b].set(c).at[b].set(c_ctx)
    mods = _modulation(cc, ada_w, ada_b).reshape(ada_w.shape[0], n_rows, N_MOD, d)
    mod0, mod0_ctx, mod1 = mods[0, :b], mods[0, b:b + 1], mods[1, :b]

    w_in = hy_w_in[0]
    idx_k, idx_v, idx_z = Q_DIM, Q_DIM + KV_DIM, Q_DIM + 2 * KV_DIM
    idx_xbc = idx_z + SSM_D_INNER
    q_cols = (np.arange(ATTN_HEADS)[:, None] * HEAD_DIM + _HEAD_PERM[None, :]).reshape(-1)
    k_cols = idx_k + (np.arange(ATTN_KV_HEADS)[:, None] * HEAD_DIM + _HEAD_PERM[None, :]).reshape(-1)
    dt_pad = jnp.zeros((d, LANES - 2 * SSM_HEADS), F32)
    wkz = jnp.concatenate([w_in[:, k_cols], w_in[:, idx_z:idx_xbc]], axis=1).astype(BF16)
    wx = jnp.concatenate([w_in[:, idx_xbc:], dt_pad], axis=1).astype(BF16)
    wt = jnp.concatenate([w_in[:, q_cols], w_in[:, idx_v:idx_z]], axis=1).T.astype(BF16)
    nw0 = norm_mix[0][None, :]

    cos, sin = _rope_tables(t)
    aq, bq, ck, sk = _qk_tables(cos, sin, hy_q_norm[0], hy_k_norm[0])
    ones_c, zeros_c = jnp.ones((nctx, AXIS_DIM), F32), jnp.zeros((nctx, AXIS_DIM), F32)
    aq_c, bq_c, ck_c, sk_c = _qk_tables(ones_c, zeros_c, hy_q_norm[0], hy_k_norm[0])
    seg = np.arange(KV_DIM) // HEAD_DIM
    bd = jnp.asarray(seg[:, None] == seg[None, :], BF16)

    conv_w, conv_b = hy_conv_w[0], hy_conv_b[0][None, :]
    dtb = jnp.broadcast_to(hy_dt_bias[0].reshape(-1, 1), (2 * SSM_HEADS, LANES))
    shared = (nw0, wx, wkz, wt)
    qt, vt, k, z, *ssd_lat = _inproj(x, mod0, True, *shared, aq, bq, ck, sk, bd, conv_w, conv_b, dtb, tm_in)
    _, vct, kc, _, *ssd_ctx = _inproj(ctx, mod0_ctx, False, *shared, aq_c, bq_c, ck_c, sk_c, bd, conv_w, conv_b, dtb,
                                      nctx)

    k_all = jnp.concatenate([k, kc], axis=1)
    vt_all = jnp.concatenate([vt, vct], axis=2)
    kmax = 1.01 * HEAD_DIM ** 0.5 * jnp.max(jnp.abs(hy_k_norm[0]))
    qmax = 1.01 * LOG2_E * jnp.max(jnp.abs(hy_q_norm[0]))
    score_bound = qmax * kmax
    a_lat = lax.cond(score_bound <= SAFE_SCORE_BOUND,
                     functools.partial(_attention_bounded, tq=min(ATTN_TQ_BOUNDED, t)),
                     functools.partial(_attention_exact, tq=min(ATTN_TQ_EXACT, t)),
                     qt, k_all, vt_all, kmax)

    alog_col = jnp.broadcast_to(hy_a_log[0].reshape(-1, 1), (2 * SSM_HEADS, SSM_CHUNK))
    dsk_col = jnp.broadcast_to(jnp.repeat(hy_d_skip[0], SSM_HEAD_DIM)[:, None], (SSM_D_INNER, SSM_CHUNK))
    yf, yb = _ssd(ssd_lat, ssd_ctx, alog_col, dsk_col, toks=min(8 * SSM_CHUNK, t))

    wg, wu, wd = ffn_w_gate.astype(BF16), ffn_w_up.astype(BF16), ffn_w_down.astype(BF16)
    x = _mix0(x, a_lat, yf, yb, z, mod0, hy_ssm_norm[0][None, :], norm_ffn[0][None, :], hy_w_out[0].astype(BF16),
              wg, wu, wd, tm_mix)

    return _layer1(x, mod1, norm_mix[1][None, :], sc_w_in[0].astype(BF16), sc_conv_w[0], sc_w_out[0].astype(BF16),
                   norm_ffn[1][None, :], wg, wu, wd, final_norm[None, :], tm_mix)
```

```python
import functools

import jax
import jax.numpy as jnp
import numpy as np
from jax import lax
from jax.experimental import pallas as pl
from jax.experimental.pallas import tpu as pltpu

F32 = jnp.float32
BF16 = jnp.bfloat16

EPS = 1e-6
N_MOD = 6
GRID_W = 64
ROPE_THETA = 10000.0

ATTN_HEADS = 8
ATTN_KV_HEADS = 2
HEAD_DIM = 64
AXIS_DIM = HEAD_DIM // 2
KV_GROUP = ATTN_HEADS // ATTN_KV_HEADS
Q_DIM = ATTN_HEADS * HEAD_DIM
KV_DIM = ATTN_KV_HEADS * HEAD_DIM

SSM_HEADS = 8
SSM_HEAD_DIM = 64
SSM_D_INNER = SSM_HEADS * SSM_HEAD_DIM
SSM_GROUPS = 2
SSM_STATE = 64
SSM_CHUNK = 128
SSM_CONV_DIM = SSM_D_INNER + 2 * SSM_GROUPS * SSM_STATE
SSM_HEADS_PER_GROUP = SSM_HEADS // SSM_GROUPS

LANES = 128
SUBLANES = 8
MXU_TILE = 256
VMEM_LIMIT = 56 * 1024 * 1024

XBCDT_W = SSM_CONV_DIM + LANES
NEG_BIG = -1e30
LOG2_E = 1.4426950408889634

ROW_GROUPS = 2
FFN_CHUNK = 3 * MXU_TILE
ATTN_KEY_TILES = (768, 512, 256, 128)
ATTN_TQ_EXACT = 128
ATTN_TQ_BOUNDED = 128
ATTN_Q_BLOCKS = 16
ATTN_SUB_TILE = MXU_TILE
ATTN_PV_LAG = 3
SAFE_SCORE_BOUND = 40.0
SCORE_BOUND_SLACK = 1.001


def _cparams(n_axes):
    return pltpu.CompilerParams(dimension_semantics=("parallel",) * n_axes, vmem_limit_bytes=VMEM_LIMIT)


def _const_spec(shape):
    nd = len(shape)
    return pl.BlockSpec(shape, lambda *_: (0,) * nd, pipeline_mode=pl.Buffered(1))


def _layer_spec(shape, layer):
    nd = len(shape)
    return pl.BlockSpec((None,) + tuple(shape), lambda *_: (layer,) + (0,) * nd, pipeline_mode=pl.Buffered(1))


def _rms(x):
    return x * lax.rsqrt(jnp.mean(x * x, axis=-1, keepdims=True) + EPS)


def _silu(x):
    return x * (1.0 / (1.0 + jnp.exp(-x)))


def _dot(a, b):
    return jnp.dot(a, b, preferred_element_type=F32)


def _dot_exact(a, b):
    return jnp.dot(a, b, preferred_element_type=F32, precision=lax.Precision.HIGHEST)


def _mod_kernel(c_ref, w_ref, b_ref, o_ref):
    c = c_ref[...]
    o_ref[0] = _dot_exact(_silu(c), w_ref[0]) + b_ref[0]


def _modulation(cc, ada_w, ada_b):
    depth, d, nd = ada_w.shape
    r = cc.shape[0]
    tn = d
    return pl.pallas_call(
        _mod_kernel,
        grid=(depth, nd // tn),
        in_specs=[
            pl.BlockSpec((r, d), lambda l, j: (0, 0)),
            pl.BlockSpec((1, d, tn), lambda l, j: (l, 0, j)),
            pl.BlockSpec((1, 1, tn), lambda l, j: (l, 0, j)),
        ],
        out_specs=pl.BlockSpec((1, r, tn), lambda l, j: (l, 0, j)),
        out_shape=jax.ShapeDtypeStruct((depth, r, nd), F32),
        compiler_params=_cparams(2),
        name="mod",
    )(cc, ada_w, ada_b.reshape(depth, 1, nd))


def _inproj_kernel(x_ref, xp_ref, xn_ref, mod_ref, nw_ref, wx_ref, wkz_ref, wt_ref, aq_ref, bq_ref, ck_ref, sk_ref,
                   bd_ref, cw_ref, cb_ref, dtb_ref,
                   qt_ref, vt_ref, k_ref, z_ref, xst_ref, bn_ref, ct_ref, dtt_ref, sc_ref):
    i = pl.program_id(1)
    last = pl.num_programs(1) - 1
    m = mod_ref[0]
    tm = x_ref.shape[1]
    half = HEAD_DIM // 2
    xa = jnp.concatenate([xp_ref[0], x_ref[0], xn_ref[0]], axis=0)
    h = _rms(xa) * (nw_ref[...] * (1.0 + m[1:2])) + m[0:1]
    px = _dot(h.astype(BF16), wx_ref[...])
    hc = h[SUBLANES:SUBLANES + tm].astype(BF16)
    pkz = _dot(hc, wkz_ref[...])
    pt = lax.dot_general(wt_ref[...], hc, (((1,), (1,)), ((), ())), preferred_element_type=F32)
    kx = pkz[:, :KV_DIM]
    sq = kx * kx
    hi = sq.astype(BF16)
    lo = (sq - hi.astype(F32)).astype(BF16)
    ss = _dot(hi, bd_ref[...]) + _dot(lo, bd_ref[...])

    xbc = px[:, :SSM_CONV_DIM]
    sc_ref[0:SUBLANES] = jnp.where(i > 0, xbc[:SUBLANES], 0.0)
    sc_ref[SUBLANES:SUBLANES + tm] = xbc[SUBLANES:SUBLANES + tm]
    sc_ref[SUBLANES + tm:] = jnp.where(i < last, xbc[SUBLANES + tm:], 0.0)
    w = cw_ref[...]
    y = (sc_ref[SUBLANES - 1:SUBLANES - 1 + tm] * w[0:1] + sc_ref[SUBLANES:SUBLANES + tm] * w[1:2]
         + sc_ref[SUBLANES + 1:SUBLANES + 1 + tm] * w[2:3] + cb_ref[...])
    u = _silu(y)
    bc = SSM_D_INNER + SSM_GROUPS * SSM_STATE
    xst_ref[0] = u[:, :SSM_D_INNER].T
    bn_ref[0] = u[:, SSM_D_INNER:bc].astype(BF16)
    ct_ref[0] = u[:, bc:].T.astype(BF16)
    d = px[SUBLANES:SUBLANES + tm, SSM_CONV_DIM:].T[:2 * SSM_HEADS]
    d = d + jnp.concatenate([dtb_ref[...]] * (tm // LANES), axis=1)
    dtt_ref[0] = jnp.maximum(d, 0.0) + jnp.log1p(jnp.exp(-jnp.abs(d)))

    rs = lax.rsqrt(ss * (1.0 / HEAD_DIM) + EPS)
    lane = lax.broadcasted_iota(jnp.int32, kx.shape, 1)
    swap = jnp.where((lane & half) == 0, pltpu.roll(kx, KV_DIM - half, 1), pltpu.roll(kx, half, 1))
    k_ref[0] = (rs * (kx * ck_ref[...] + swap * sk_ref[...])).astype(BF16)

    q3 = pt[:Q_DIM].reshape(ATTN_HEADS, HEAD_DIM, tm)
    qrs = lax.rsqrt(jnp.mean(q3 * q3, axis=1, keepdims=True) + EPS)
    qsw = jnp.concatenate([q3[:, half:], q3[:, :half]], axis=1)
    qo = qrs * (q3 * aq_ref[...][None] + qsw * bq_ref[...][None])
    qt_ref[0] = qo.reshape(Q_DIM, tm).astype(BF16)

    vt_ref[0] = pt[Q_DIM:].astype(BF16)
    z_ref[0] = pkz[:, KV_DIM:]


def _inproj(x, mod, per_batch_mod, nw, wx, wkz, wt, aq, bq, ck, sk, bd, cw, cb, dtb, tm):
    b, t, d = x.shape
    r8 = tm // SUBLANES
    nblk8 = t // SUBLANES
    gn = SSM_GROUPS * SSM_STATE
    mod_idx = (lambda bi, i: (bi, 0, 0)) if per_batch_mod else (lambda bi, i: (0, 0, 0))
    row = lambda bi, i: (bi, i, 0)
    col = lambda bi, i: (bi, 0, i)
    return pl.pallas_call(
        _inproj_kernel,
        grid=(b, t // tm),
        in_specs=[
            pl.BlockSpec((1, tm, d), row),
            pl.BlockSpec((1, SUBLANES, d), lambda bi, i: (bi, jnp.maximum(i * r8 - 1, 0), 0)),
            pl.BlockSpec((1, SUBLANES, d), lambda bi, i: (bi, jnp.minimum((i + 1) * r8, nblk8 - 1), 0)),
            pl.BlockSpec((1, N_MOD, d), mod_idx),
            _const_spec((1, d)),
            _const_spec((d, XBCDT_W)),
            _const_spec((d, KV_DIM + SSM_D_INNER)),
            _const_spec((Q_DIM + KV_DIM, d)),
            pl.BlockSpec((HEAD_DIM, tm), lambda bi, i: (0, i)),
            pl.BlockSpec((HEAD_DIM, tm), lambda bi, i: (0, i)),
            pl.BlockSpec((tm, KV_DIM), lambda bi, i: (i, 0)),
            pl.BlockSpec((tm, KV_DIM), lambda bi, i: (i, 0)),
            _const_spec((KV_DIM, KV_DIM)),
            _const_spec((3, SSM_CONV_DIM)),
            _const_spec((1, SSM_CONV_DIM)),
            _const_spec((2 * SSM_HEADS, LANES)),
        ],
        out_specs=[
            pl.BlockSpec((1, Q_DIM, tm), col),
            pl.BlockSpec((1, KV_DIM, tm), col),
            pl.BlockSpec((1, tm, KV_DIM), row),
            pl.BlockSpec((1, tm, SSM_D_INNER), row),
            pl.BlockSpec((1, SSM_D_INNER, tm), col),
            pl.BlockSpec((1, tm, gn), row),
            pl.BlockSpec((1, gn, tm), col),
            pl.BlockSpec((1, 2 * SSM_HEADS, tm), col),
        ],
        out_shape=[
            jax.ShapeDtypeStruct((b, Q_DIM, t), BF16),
            jax.ShapeDtypeStruct((b, KV_DIM, t), BF16),
            jax.ShapeDtypeStruct((b, t, KV_DIM), BF16),
            jax.ShapeDtypeStruct((b, t, SSM_D_INNER), F32),
            jax.ShapeDtypeStruct((b, SSM_D_INNER, t), F32),
            jax.ShapeDtypeStruct((b, t, gn), BF16),
            jax.ShapeDtypeStruct((b, gn, t), BF16),
            jax.ShapeDtypeStruct((b, 2 * SSM_HEADS, t), F32),
        ],
        scratch_shapes=[pltpu.VMEM((tm + 2 * SUBLANES, SSM_CONV_DIM), F32)],
        compiler_params=_cparams(2),
        name="inproj",
    )(x, x, x, mod, nw, wx, wkz, wt, aq, bq, ck, sk, bd, cw, cb, dtb)


def _attn_kernel(qt_ref, k_ref, vt_ref, o_ref, s0_ref, s1_ref, p0_ref, p1_ref, m0_ref, m1_ref, *, nq):
    g = pl.program_id(1)
    i = pl.program_id(2)
    tq = qt_ref.shape[2]
    cols = KV_GROUP * tq
    nk, kt = s0_ref.shape[0], s0_ref.shape[1]
    s_refs, m_refs, p_refs = (s0_ref, s1_ref), (m0_ref, m1_ref), (p0_ref, p1_ref)

    def run(a_buf, b_buf):
        do_a, do_bc = a_buf is not None, b_buf is not None
        if do_a:
            q4 = qt_ref[0]
            qcat = jnp.concatenate([q4[h * HEAD_DIM:(h + 1) * HEAD_DIM] for h in range(KV_GROUP)], axis=1)
            qp = jnp.concatenate([qcat] * ATTN_KV_HEADS, axis=0)
            rowgrp = lax.broadcasted_iota(jnp.int32, qp.shape, 0) // HEAD_DIM
            qp = jnp.where(rowgrp == g, qp, jnp.zeros_like(qp))
        if do_bc:
            m8 = jnp.broadcast_to(jnp.max(m_refs[b_buf][...], axis=0, keepdims=True), (SUBLANES, cols))

        def stage_a(j, mx):
            ktile = k_ref[0, pl.ds(pl.multiple_of(j * kt, kt), kt), :]
            s = _dot(ktile, qp)
            s_refs[a_buf][j] = s
            return jnp.maximum(mx, jnp.max(s.reshape(kt // SUBLANES, SUBLANES, cols), axis=0))

        def stage_b(j, par, ls):
            s = s_refs[b_buf][j].reshape(kt // SUBLANES, SUBLANES, cols)
            p = jnp.exp2(s - m8[None])
            p_refs[par][...] = p.reshape(kt, cols).astype(BF16)
            return ls + jnp.sum(p, axis=0)

        def stage_c(j, par, acc):
            return acc + _dot(vt_ref[0, 0, j], p_refs[par][...])

        def tile(j, par, carry):
            mx, ls, acc = carry
            if do_a:
                mx = stage_a(j, mx)
            if do_bc:
                acc = stage_c(j - 1, 1 - par, acc)
                ls = stage_b(j, par, ls)
            return mx, ls, acc

        mx = jnp.full((SUBLANES, cols), NEG_BIG, F32)
        ls = jnp.zeros((SUBLANES, cols), F32)
        acc = jnp.zeros((HEAD_DIM, cols), F32)
        if do_a:
            mx = stage_a(0, mx)
        if do_bc:
            ls = stage_b(0, 0, ls)

        def body(t, carry):
            j = 1 + 2 * t
            return tile(j + 1, 0, tile(j, 1, carry))

        carry = lax.fori_loop(0, (nk - 1) // 2, body, (mx, ls, acc), unroll=True)
        if (nk - 1) % 2:
            carry = tile(nk - 1, (nk - 1) % 2, carry)
        mx, ls, acc = carry
        if do_a:
            m_refs[a_buf][...] = mx
        if do_bc:
            acc = stage_c(nk - 1, (nk - 1) % 2, acc)
            o = acc * (1.0 / jnp.sum(ls, axis=0, keepdims=True))
            o4 = jnp.concatenate([o[:, h * tq:(h + 1) * tq] for h in range(KV_GROUP)], axis=0)
            o_ref[0] = o4.T.astype(BF16)

    @pl.when(i == 0)
    def _():
        run(0, None)

    inner = jnp.logical_and(i > 0, i < nq)

    @pl.when(jnp.logical_and(inner, i % 2 == 1))
    def _():
        run(1, 0)

    @pl.when(jnp.logical_and(inner, i % 2 == 0))
    def _():
        run(0, 1)

    @pl.when(i == nq)
    def _():
        run(None, (nq - 1) % 2)


def _attention(qt, k_all, vt_tiles, tq):
    b, _, t = qt.shape
    tk = k_all.shape[1]
    nk, kt = vt_tiles.shape[2], vt_tiles.shape[4]
    nq = t // tq
    gw = KV_GROUP * HEAD_DIM
    cols = KV_GROUP * tq
    return pl.pallas_call(
        functools.partial(_attn_kernel, nq=nq),
        grid=(b, ATTN_KV_HEADS, nq + 1),
        in_specs=[
            pl.BlockSpec((1, gw, tq), lambda bi, g, i: (bi, g, jnp.minimum(i, nq - 1))),
            pl.BlockSpec((1, tk, KV_DIM), lambda bi, g, i: (bi, 0, 0)),
            pl.BlockSpec((1, 1, nk, HEAD_DIM, kt), lambda bi, g, i: (bi, g, 0, 0, 0)),
        ],
        out_specs=pl.BlockSpec((1, tq, gw), lambda bi, g, i: (bi, jnp.maximum(i - 1, 0), g)),
        out_shape=jax.ShapeDtypeStruct((b, t, Q_DIM), BF16),
        scratch_shapes=(
            [pltpu.VMEM((nk, kt, cols), F32)] * 2
            + [pltpu.VMEM((kt, cols), BF16)] * 2
            + [pltpu.VMEM((SUBLANES, cols), F32)] * 2
        ),
        compiler_params=pltpu.CompilerParams(dimension_semantics=("parallel", "parallel", "arbitrary"),
                                             vmem_limit_bytes=VMEM_LIMIT),
        name="attn",
    )(qt, k_all, vt_tiles)


def _attn_bounded_kernel(qt_ref, k_ref, vt_ref, km_ref, o_ref, *p_refs):
    g = pl.program_id(1)
    cols = p_refs[0].shape[1]
    tq = cols // KV_GROUP
    n_blk = qt_ref.shape[2] // tq
    sub = p_refs[0].shape[0]
    n_all = k_ref.shape[1] // sub

    kmax = jnp.concatenate([km_ref[0:1, :]] * (cols // LANES), axis=1)
    qps, mb8s = [], []
    for blk in range(n_blk):
        q4 = qt_ref[0, :, blk * tq:(blk + 1) * tq]
        qf = q4.astype(F32)
        qn = jnp.sqrt(jnp.sum((qf * qf).reshape(KV_GROUP, HEAD_DIM, tq), axis=1))
        qn = jnp.concatenate([qn[h:h + 1] for h in range(KV_GROUP)], axis=1)
        mb8s.append(jnp.broadcast_to(qn * kmax * SCORE_BOUND_SLACK, (SUBLANES, cols)))
        qcat = jnp.concatenate([q4[h * HEAD_DIM:(h + 1) * HEAD_DIM] for h in range(KV_GROUP)], axis=1)
        qp = jnp.concatenate([qcat] * ATTN_KV_HEADS, axis=0)
        rowgrp = lax.broadcasted_iota(jnp.int32, qp.shape, 0) // HEAD_DIM
        qps.append(jnp.where(rowgrp == g, qp, jnp.zeros_like(qp)))

    ring = len(p_refs)
    lag = ring - 1
    ls = [jnp.zeros((SUBLANES, cols), F32) for _ in range(n_blk)]
    acc = [jnp.zeros((HEAD_DIM, cols), F32) for _ in range(n_blk)]

    def pv(w):
        blk, u = divmod(w, n_all)
        acc[blk] = acc[blk] + _dot(vt_ref[0, :, u * sub:(u + 1) * sub], p_refs[w % ring][...])
        if u == n_all - 1:
            o = acc[blk] * (1.0 / jnp.sum(ls[blk], axis=0, keepdims=True))
            o4 = jnp.concatenate([o[:, h * tq:(h + 1) * tq] for h in range(KV_GROUP)], axis=0)
            o_ref[0, blk * tq:(blk + 1) * tq, :] = o4.T.astype(BF16)

    n_items = n_blk * n_all
    for w in range(n_items):
        blk, u = divmod(w, n_all)
        s = _dot(k_ref[0, u * sub:(u + 1) * sub, :], qps[blk])
        if w >= lag:
            pv(w - lag)
        p = jnp.exp2(s.reshape(sub // SUBLANES, SUBLANES, cols) - mb8s[blk][None])
        p_refs[w % ring][...] = p.reshape(sub, cols).astype(BF16)
        ls[blk] = ls[blk] + jnp.sum(p, axis=0)
    for w in range(max(n_items - lag, 0), n_items):
        pv(w)


def _attention_bounded(qt, k_all, vt_all, kmax, tq):
    b, _, t = qt.shape
    tk = k_all.shape[1]
    sub = next(c for c in (ATTN_SUB_TILE, LANES) if tk % c == 0)
    gw = KV_GROUP * HEAD_DIM
    cols = KV_GROUP * tq
    tstep = tq * (ATTN_Q_BLOCKS if t % (tq * ATTN_Q_BLOCKS) == 0 else 1)
    return pl.pallas_call(
        _attn_bounded_kernel,
        grid=(b, ATTN_KV_HEADS, t // tstep),
        in_specs=[
            pl.BlockSpec((1, gw, tstep), lambda bi, g, i: (bi, g, i)),
            pl.BlockSpec((1, tk, KV_DIM), lambda bi, g, i: (bi, 0, 0)),
            pl.BlockSpec((1, HEAD_DIM, tk), lambda bi, g, i: (bi, g, 0)),
            _const_spec((SUBLANES, LANES)),
        ],
        out_specs=pl.BlockSpec((1, tstep, gw), lambda bi, g, i: (bi, i, g)),
        out_shape=jax.ShapeDtypeStruct((b, t, Q_DIM), BF16),
        scratch_shapes=[pltpu.VMEM((sub, cols), BF16)] * (ATTN_PV_LAG + 1),
        compiler_params=_cparams(3),
        name="attn_bounded",
    )(qt, k_all, vt_all, jnp.full((SUBLANES, LANES), kmax, F32))


def _attention_exact(qt, k_all, vt_all, kmax, tq):
    del kmax
    b, _, tk = vt_all.shape
    kt = next(c for c in ATTN_KEY_TILES if tk % c == 0)
    vt_tiles = vt_all.reshape(b, ATTN_KV_HEADS, HEAD_DIM, tk // kt, kt).transpose(0, 1, 3, 2, 4)
    return _attention(qt, k_all, vt_tiles, tq)


class _SsdChunk:
    def __init__(self, xst, bn, ct, dtt, a_col, direction):
        self.xst, self.bn, self.ct, self.dtt, self.a_col, self.direction = xst, bn, ct, dtt, a_col, direction

    def decay(self):
        ln = self.xst.shape[1]
        da = self.dtt * self.a_col
        ri = lax.broadcasted_iota(jnp.int32, (ln, ln), 0)
        ci = lax.broadcasted_iota(jnp.int32, (ln, ln), 1)
        self.reach = (ci >= ri) if self.direction == 0 else (ci <= ri)
        self.cs_row = _dot_exact(da, self.reach.astype(F32))
        self.tot = _dot_exact(da, jnp.ones((ln, ln), F32))

    def local(self):
        ln = self.xst.shape[1]
        gn = SSM_GROUPS * SSM_STATE
        cs_col = jnp.concatenate([self.cs_row, jnp.zeros((ln - 2 * SSM_HEADS, ln), F32)], axis=0).T
        rowgrp = lax.broadcasted_iota(jnp.int32, (gn, ln), 0) // SSM_STATE
        self.ctz, self.y_diag, self.xw, self.e_row, self.e_tot = [], [], [], [], []
        for grp in range(SSM_GROUPS):
            ctz = jnp.where(rowgrp == grp, self.ct, jnp.zeros_like(self.ct))
            cbt = _dot(self.bn, ctz)
            y_parts, xw_parts = [], []
            for e in range(SSM_HEADS_PER_GROUP):
                hd = grp * SSM_HEADS_PER_GROUP + e
                c = self.direction * SSM_HEADS + hd
                row = self.cs_row[c:c + 1, :]
                lmt = jnp.exp2(jnp.where(self.reach, row - cs_col[:, c:c + 1], NEG_BIG))
                mt = (cbt * lmt).astype(BF16)
                xdt = self.xst[hd * SSM_HEAD_DIM:(hd + 1) * SSM_HEAD_DIM, :] * self.dtt[c:c + 1, :]
                y_parts.append(_dot(xdt.astype(BF16), mt))
                xw_parts.append((xdt * jnp.exp2(self.tot[c:c + 1, :] - row)).astype(BF16))
                self.e_row.append(jnp.exp2(row))
                self.e_tot.append(jnp.exp2(self.tot[c:c + 1, :gn]))
            self.ctz.append(ctz)
            self.y_diag.append(y_parts)
            self.xw.append(jnp.concatenate(xw_parts, axis=0))

    def carry(self, h_ref):
        gp = SSM_HEADS_PER_GROUP * SSM_HEAD_DIM
        ys = []
        for grp in range(SSM_GROUPS):
            hg = h_ref[grp * gp:(grp + 1) * gp, :]
            y_off = _dot(hg.astype(BF16), self.ctz[grp])
            st = _dot(self.xw[grp], self.bn)
            h_parts = []
            for e in range(SSM_HEADS_PER_GROUP):
                hd = grp * SSM_HEADS_PER_GROUP + e
                rows = slice(e * SSM_HEAD_DIM, (e + 1) * SSM_HEAD_DIM)
                ys.append(self.y_diag[grp][e] + y_off[rows] * self.e_row[hd])
                h_parts.append(hg[rows] * self.e_tot[hd])
            h_ref[grp * gp:(grp + 1) * gp, :] = jnp.concatenate(h_parts, axis=0) + st
        return jnp.concatenate(ys, axis=0)


def _ssd_run(chunks, h_refs):
    for ch in chunks:
        ch.decay()
    for ch in chunks:
        ch.local()
    return [ch.carry(h_refs[ch.direction]) for ch in chunks]


def _ssd_kernel(xf_ref, xb_ref, bf_ref, bb_ref, cf_ref, cb_ref, df_ref, db_ref, xc_ref, bc_ref, cc_ref, dc_ref,
                alog_ref, dsk_ref, yf_ref, yb_ref, hf_ref, hb_ref):
    c = pl.program_id(1)
    a_col = -jnp.exp(alog_ref[...]) * LOG2_E
    ln = SSM_CHUNK
    h_refs = (hf_ref, hb_ref)

    def chunk(x_ref, b_ref, c_ref, d_ref, s, direction):
        tok = slice(s * ln, (s + 1) * ln)
        return _SsdChunk(x_ref[0, :, tok], b_ref[0, tok, :], c_ref[0, :, tok], d_ref[0, :, tok], a_col, direction)

    @pl.when(c == 0)
    def _():
        n_ctx = xc_ref.shape[2] // ln
        for direction in range(2):
            h_refs[direction][...] = jnp.zeros_like(h_refs[direction])
        chunks = []
        for k in range(n_ctx):
            chunks.append(chunk(xc_ref, bc_ref, cc_ref, dc_ref, k, 0))
            chunks.append(chunk(xc_ref, bc_ref, cc_ref, dc_ref, n_ctx - 1 - k, 1))
        _ssd_run(chunks, h_refs)

    n_sub = xf_ref.shape[2] // ln
    chunks = []
    for k in range(n_sub):
        chunks.append(chunk(xf_ref, bf_ref, cf_ref, df_ref, k, 0))
        chunks.append(chunk(xb_ref, bb_ref, cb_ref, db_ref, n_sub - 1 - k, 1))
    ys = _ssd_run(chunks, h_refs)
    for k in range(n_sub):
        tok = slice(k * ln, (k + 1) * ln)
        yf_ref[0, :, tok] = ys[2 * k] + xf_ref[0, :, tok] * dsk_ref[...]
        tok = slice((n_sub - 1 - k) * ln, (n_sub - k) * ln)
        yb_ref[0, :, tok] = ys[2 * k + 1]


def _ssd(lat, ctx, alog_col, dsk_col, toks):
    xst, bn, ct, dtt = lat
    b, di, t = xst.shape
    gn = bn.shape[2]
    nh2 = dtt.shape[1]
    nctx = ctx[0].shape[2]
    ns = t // toks
    fcol = lambda bi, c: (bi, 0, c)
    bcol = lambda bi, c: (bi, 0, ns - 1 - c)
    frow = lambda bi, c: (bi, c, 0)
    brow = lambda bi, c: (bi, ns - 1 - c, 0)
    whole = lambda bi, c: (bi, 0, 0)
    return pl.pallas_call(
        _ssd_kernel,
        grid=(b, ns),
        in_specs=[
            pl.BlockSpec((1, di, toks), fcol), pl.BlockSpec((1, di, toks), bcol),
            pl.BlockSpec((1, toks, gn), frow), pl.BlockSpec((1, toks, gn), brow),
            pl.BlockSpec((1, gn, toks), fcol), pl.BlockSpec((1, gn, toks), bcol),
            pl.BlockSpec((1, nh2, toks), fcol), pl.BlockSpec((1, nh2, toks), bcol),
            pl.BlockSpec((1, di, nctx), whole), pl.BlockSpec((1, nctx, gn), whole),
            pl.BlockSpec((1, gn, nctx), whole), pl.BlockSpec((1, nh2, nctx), whole),
            _const_spec((nh2, SSM_CHUNK)),
            _const_spec((di, SSM_CHUNK)),
        ],
        out_specs=[pl.BlockSpec((1, di, toks), fcol), pl.BlockSpec((1, di, toks), bcol)],
        out_shape=[jax.ShapeDtypeStruct((b, di, t), F32), jax.ShapeDtypeStruct((b, di, t), F32)],
        scratch_shapes=[pltpu.VMEM((di, gn), F32)] * 2,
        compiler_params=pltpu.CompilerParams(dimension_semantics=("parallel", "arbitrary"),
                                             vmem_limit_bytes=VMEM_LIMIT),
        name="ssd",
    )(xst, xst, bn, bn, ct, ct, dtt, dtt, *ctx, alog_col, dsk_col)


def _row_groups(tm):
    n = ROW_GROUPS if tm % (ROW_GROUPS * SUBLANES * 2) == 0 else 1
    return [slice(i * (tm // n), (i + 1) * (tm // n)) for i in range(n)]


def _ffn_tail(x1s, m, nw, wg_ref, wu_ref, wd_ref):
    scale = nw * (1.0 + m[4:5])
    hs = [(_rms(x1) * scale + m[3:4]).astype(BF16) for x1 in x1s]
    ffn = wg_ref.shape[1]
    fs = [None] * len(x1s)
    for c0 in range(0, ffn, FFN_CHUNK):
        c1 = min(c0 + FFN_CHUNK, ffn)
        hids = [(_silu(_dot(h, wg_ref[:, c0:c1])) * _dot(h, wu_ref[:, c0:c1])).astype(BF16) for h in hs]
        for i, hid in enumerate(hids):
            part = _dot(hid, wd_ref[c0:c1, :])
            fs[i] = part if fs[i] is None else fs[i] + part
    return [x1 + m[5:6] * f for x1, f in zip(x1s, fs)]


def _mix0_kernel(x_ref, a_ref, yf_ref, yb_ref, z_ref, mod_ref, snw_ref, nw_ref,
                 wo_ref, wg_ref, wu_ref, wd_ref, o_ref):
    m = mod_ref[0]
    gw = SSM_D_INNER // SSM_GROUPS
    groups = _row_groups(x_ref.shape[1])
    yns = []
    for rows in groups:
        y = (yf_ref[0, :, rows] + yb_ref[0, :, rows]).T
        gy = y * _silu(z_ref[0, rows, :])
        gn = jnp.concatenate([_rms(gy[:, i * gw:(i + 1) * gw]) for i in range(SSM_GROUPS)], axis=1)
        yns.append((gn * snw_ref[...]).astype(BF16))
    x1s = []
    for rows, yn in zip(groups, yns):
        mix = _dot(jnp.concatenate([a_ref[0, rows, :], yn], axis=1), wo_ref[...])
        x1s.append(x_ref[0, rows, :] + m[2:3] * mix)
    for rows, out in zip(groups, _ffn_tail(x1s, m, nw_ref[...], wg_ref, wu_ref, wd_ref)):
        o_ref[0, rows, :] = out


def _mix0(x, a, yf, yb, z, mod, snw, nw, wo, wg, wu, wd, tm):
    b, t, d = x.shape
    ffn = wg.shape[2]
    row = lambda bi, i: (bi, i, 0)
    col = lambda bi, i: (bi, 0, i)
    return pl.pallas_call(
        _mix0_kernel,
        grid=(b, t // tm),
        in_specs=[
            pl.BlockSpec((1, tm, d), row),
            pl.BlockSpec((1, tm, Q_DIM), row),
            pl.BlockSpec((1, SSM_D_INNER, tm), col),
            pl.BlockSpec((1, SSM_D_INNER, tm), col),
            pl.BlockSpec((1, tm, SSM_D_INNER), row),
            pl.BlockSpec((1, N_MOD, d), lambda bi, i: (bi, 0, 0)),
            _const_spec((1, SSM_D_INNER)),
            _const_spec((1, d)),
            _const_spec((Q_DIM + SSM_D_INNER, d)),
            _layer_spec((d, ffn), 0),
            _layer_spec((d, ffn), 0),
            _layer_spec((ffn, d), 0),
        ],
        out_specs=pl.BlockSpec((1, tm, d), row),
        out_shape=jax.ShapeDtypeStruct((b, t, d), F32),
        compiler_params=_cparams(2),
        name="mix0",
    )(x, a, yf, yb, z, mod, snw, nw, wo, wg, wu, wd)


def _layer1_kernel(x_ref, xp_ref, xn_ref, mod_ref, nmix_ref, win_ref, cw_ref, wout_ref, nffn_ref,
                   wg_ref, wu_ref, wd_ref, fn_ref, o_ref, v_ref):
    i = pl.program_id(1)
    last = pl.num_programs(1) - 1
    m = mod_ref[0]
    tm, d = x_ref.shape[1], x_ref.shape[2]
    x = x_ref[0]
    xa = jnp.concatenate([xp_ref[0], x, xn_ref[0]], axis=0)
    h = _rms(xa) * (nmix_ref[...] * (1.0 + m[1:2])) + m[0:1]
    pcu = _dot(h.astype(BF16), win_ref[:, d:])
    pgb = _dot(h[SUBLANES:SUBLANES + tm].astype(BF16), win_ref[:, :d])
    v = pcu[:, :d] * pcu[:, d:]
    v_ref[0:SUBLANES] = jnp.where(i > 0, v[:SUBLANES], 0.0)
    v_ref[SUBLANES:SUBLANES + tm] = v[SUBLANES:SUBLANES + tm]
    v_ref[SUBLANES + tm:] = jnp.where(i < last, v[SUBLANES + tm:], 0.0)
    cw = cw_ref[...]
    groups = _row_groups(tm)
    x1s = []
    for rows in groups:
        r0 = rows.start + SUBLANES
        n = rows.stop - rows.start
        conv = (v_ref[r0 - 1:r0 - 1 + n] * cw[0:1] + v_ref[r0:r0 + n] * cw[1:2] + v_ref[r0 + 1:r0 + 1 + n] * cw[2:3])
        gated = (pgb[rows] * conv).astype(BF16)
        x1s.append(x[rows] + m[2:3] * _dot(gated, wout_ref[...]))
    for rows, x2 in zip(groups, _ffn_tail(x1s, m, nffn_ref[...], wg_ref, wu_ref, wd_ref)):
        o_ref[0, rows, :] = _rms(x2) * fn_ref[...]


def _layer1(x, mod, nmix, win, cw, wout, nffn, wg, wu, wd, fn, tm):
    b, t, d = x.shape
    ffn = wg.shape[2]
    r8 = tm // SUBLANES
    nblk8 = t // SUBLANES
    return pl.pallas_call(
        _layer1_kernel,
        grid=(b, t // tm),
        in_specs=[
            pl.BlockSpec((1, tm, d), lambda bi, i: (bi, i, 0)),
            pl.BlockSpec((1, SUBLANES, d), lambda bi, i: (bi, jnp.maximum(i * r8 - 1, 0), 0)),
            pl.BlockSpec((1, SUBLANES, d), lambda bi, i: (bi, jnp.minimum((i + 1) * r8, nblk8 - 1), 0)),
            pl.BlockSpec((1, N_MOD, d), lambda bi, i: (bi, 0, 0)),
            _const_spec((1, d)),
            _const_spec((d, 3 * d)),
            _const_spec((3, d)),
            _const_spec((d, d)),
            _const_spec((1, d)),
            _layer_spec((d, ffn), 1),
            _layer_spec((d, ffn), 1),
            _layer_spec((ffn, d), 1),
            _const_spec((1, d)),
        ],
        out_specs=pl.BlockSpec((1, tm, d), lambda bi, i: (bi, i, 0)),
        out_shape=jax.ShapeDtypeStruct((b, t, d), F32),
        scratch_shapes=[pltpu.VMEM((tm + 2 * SUBLANES, d), F32)],
        compiler_params=_cparams(2),
        name="layer1",
    )(x, x, x, mod, nmix, win, cw, wout, nffn, wg, wu, wd, fn)


def _rope_tables(n_tokens):
    rows = n_tokens // GRID_W
    row = jnp.repeat(jnp.arange(rows), GRID_W).astype(F32)
    col = jnp.tile(jnp.arange(GRID_W), rows).astype(F32)
    inv = 1.0 / (ROPE_THETA ** (jnp.arange(0, AXIS_DIM, 2, dtype=F32) / AXIS_DIM))
    ang = jnp.concatenate([row[:, None] * inv, col[:, None] * inv], axis=-1)
    return jnp.cos(ang), jnp.sin(ang)


_HEAD_PERM = np.concatenate([np.arange(0, HEAD_DIM, 2), np.arange(1, HEAD_DIM, 2)])


def _qk_tables(cos, sin, q_gain, k_gain):
    half = HEAD_DIM // 2
    sign = jnp.concatenate([-jnp.ones((half,), F32), jnp.ones((half,), F32)])
    c64 = jnp.concatenate([cos, cos], axis=1)
    s64 = jnp.concatenate([sin, sin], axis=1) * sign
    swap = np.concatenate([np.arange(half, HEAD_DIM), np.arange(0, half)])

    def tables(gain, scale):
        gp = gain[_HEAD_PERM] * scale
        return c64 * gp[None, :], s64 * gp[swap][None, :]

    aq, bq = tables(q_gain, HEAD_DIM ** -0.5 * LOG2_E)
    ak, bk = tables(k_gain, 1.0)
    return aq.T, bq.T, jnp.tile(ak, (1, ATTN_KV_HEADS)), jnp.tile(bk, (1, ATTN_KV_HEADS))


def kernel(x, c, ctx, c_ctx, ada_w, ada_b, norm_mix, norm_ffn, ffn_w_gate, ffn_w_up, ffn_w_down, hy_w_in,
           hy_q_norm, hy_k_norm, hy_conv_w, hy_conv_b, hy_dt_bias, hy_a_log, hy_d_skip, hy_ssm_norm, hy_w_out,
           sc_w_in, sc_conv_w, sc_w_out, final_norm):
    b, t, d = x.shape
    nctx = ctx.shape[1]
    tm_in = min(1024, t)
    tm_mix = min(512, t)

    n_rows = -(-(b + 1) // SUBLANES) * SUBLANES
    cc = jnp.zeros((n_rows, d), F32).at[:b].set(c).at[b].set(c_ctx)
    mods = _modulation(cc, ada_w, ada_b).reshape(ada_w.shape[0], n_rows, N_MOD, d)
    mod0, mod0_ctx, mod1 = mods[0, :b], mods[0, b:b + 1], mods[1, :b]

    w_in = hy_w_in[0]
    idx_k, idx_v, idx_z = Q_DIM, Q_DIM + KV_DIM, Q_DIM + 2 * KV_DIM
    idx_xbc = idx_z + SSM_D_INNER
    q_cols = (np.arange(ATTN_HEADS)[:, None] * HEAD_DIM + _HEAD_PERM[None, :]).reshape(-1)
    k_cols = idx_k + (np.arange(ATTN_KV_HEADS)[:, None] * HEAD_DIM + _HEAD_PERM[None, :]).reshape(-1)
    dt_pad = jnp.zeros((d, LANES - 2 * SSM_HEADS), F32)
    wkz = jnp.concatenate([w_in[:, k_cols], w_in[:, idx_z:idx_xbc]], axis=1).astype(BF16)
    wx = jnp.concatenate([w_in[:, idx_xbc:], dt_pad], axis=1).astype(BF16)
    wt = jnp.concatenate([w_in[:, q_cols], w_in[:, idx_v:idx_z]], axis=1).T.astype(BF16)
    nw0 = norm_mix[0][None, :]

    cos, sin = _rope_tables(t)
    aq, bq, ck, sk = _qk_tables(cos, sin, hy_q_norm[0], hy_k_norm[0])
    ones_c, zeros_c = jnp.ones((nctx, AXIS_DIM), F32), jnp.zeros((nctx, AXIS_DIM), F32)
    aq_c, bq_c, ck_c, sk_c = _qk_tables(ones_c, zeros_c, hy_q_norm[0], hy_k_norm[0])
    seg = np.arange(KV_DIM) // HEAD_DIM
    bd = jnp.asarray(seg[:, None] == seg[None, :], BF16)

    conv_w, conv_b = hy_conv_w[0], hy_conv_b[0][None, :]
    dtb = jnp.broadcast_to(hy_dt_bias[0].reshape(-1, 1), (2 * SSM_HEADS, LANES))
    shared = (nw0, wx, wkz, wt)
    qt, vt, k, z, *ssd_lat = _inproj(x, mod0, True, *shared, aq, bq, ck, sk, bd, conv_w, conv_b, dtb, tm_in)
    _, vct, kc, _, *ssd_ctx = _inproj(ctx, mod0_ctx, False, *shared, aq_c, bq_c, ck_c, sk_c, bd, conv_w, conv_b, dtb,
                                      nctx)

    k_all = jnp.concatenate([k, kc], axis=1)
    vt_all = jnp.concatenate([vt, vct], axis=2)
    kmax = 1.01 * HEAD_DIM ** 0.5 * jnp.max(jnp.abs(hy_k_norm[0]))
    qmax = 1.01 * LOG2_E * jnp.max(jnp.abs(hy_q_norm[0]))
    score_bound = qmax * kmax
    a_lat = lax.cond(score_bound <= SAFE_SCORE_BOUND,
                     functools.partial(_attention_bounded, tq=min(ATTN_TQ_BOUNDED, t)),
                     functools.partial(_attention_exact, tq=min(ATTN_TQ_EXACT, t)),
                     qt, k_all, vt_all, kmax)

    alog_col = jnp.broadcast_to(hy_a_log[0].reshape(-1, 1), (2 * SSM_HEADS, SSM_CHUNK))
    dsk_col = jnp.broadcast_to(jnp.repeat(hy_d_skip[0], SSM_HEAD_DIM)[:, None], (SSM_D_INNER, SSM_CHUNK))
    yf, yb = _ssd(ssd_lat, ssd_ctx, alog_col, dsk_col, toks=min(8 * SSM_CHUNK, t))

    wg, wu, wd = ffn_w_gate.astype(BF16), ffn_w_up.astype(BF16), ffn_w_down.astype(BF16)
    x = _mix0(x, a_lat, yf, yb, z, mod0, hy_ssm_norm[0][None, :], norm_ffn[0][None, :], hy_w_out[0].astype(BF16),
              wg, wu, wd, tm_mix)

    return _layer1(x, mod1, norm_mix[1][None, :], sc_w_in[0].astype(BF16), sc_conv_w[0], sc_w_out[0].astype(BF16),
                   norm_ffn[1][None, :], wg, wu, wd, final_norm[None, :], tm_mix)
```
